```python
import math
import jax, jax.numpy as jnp
from jax import lax
import numpy as np

D_MODEL = 1024
BATCH = 16
SEQ = 4096
DEPTH = 4

N_MIXERS = 3
D_FF = 4 * D_MODEL
RMS_EPS = 1e-6
Q_BLOCK = 128

N_BUCKETS = 32
MAX_DISTANCE = 128
N_BIAS_HEADS = 16

A_HEADS = 16
A_HEAD_DIM = 64
IDX_HEADS = 8
IDX_DIM = 64
TOPK_MAX = 256
A_IN = A_HEADS * A_HEAD_DIM + 2 * A_HEAD_DIM + IDX_HEADS * IDX_DIM + IDX_DIM + IDX_HEADS

B_HEADS = 16
B_KV_HEADS = 4
B_HEAD_DIM = 64
WINDOW = 128
B_IN = (B_HEADS + 2 * B_KV_HEADS) * B_HEAD_DIM

C_HEADS = 16
C_Q_RANK = 256
C_KV_RANK = 128
C_NOPE = 64
C_ROPE = 32
C_V = 64
C_QK = C_NOPE + C_ROPE
C_IN = C_Q_RANK + C_KV_RANK + C_ROPE
ROPE_THETA = 10000.0

N_A = (DEPTH + 2) // 3
N_B = (DEPTH + 1) // 3
N_C = DEPTH // 3

kernel_name = "hybrid_dsa_swa_mla_trunk"


def rms_norm(x, g):
    xf = x.astype(jnp.float32)
    y = xf * lax.rsqrt(jnp.mean(xf * xf, axis=-1, keepdims=True) + RMS_EPS)
    return (y * g.astype(jnp.float32)).astype(x.dtype)


def split_cols(t, sizes):
    return jnp.split(t, np.cumsum(sizes)[:-1].tolist(), axis=-1)


def t5_bucket(rel):
    n = jnp.maximum(rel, 0)
    max_exact = N_BUCKETS // 2
    nf = jnp.maximum(n, 1).astype(jnp.float32)
    large = max_exact + (jnp.log(nf / max_exact) / math.log(MAX_DISTANCE / max_exact)
                         * (N_BUCKETS - max_exact)).astype(jnp.int32)
    large = jnp.minimum(large, N_BUCKETS - 1)
    return jnp.where(n < max_exact, n, large)


def to_blocks(t, nb):
    return jnp.moveaxis(t.reshape(t.shape[0], nb, Q_BLOCK, *t.shape[2:]), 1, 0)


def from_blocks(t):
    t = jnp.moveaxis(t, 0, 1)
    return t.reshape(t.shape[0], t.shape[1] * t.shape[2], *t.shape[3:])


def dsa_mixer(h, w_in, q_gain, k_gain, w_out, rel_bias):
    B, L, _ = h.shape
    nb = L // Q_BLOCK
    k_sel = min(TOPK_MAX, L // 4)
    q, k, v, q_idx, k_idx, w_idx = split_cols(
        h @ w_in, [A_HEADS * A_HEAD_DIM, A_HEAD_DIM, A_HEAD_DIM, IDX_HEADS * IDX_DIM, IDX_DIM, IDX_HEADS])
    q = rms_norm(q.reshape(B, L, A_HEADS, A_HEAD_DIM), q_gain)
    k = rms_norm(k, k_gain)
    q_idx = q_idx.reshape(B, L, IDX_HEADS, IDX_DIM)
    w_idx = w_idx * IDX_HEADS ** -0.5
    key_pos = jnp.arange(L, dtype=jnp.int32)
    q_pos = key_pos.reshape(nb, Q_BLOCK)
    scale = A_HEAD_DIM ** -0.5

    def block(args):
        qb, qib, wb, pos = args
        s_idx = jnp.einsum('bqhd,bsd->bqhs', qib, k_idx) * IDX_DIM ** -0.5
        score = jnp.einsum('bqh,bqhs->bqs', wb, jax.nn.relu(s_idx)).astype(jnp.float32)
        admissible = key_pos[None, :] <= pos[:, None]
        score = jnp.where(admissible[None], score, -jnp.inf)
        _, sel = lax.top_k(score, k_sel)
        kg = jax.vmap(lambda kk, ii: kk[ii])(k, sel)
        vg = jax.vmap(lambda vv, ii: vv[ii])(v, sel)
        rel = pos[None, :, None] - sel
        bias = jnp.moveaxis(rel_bias[t5_bucket(rel)], -1, 1)
        logits = jnp.einsum('bqhd,bqjd->bhqj', qb, kg).astype(jnp.float32) * scale + bias
        logits = jnp.where((rel >= 0)[:, None], logits, -jnp.inf)
        p = jax.nn.softmax(logits, axis=-1).astype(vg.dtype)
        return jnp.einsum('bhqj,bqjd->bqhd', p, vg)

    out = lax.map(block, (to_blocks(q, nb), to_blocks(q_idx, nb), to_blocks(w_idx, nb), q_pos))
    return from_blocks(out).reshape(B, L, A_HEADS * A_HEAD_DIM) @ w_out


def swa_mixer(h, w_in, q_gain, k_gain, sinks, w_out, rel_bias):
    B, L, _ = h.shape
    nb = L // WINDOW
    group = B_HEADS // B_KV_HEADS
    q, k, v = split_cols(h @ w_in, [B_HEADS * B_HEAD_DIM, B_KV_HEADS * B_HEAD_DIM, B_KV_HEADS * B_HEAD_DIM])
    q = rms_norm(q.reshape(B, L, B_HEADS, B_HEAD_DIM), q_gain).reshape(B, L, B_KV_HEADS, group, B_HEAD_DIM)
    k = rms_norm(k.reshape(B, L, B_KV_HEADS, B_HEAD_DIM), k_gain)
    v = v.reshape(B, L, B_KV_HEADS, B_HEAD_DIM)

    def band(t):
        tb = t.reshape(B, nb, WINDOW, B_KV_HEADS, B_HEAD_DIM)
        prev = jnp.pad(tb[:, :-1], ((0, 0), (1, 0), (0, 0), (0, 0), (0, 0)))
        return jnp.moveaxis(jnp.concatenate([prev, tb], axis=2), 1, 0)

    kb, vb = band(k), band(v)
    qi = jnp.arange(WINDOW, dtype=jnp.int32)[:, None] + WINDOW
    kj = jnp.arange(2 * WINDOW, dtype=jnp.int32)[None, :]
    rel = qi - kj
    in_band = (rel >= 0) & (rel < WINDOW)
    bias = jnp.transpose(rel_bias[t5_bucket(rel)], (2, 0, 1)).reshape(B_KV_HEADS, group, WINDOW, 2 * WINDOW)
    sink = sinks.astype(jnp.float32).reshape(B_KV_HEADS, group, 1, 1)
    scale = B_HEAD_DIM ** -0.5

    def block(args):
        qb, kbb, vbb, n = args
        logits = jnp.einsum('bqkgd,bjkd->bkgqj', qb, kbb).astype(jnp.float32) * scale + bias
        ok = in_band & ((n > 0) | (kj >= WINDOW))
        logits = jnp.where(ok, logits, -jnp.inf)
        m = jnp.maximum(jnp.max(logits, axis=-1, keepdims=True), sink)
        e = jnp.exp(logits - m)
        p = e / (jnp.sum(e, axis=-1, keepdims=True) + jnp.exp(sink - m))
        return jnp.einsum('bkgqj,bjkd->bqkgd', p.astype(vbb.dtype), vbb)

    out = lax.map(block, (to_blocks(q, nb), kb, vb, jnp.arange(nb, dtype=jnp.int32)))
    return from_blocks(out).reshape(B, L, B_HEADS * B_HEAD_DIM) @ w_out


def apply_rope(t, cos, sin):
    t1, t2 = jnp.split(t, 2, axis=-1)
    return jnp.concatenate([t1 * cos - t2 * sin, t2 * cos + t1 * sin], axis=-1).astype(t.dtype)


def mla_mixer(h, positions, w_in, q_a_gain, w_q_b, kv_a_gain, w_kv_b, q_gain, k_gain, w_out):
    B, L, _ = h.shape
    nb = L // Q_BLOCK
    q_lat, kv_lat, k_rope = split_cols(h @ w_in, [C_Q_RANK, C_KV_RANK, C_ROPE])
    q = (rms_norm(q_lat, q_a_gain) @ w_q_b).reshape(B, L, C_HEADS, C_QK)
    kv = (rms_norm(kv_lat, kv_a_gain) @ w_kv_b).reshape(B, L, C_HEADS, C_NOPE + C_V)
    k_nope, v = split_cols(kv, [C_NOPE, C_V])
    k = jnp.concatenate([k_nope, jnp.broadcast_to(k_rope[:, :, None, :], (B, L, C_HEADS, C_ROPE))], axis=-1)
    q = rms_norm(q, q_gain)
    k = rms_norm(k, k_gain)
    inv_freq = ROPE_THETA ** (-jnp.arange(0, C_ROPE, 2, dtype=jnp.float32) / C_ROPE)
    ang = positions.astype(jnp.float32)[..., None] * inv_freq
    cos, sin = jnp.cos(ang)[:, :, None, :], jnp.sin(ang)[:, :, None, :]
    q = jnp.concatenate([q[..., :C_NOPE], apply_rope(q[..., C_NOPE:], cos, sin)], axis=-1)
    k = jnp.concatenate([k[..., :C_NOPE], apply_rope(k[..., C_NOPE:], cos, sin)], axis=-1)
    key_pos = jnp.arange(L, dtype=jnp.int32)
    scale = C_QK ** -0.5

    def block(args):
        qb, pos = args
        logits = jnp.einsum('bqhd,bshd->bhqs', qb, k).astype(jnp.float32) * scale
        logits = jnp.where(key_pos[None, :] <= pos[:, None], logits, -jnp.inf)
        p = jax.nn.softmax(logits, axis=-1).astype(v.dtype)
        return jnp.einsum('bhqs,bshd->bqhd', p, v)

    out = lax.map(block, (to_blocks(q, nb), key_pos.reshape(nb, Q_BLOCK)))
    return from_blocks(out).reshape(B, L, C_HEADS * C_V) @ w_out


def sqrelu_mlp(h, w_up, w_down):
    return jnp.square(jax.nn.relu(h @ w_up)) @ w_down


def setup_inputs(seed: int = 0) -> dict:
    key = jax.random.key(seed)
    ks = jax.random.split(key, 24)
    f32 = jnp.float32

    def w(k, shape, fan_in):
        return jax.random.normal(k, shape, f32) * fan_in ** -0.5

    def gain(k, shape):
        return 1.0 + 0.02 * jax.random.normal(k, shape, f32)

    x = jax.random.normal(ks[0], (BATCH, SEQ, D_MODEL), f32)
    start = jax.random.randint(ks[1], (BATCH, 1), 0, 1024, dtype=jnp.int32)
    positions = start + jnp.arange(SEQ, dtype=jnp.int32)[None, :]
    return {
        'x': x,
        'positions': positions,
        'rel_bias': 0.5 * jax.random.normal(ks[2], (N_BUCKETS, N_BIAS_HEADS), f32),
        'norm_mix': gain(ks[3], (DEPTH, D_MODEL)),
        'norm_mlp': gain(ks[4], (DEPTH, D_MODEL)),
        'w_up': w(ks[5], (DEPTH, D_MODEL, D_FF), D_MODEL),
        'w_down': w(ks[6], (DEPTH, D_FF, D_MODEL), D_FF),
        'a_w_in': w(ks[7], (N_A, D_MODEL, A_IN), D_MODEL),
        'a_q_gain': gain(ks[8], (N_A, A_HEAD_DIM)),
        'a_k_gain': gain(ks[9], (N_A, A_HEAD_DIM)),
        'a_w_out': w(ks[10], (N_A, A_HEADS * A_HEAD_DIM, D_MODEL), A_HEADS * A_HEAD_DIM),
        'b_w_in': w(ks[11], (N_B, D_MODEL, B_IN), D_MODEL),
        'b_q_gain': gain(ks[12], (N_B, B_HEAD_DIM)),
        'b_k_gain': gain(ks[13], (N_B, B_HEAD_DIM)),
        'b_sinks': jax.random.normal(ks[14], (N_B, B_HEADS), f32),
        'b_w_out': w(ks[15], (N_B, B_HEADS * B_HEAD_DIM, D_MODEL), B_HEADS * B_HEAD_DIM),
        'c_w_in': w(ks[16], (N_C, D_MODEL, C_IN), D_MODEL),
        'c_q_a_gain': gain(ks[17], (N_C, C_Q_RANK)),
        'c_w_q_b': w(ks[18], (N_C, C_Q_RANK, C_HEADS * C_QK), C_Q_RANK),
        'c_kv_a_gain': gain(ks[19], (N_C, C_KV_RANK)),
        'c_w_kv_b': w(ks[20], (N_C, C_KV_RANK, C_HEADS * (C_NOPE + C_V)), C_KV_RANK),
        'c_q_gain': gain(ks[21], (N_C, C_QK)),
        'c_k_gain': gain(ks[22], (N_C, C_QK)),
        'c_w_out': w(ks[23], (N_C, C_HEADS * C_V, D_MODEL), C_HEADS * C_V),
    }


def reference(x, positions, rel_bias, norm_mix, norm_mlp, w_up, w_down,
              a_w_in, a_q_gain, a_k_gain, a_w_out,
              b_w_in, b_q_gain, b_k_gain, b_sinks, b_w_out,
              c_w_in, c_q_a_gain, c_w_q_b, c_kv_a_gain, c_w_kv_b, c_q_gain, c_k_gain, c_w_out):
    ia = ib = ic = 0
    for i in range(DEPTH):
        h = rms_norm(x, norm_mix[i])
        kind = i % N_MIXERS
        if kind == 0:
            y = dsa_mixer(h, a_w_in[ia], a_q_gain[ia], a_k_gain[ia], a_w_out[ia], rel_bias)
            ia += 1
        elif kind == 1:
            y = swa_mixer(h, b_w_in[ib], b_q_gain[ib], b_k_gain[ib], b_sinks[ib], b_w_out[ib], rel_bias)
            ib += 1
        else:
            y = mla_mixer(h, positions, c_w_in[ic], c_q_a_gain[ic], c_w_q_b[ic], c_kv_a_gain[ic],
                          c_w_kv_b[ic], c_q_gain[ic], c_k_gain[ic], c_w_out[ic])
            ic += 1
        x = x + y
        x = x + sqrelu_mlp(rms_norm(x, norm_mlp[i]), w_up[i], w_down[i])
    return x
```

```python
import functools
import math

import numpy as np
import jax
import jax.numpy as jnp
from jax import lax
from jax.experimental import pallas as pl
from jax.experimental.pallas import tpu as pltpu

F32 = jnp.float32
MXU_DTYPE = jnp.bfloat16

D_MODEL = 1024
D_FF = 4 * D_MODEL
RMS_EPS = 1e-6
N_BUCKETS = 32
MAX_DISTANCE = 128
HEADS = 16
HEAD_DIM = 64
PAIRS = HEADS // 2
LANES = 128
BLK = 128
IDX_HEADS = 8
IDX_DIM = 64
TOPK_MAX = 256
B_KV_HEADS = 4
WINDOW = 128
C_Q_RANK = 256
C_KV_RANK = 128
C_NOPE = 64
C_ROPE = 32
C_QK = C_NOPE + C_ROPE
ROPE_THETA = 10000.0
NEG_INF = float("-inf")
VMEM_LIMIT = 56 * 1024 * 1024


def _nt_dot(a, b):
    return lax.dot_general(a, b, (((1,), (1,)), ((), ())), preferred_element_type=F32)


def _dot(a, b):
    return jnp.dot(a, b, preferred_element_type=F32)


def _rms(x, g):
    ms = jnp.mean(x * x, axis=-1, keepdims=True)
    return x * lax.rsqrt(ms + RMS_EPS) * g


def _lane_iota(shape):
    return lax.broadcasted_iota(jnp.int32, shape, len(shape) - 1)


def _pair_rms(y, gain):
    left = _lane_iota(y.shape) < HEAD_DIM
    y2 = y * y
    s_left = jnp.sum(jnp.where(left, y2, 0.0), axis=-1, keepdims=True)
    s_right = jnp.sum(y2, axis=-1, keepdims=True) - s_left
    r = jnp.where(left, lax.rsqrt(s_left / HEAD_DIM + RMS_EPS), lax.rsqrt(s_right / HEAD_DIM + RMS_EPS))
    return y * r * gain


def _split_kv(t, parity, ones_lane=None):
    lane = _lane_iota(t.shape)
    keep = (lane < HEAD_DIM) if parity == 0 else (lane >= HEAD_DIM)
    out = jnp.where(keep, t, jnp.zeros_like(t))
    if ones_lane is not None:
        out = jnp.where(lane == ones_lane, jnp.ones_like(t), out)
    return out


def _proj_a_kernel(x_ref, g_ref, w_ref, gq_ref, gk_ref, q_out, k_out, v_out, qi_out, ki_out, wi_out):
    xn = _rms(x_ref[...], g_ref[...]).astype(MXU_DTYPE)
    for p in range(PAIRS):
        q = _dot(xn, w_ref[:, p * LANES:(p + 1) * LANES])
        q_out[:, p * LANES:(p + 1) * LANES] = _pair_rms(q, gq_ref[...]).astype(q_out.dtype)
    o = HEADS * HEAD_DIM
    k_out[...] = _pair_rms(_dot(xn, w_ref[:, o:o + LANES]), gk_ref[...]).astype(k_out.dtype)
    o += LANES
    v_out[...] = _dot(xn, w_ref[:, o:o + LANES]).astype(v_out.dtype)
    o += LANES
    n_qi = IDX_HEADS * IDX_DIM
    qi_out[...] = (_dot(xn, w_ref[:, o:o + n_qi]) * IDX_DIM ** -0.5).astype(qi_out.dtype)
    o += n_qi
    ki_out[...] = _dot(xn, w_ref[:, o:o + LANES]).astype(ki_out.dtype)
    o += LANES
    wi_out[...] = _dot(xn, w_ref[:, o:o + LANES]) * IDX_HEADS ** -0.5


def _proj_b_kernel(x_ref, g_ref, w_ref, gq_ref, gk_ref, q_out, k_out, v_out):
    xn = _rms(x_ref[...], g_ref[...]).astype(MXU_DTYPE)
    for p in range(PAIRS):
        q = _dot(xn, w_ref[:, p * LANES:(p + 1) * LANES])
        q_out[:, p * LANES:(p + 1) * LANES] = _pair_rms(q, gq_ref[...]).astype(q_out.dtype)
    o = HEADS * HEAD_DIM
    for c in range(B_KV_HEADS):
        k = _dot(xn, w_ref[:, o + c * LANES:o + (c + 1) * LANES])
        k_out[:, c * LANES:(c + 1) * LANES] = _pair_rms(k, gk_ref[...]).astype(k_out.dtype)
    o += B_KV_HEADS * LANES
    v_out[...] = _dot(xn, w_ref[:, o:o + B_KV_HEADS * LANES]).astype(v_out.dtype)


def _rope(y, cos_t, sin_t):
    lane = _lane_iota(y.shape)
    first_half = (lane >= C_NOPE) & (lane < C_NOPE + C_ROPE // 2)
    partner = jnp.where(first_half, pltpu.roll(y, LANES - C_ROPE // 2, 1), pltpu.roll(y, C_ROPE // 2, 1))
    return y * cos_t + partner * sin_t


def _proj_c_kernel(x_ref, g_ref, w_in_ref, gqa_ref, gkva_ref, wq_ref, wk_ref, wv_ref, gq_ref, gk_ref,
                   cos_ref, sin_ref, q_out, k_out, v_out):
    xn = _rms(x_ref[...], g_ref[...]).astype(MXU_DTYPE)
    lat = _dot(xn, w_in_ref[...])
    qa = _rms(lat[:, :C_Q_RANK], gqa_ref[...]).astype(MXU_DTYPE)
    kva = _rms(lat[:, C_Q_RANK:C_Q_RANK + C_KV_RANK], gkva_ref[...]).astype(MXU_DTYPE)
    k_rope = lat[:, C_Q_RANK + C_KV_RANK:]
    cos_t, sin_t = cos_ref[...], sin_ref[...]
    v_out[...] = _dot(kva, wv_ref[...]).astype(v_out.dtype)
    for h in range(HEADS):
        sl = slice(h * LANES, (h + 1) * LANES)
        q = _dot(qa, wq_ref[:, sl])
        q = q * lax.rsqrt(jnp.sum(q * q, axis=-1, keepdims=True) / C_QK + RMS_EPS) * gq_ref[...]
        q_out[:, sl] = _rope(q, cos_t, sin_t).astype(q_out.dtype)
        k = _dot(kva, wk_ref[:, sl]) + k_rope
        k = k * lax.rsqrt(jnp.sum(k * k, axis=-1, keepdims=True) / C_QK + RMS_EPS) * gk_ref[...]
        k_out[:, sl] = _rope(k, cos_t, sin_t).astype(k_out.dtype)


def _row_spec(tm, n):
    return pl.BlockSpec((tm, n), lambda i: (i, 0))


def _const_spec(shape):
    return pl.BlockSpec(shape, lambda i: (0,) * len(shape))


def _proj_call(kernel, x, consts, row_inputs, out_widths, out_dtypes, tm=512):
    t = x.shape[0]
    in_specs = [_row_spec(tm, x.shape[1])] + [_const_spec(c.shape) for c in consts]
    in_specs += [_row_spec(tm, r.shape[1]) for r in row_inputs]
    return pl.pallas_call(
        kernel,
        name=kernel.__name__.strip("_"),
        out_shape=[jax.ShapeDtypeStruct((t, n), d) for n, d in zip(out_widths, out_dtypes)],
        grid=(t // tm,),
        in_specs=in_specs,
        out_specs=[_row_spec(tm, n) for n in out_widths],
        compiler_params=pltpu.CompilerParams(dimension_semantics=("parallel",), vmem_limit_bytes=VMEM_LIMIT),
    )(x, *consts, *row_inputs)


def _out_mlp_kernel(x_ref, a_ref, wo_ref, g_ref, wu_ref, wd_ref, o_ref, *, f_chunk):
    x1 = x_ref[...] + _dot(a_ref[...], wo_ref[...])
    h = _rms(x1, g_ref[...]).astype(MXU_DTYPE)
    acc = x1
    for c in range(D_FF // f_chunk):
        u = jnp.maximum(_dot(h, wu_ref[:, c * f_chunk:(c + 1) * f_chunk]), 0.0)
        acc = acc + _dot((u * u).astype(MXU_DTYPE), wd_ref[c * f_chunk:(c + 1) * f_chunk, :])
    o_ref[...] = acc


def _out_mlp_call(x, attn, w_out, g, w_up, w_down, tm=512, f_chunk=1024):
    t = x.shape[0]
    single = pl.Buffered(1)
    return pl.pallas_call(
        functools.partial(_out_mlp_kernel, f_chunk=f_chunk),
        name="out_mlp",
        out_shape=jax.ShapeDtypeStruct((t, D_MODEL), F32),
        grid=(t // tm,),
        in_specs=[
            _row_spec(tm, D_MODEL),
            _row_spec(tm, attn.shape[1]),
            pl.BlockSpec(w_out.shape, lambda i: (0, 0), pipeline_mode=single),
            _const_spec(g.shape),
            pl.BlockSpec(w_up.shape, lambda i: (0, 0), pipeline_mode=single),
            pl.BlockSpec(w_down.shape, lambda i: (0, 0), pipeline_mode=single),
        ],
        out_specs=_row_spec(tm, D_MODEL),
        compiler_params=pltpu.CompilerParams(dimension_semantics=("parallel",), vmem_limit_bytes=VMEM_LIMIT),
    )(x, attn, w_out, g, w_up, w_down)


def _swa_kernel(q_ref, kp_ref, kc_ref, vp_ref, vc_ref, bias_ref, sink_ref, o_ref):
    n = pl.program_id(1)
    row = lax.broadcasted_iota(jnp.int32, (BLK, 2 * BLK), 0)
    col = lax.broadcasted_iota(jnp.int32, (BLK, 2 * BLK), 1)
    rel = row + WINDOW - col
    ok = (rel >= 0) & (rel < WINDOW) & ((n > 0) | (col >= WINDOW))
    for p in range(PAIRS):
        c = p // (PAIRS // B_KV_HEADS)
        sl = slice(c * LANES, (c + 1) * LANES)
        k2 = jnp.concatenate([kp_ref[0, :, sl], kc_ref[0, :, sl]], axis=0)
        v2 = jnp.concatenate([vp_ref[0, :, sl], vc_ref[0, :, sl]], axis=0)
        q = q_ref[0, :, p * LANES:(p + 1) * LANES]
        out = jnp.zeros((BLK, LANES), F32)
        for e in range(2):
            h = 2 * p + e
            s = _nt_dot(q, _split_kv(k2, e)) + bias_ref[h]
            s = jnp.where(ok, s, NEG_INF)
            sink = sink_ref[h]
            m = jnp.maximum(jnp.max(s, axis=-1, keepdims=True), sink)
            ex = jnp.exp(s - m)
            denom = jnp.sum(ex, axis=-1, keepdims=True) + jnp.exp(sink - m)
            pr = (ex / denom).astype(MXU_DTYPE)
            out = out + _dot(pr, _split_kv(v2, e))
        o_ref[0, :, p * LANES:(p + 1) * LANES] = out.astype(o_ref.dtype)


def _swa_call(q, k2, v2, bias, sinks):
    b, l, _ = q.shape
    nb = l // BLK
    kv_w = k2.shape[-1]
    prev = lambda bi, n: (bi, jnp.maximum(n - 1, 0), 0)
    cur = lambda bi, n: (bi, n, 0)
    return pl.pallas_call(
        _swa_kernel,
        name="swa_attn",
        out_shape=jax.ShapeDtypeStruct(q.shape, MXU_DTYPE),
        grid=(b, nb),
        in_specs=[
            pl.BlockSpec((1, BLK, q.shape[-1]), cur),
            pl.BlockSpec((1, BLK, kv_w), prev),
            pl.BlockSpec((1, BLK, kv_w), cur),
            pl.BlockSpec((1, BLK, kv_w), prev),
            pl.BlockSpec((1, BLK, kv_w), cur),
            pl.BlockSpec(bias.shape, lambda bi, n: (0, 0, 0)),
            pl.BlockSpec(memory_space=pltpu.SMEM),
        ],
        out_specs=pl.BlockSpec((1, BLK, q.shape[-1]), cur),
        compiler_params=pltpu.CompilerParams(dimension_semantics=("parallel", "parallel"),
                                             vmem_limit_bytes=VMEM_LIMIT),
    )(q, k2, k2, v2, v2, bias, sinks)


def _mla_kernel(q_ref, k_ref, v_ref, o_ref, m_scr, acc_scr, *, tq):
    seq = q_ref.shape[1]
    row = lax.broadcasted_iota(jnp.int32, (tq, tq), 0)
    col = lax.broadcasted_iota(jnp.int32, (tq, tq), 1)
    causal = col <= row

    def q_tile(qt, _):
        q0 = pl.multiple_of(qt * tq, tq)
        halves = []
        for e in range(2):
            q = q_ref[0, pl.ds(q0, tq), e * LANES:(e + 1) * LANES]
            sum_lane = HEAD_DIM if e == 0 else 0
            m_scr[...] = jnp.full(m_scr.shape, NEG_INF, F32)
            acc_scr[...] = jnp.zeros(acc_scr.shape, F32)

            def kv_step(kt, masked):
                k0 = pl.multiple_of(kt * tq, tq)
                s = _nt_dot(q, k_ref[0, pl.ds(k0, tq), e * LANES:(e + 1) * LANES])
                if masked:
                    s = jnp.where(causal, s, NEG_INF)
                m_old = m_scr[...]
                m_new = jnp.maximum(m_old, jnp.max(s, axis=-1, keepdims=True))
                pr = jnp.exp(s - m_new).astype(MXU_DTYPE)
                v = _split_kv(v_ref[0, pl.ds(k0, tq), :], e, ones_lane=sum_lane)
                acc_scr[...] = acc_scr[...] * jnp.exp(m_old - m_new) + _dot(pr, v)
                m_scr[...] = m_new

            def full_step(kt, carry):
                kv_step(kt, False)
                return carry

            lax.fori_loop(0, qt, full_step, 0)
            kv_step(qt, True)
            acc = acc_scr[...]
            halves.append(acc / acc[:, sum_lane:sum_lane + 1])
        out = jnp.where(_lane_iota((tq, LANES)) < HEAD_DIM, halves[0], halves[1])
        o_ref[0, pl.ds(q0, tq), :] = out.astype(o_ref.dtype)
        return 0

    lax.fori_loop(0, seq // tq, q_tile, 0)


def _mla_call(q, k, v, tq=256):
    b, l, _ = q.shape
    return pl.pallas_call(
        functools.partial(_mla_kernel, tq=tq),
        name="mla_attn",
        out_shape=jax.ShapeDtypeStruct(v.shape, MXU_DTYPE),
        grid=(b, PAIRS),
        in_specs=[
            pl.BlockSpec((1, l, 2 * LANES), lambda bi, p: (bi, 0, p)),
            pl.BlockSpec((1, l, 2 * LANES), lambda bi, p: (bi, 0, p)),
            pl.BlockSpec((1, l, LANES), lambda bi, p: (bi, 0, p)),
        ],
        out_specs=pl.BlockSpec((1, l, LANES), lambda bi, p: (bi, 0, p)),
        scratch_shapes=[pltpu.VMEM((tq, 1), F32), pltpu.VMEM((tq, LANES), F32)],
        compiler_params=pltpu.CompilerParams(dimension_semantics=("parallel", "parallel"),
                                             vmem_limit_bytes=VMEM_LIMIT),
    )(q, k, v)


def _dsa_kernel(q_ref, qi_ref, wi_ref, k_ref, v_ref, ki_ref, bias_ref, tri_ref, o_ref,
                s_scr, m_scr, acc_scr, *, k_sel):
    i = pl.program_id(1)
    n_tiles = i + 1
    row = lax.broadcasted_iota(jnp.int32, (BLK, BLK), 0)
    col = lax.broadcasted_iota(jnp.int32, (BLK, BLK), 1)
    int_min = jnp.int32(-2 ** 31)

    wi = wi_ref[0]

    def score_tile(j, carry):
        k0 = pl.multiple_of(j * BLK, BLK)
        ki = ki_ref[0, pl.ds(k0, BLK), :]
        sc = jnp.zeros((BLK, BLK), F32)
        for hp in range(IDX_HEADS // 2):
            qi = qi_ref[0, :, hp * LANES:(hp + 1) * LANES]
            for e in range(2):
                h = 2 * hp + e
                sc = sc + wi[:, h:h + 1] * jnp.maximum(_nt_dot(qi, _split_kv(ki, e)), 0.0)
        s_scr[j] = jnp.where((j < i) | (col <= row), sc, NEG_INF)
        return carry

    lax.fori_loop(0, n_tiles, score_tile, 0)

    def count(pred):
        def body(j, c):
            return c + jnp.where(pred(s_scr[j]), 1.0, 0.0)
        c = lax.fori_loop(0, n_tiles, body, jnp.zeros((BLK, BLK), F32))
        return jnp.sum(c, axis=-1, keepdims=True)

    def key_to_float(key):
        bits = jnp.where(key >= 0, key, key ^ jnp.int32(0x7FFFFFFF))
        return lax.bitcast_convert_type(bits, F32)

    def search():
        def bit_step(it, u):
            cand_u = u | lax.shift_left(jnp.int32(1), 31 - it)
            cand = key_to_float(cand_u ^ int_min)
            take = count(lambda s: s >= cand) >= k_sel
            return jnp.where(take, cand_u, u)
        u = lax.fori_loop(0, 32, bit_step, jnp.zeros((BLK, 1), jnp.int32))
        t = key_to_float(u ^ int_min)
        return jnp.where(t != t, NEG_INF, t)

    thr = lax.cond(n_tiles * BLK > k_sel, search, lambda: jnp.full((BLK, 1), NEG_INF, F32))

    n_gt = count(lambda s: s > thr)
    n_ge = count(lambda s: s >= thr)
    tied = (n_ge > k_sel) & (thr > NEG_INF)

    @pl.when(jnp.max(jnp.where(tied, 1.0, 0.0)) > 0.0)
    def _():
        need = k_sel - n_gt

        def tie_tile(j, seen):
            s = s_scr[j]
            eq = s == thr
            eq_f = jnp.where(eq, 1.0, 0.0).astype(MXU_DTYPE)
            rank = seen + _dot(eq_f, tri_ref[...])
            s_scr[j] = jnp.where(eq & tied & (rank > need), NEG_INF, s)
            return seen + jnp.sum(eq_f.astype(F32), axis=-1, keepdims=True)

        lax.fori_loop(0, n_tiles, tie_tile, jnp.zeros((BLK, 1), F32))

    m_scr[...] = jnp.full(m_scr.shape, NEG_INF, F32)
    acc_scr[...] = jnp.zeros(acc_scr.shape, F32)

    def attn_tile(j, carry):
        k0 = pl.multiple_of(j * BLK, BLK)
        sel = (s_scr[j] >= thr) & ((j < i) | (col <= row))
        mask_bias = jnp.where(sel, 0.0, NEG_INF)
        k2 = k_ref[0, pl.ds(k0, BLK), :]
        v = _split_kv(v_ref[0, pl.ds(k0, BLK), :], 0, ones_lane=HEAD_DIM)
        near =jnp.where(j == i, 1, jnp.where(j == i - 1, 0, 2))
        for p in range(PAIRS):
            q = q_ref[0, :, p * LANES:(p + 1) * LANES]
            for e in range(2):
                h = 2 * p + e
                s = _nt_dot(q, _split_kv(k2, e)) + bias_ref[h, near] + mask_bias
                m_old = m_scr[h]
                m_new = jnp.maximum(m_old, jnp.max(s, axis=-1, keepdims=True))
                m_safe = jnp.where(m_new == NEG_INF, 0.0, m_new)
                pr = jnp.exp(s - m_safe).astype(MXU_DTYPE)
                acc_scr[h] = acc_scr[h] * jnp.exp(m_old - m_safe) + _dot(pr, v)
                m_scr[h] = m_new
        return carry

    lax.fori_loop(0, n_tiles, attn_tile, 0)

    for p in range(PAIRS):
        a0, a1 = acc_scr[2 * p], acc_scr[2 * p + 1]
        o0 = a0[:, :HEAD_DIM] / a0[:, HEAD_DIM:HEAD_DIM + 1]
        o1 = a1[:, :HEAD_DIM] / a1[:, HEAD_DIM:HEAD_DIM + 1]
        o_ref[0, :, p * LANES:(p + 1) * LANES] = jnp.concatenate([o0, o1], axis=-1).astype(o_ref.dtype)


def _dsa_call(q, qi, wi, k2, v2, ki2, bias3, tri):
    b, l, _ = q.shape
    nb = l // BLK
    k_sel = min(TOPK_MAX, l // 4)
    blk = lambda bi, n: (bi, n, 0)
    seq = lambda bi, n: (bi, 0, 0)
    return pl.pallas_call(
        functools.partial(_dsa_kernel, k_sel=k_sel),
        name="dsa_attn",
        out_shape=jax.ShapeDtypeStruct(q.shape, MXU_DTYPE),
        grid=(b, nb),
        in_specs=[
            pl.BlockSpec((1, BLK, q.shape[-1]), blk),
            pl.BlockSpec((1, BLK, qi.shape[-1]), blk),
            pl.BlockSpec((1, BLK, LANES), blk),
            pl.BlockSpec((1, l, LANES), seq),
            pl.BlockSpec((1, l, LANES), seq),
            pl.BlockSpec((1, l, LANES), seq),
            pl.BlockSpec(bias3.shape, lambda bi, n: (0, 0, 0, 0)),
            pl.BlockSpec(tri.shape, lambda bi, n: (0, 0)),
        ],
        out_specs=pl.BlockSpec((1, BLK, q.shape[-1]), blk),
        scratch_shapes=[
            pltpu.VMEM((nb, BLK, BLK), F32),
            pltpu.VMEM((HEADS, BLK, 1), F32),
            pltpu.VMEM((HEADS, BLK, LANES), F32),
        ],
        compiler_params=pltpu.CompilerParams(dimension_semantics=("parallel", "arbitrary"),
                                             vmem_limit_bytes=VMEM_LIMIT),
    )(q, qi, wi, k2, v2, ki2, bias3, tri)


def _t5_bucket(rel):
    n = jnp.maximum(rel, 0)
    max_exact = N_BUCKETS // 2
    nf = jnp.maximum(n, 1).astype(F32)
    large = max_exact + (jnp.log(nf / max_exact) / math.log(MAX_DISTANCE / max_exact)
                         * (N_BUCKETS - max_exact)).astype(jnp.int32)
    large = jnp.minimum(large, N_BUCKETS - 1)
    return jnp.where(n < max_exact, n, large)


def _band_bias(rel_bias):
    qi = jnp.arange(BLK, dtype=jnp.int32)[:, None] + BLK
    kj = jnp.arange(2 * BLK, dtype=jnp.int32)[None, :]
    return jnp.transpose(rel_bias[_t5_bucket(qi - kj)], (2, 0, 1))


def _pair_gain(g, scale=1.0):
    return (jnp.concatenate([g, g]) * scale).reshape(1, LANES).astype(F32)


def _dup(w, n_heads):
    d = w.shape[0]
    w = w.reshape(d, n_heads, 1, HEAD_DIM)
    return jnp.broadcast_to(w, (d, n_heads, 2, HEAD_DIM)).reshape(d, n_heads * LANES)


def _layer_a(x, g_mix, w_in, q_gain, k_gain):
    nq = HEADS * HEAD_DIM
    sizes = [nq, HEAD_DIM, HEAD_DIM, IDX_HEADS * IDX_DIM, IDX_DIM, IDX_HEADS]
    wq, wk, wv, wqi, wki, wwi = jnp.split(w_in, np.cumsum(sizes)[:-1].tolist(), axis=1)
    w = jnp.concatenate([wq, _dup(wk, 1), _dup(wv, 1), wqi, _dup(wki, 1),
                         jnp.pad(wwi, ((0, 0), (0, LANES - IDX_HEADS)))], axis=1).astype(MXU_DTYPE)
    consts = [g_mix.reshape(1, -1), w, _pair_gain(q_gain, HEAD_DIM ** -0.5), _pair_gain(k_gain)]
    return _proj_call(_proj_a_kernel, x, consts, [],
                      [nq, LANES, LANES, IDX_HEADS * IDX_DIM, LANES, LANES],
                      [MXU_DTYPE] * 5 + [F32])


def _layer_b(x, g_mix, w_in, q_gain, k_gain):
    nq = HEADS * HEAD_DIM
    nkv = B_KV_HEADS * HEAD_DIM
    wq, wk, wv = jnp.split(w_in, [nq, nq + nkv], axis=1)
    w = jnp.concatenate([wq, _dup(wk, B_KV_HEADS), _dup(wv, B_KV_HEADS)], axis=1).astype(MXU_DTYPE)
    consts = [g_mix.reshape(1, -1), w, _pair_gain(q_gain, HEAD_DIM ** -0.5), _pair_gain(k_gain)]
    return _proj_call(_proj_b_kernel, x, consts, [],
                      [nq, B_KV_HEADS * LANES, B_KV_HEADS * LANES], [MXU_DTYPE] * 3)


def _pad_heads(w, width):
    d = w.shape[0]
    w = w.reshape(d, HEADS, width)
    return jnp.pad(w, ((0, 0), (0, 0), (0, LANES - width))).reshape(d, HEADS * LANES)


def _layer_c(x, g_mix, w_in, q_a_gain, w_q_b, kv_a_gain, w_kv_b, q_gain, k_gain, cos_t, sin_t):
    d = w_in.shape[0]
    w_rope = jnp.zeros((d, LANES), w_in.dtype).at[:, C_NOPE:C_QK].set(w_in[:, C_Q_RANK + C_KV_RANK:])
    w_in_p = jnp.concatenate([w_in[:, :C_Q_RANK + C_KV_RANK], w_rope], axis=1).astype(MXU_DTYPE)
    wq = _pad_heads(w_q_b, C_QK).astype(MXU_DTYPE)
    w_kv = w_kv_b.reshape(C_KV_RANK, HEADS, C_NOPE + HEAD_DIM)
    wk = _pad_heads(w_kv[:, :, :C_NOPE].reshape(C_KV_RANK, -1), C_NOPE).astype(MXU_DTYPE)
    wv = w_kv[:, :, C_NOPE:].reshape(C_KV_RANK, -1).astype(MXU_DTYPE)
    pad_gain = lambda g, s: (jnp.pad(g, (0, LANES - C_QK)) * s).reshape(1, LANES).astype(F32)
    consts = [g_mix.reshape(1, -1), w_in_p, q_a_gain.reshape(1, -1), kv_a_gain.reshape(1, -1), wq, wk, wv,
              pad_gain(q_gain, C_QK ** -0.5), pad_gain(k_gain, 1.0)]
    return _proj_call(_proj_c_kernel, x, consts, [cos_t, sin_t],
                      [HEADS * LANES, HEADS * LANES, HEADS * HEAD_DIM], [MXU_DTYPE] * 3, tm=256)


def kernel(x, positions, rel_bias, norm_mix, norm_mlp, w_up, w_down, a_w_in, a_q_gain, a_k_gain, a_w_out,
           b_w_in, b_q_gain, b_k_gain, b_sinks, b_w_out, c_w_in, c_q_a_gain, c_w_q_b, c_kv_a_gain, c_w_kv_b,
           c_q_gain, c_k_gain, c_w_out):
    b, l, d = x.shape
    t = b * l
    depth = norm_mix.shape[0]

    band = _band_bias(rel_bias.astype(F32))
    far = rel_bias[N_BUCKETS - 1].astype(F32)[:, None, None]
    bias3 = jnp.stack([band[:, :, :BLK] - far, band[:, :, BLK:] - far, jnp.zeros((HEADS, BLK, BLK), F32)], axis=1)
    tri = (np.arange(BLK)[:, None] <= np.arange(BLK)[None, :]).astype(np.float32)
    tri = jnp.asarray(tri, MXU_DTYPE)

    inv_freq = ROPE_THETA ** (-jnp.arange(0, C_ROPE, 2, dtype=F32) / C_ROPE)
    ang = positions.astype(F32).reshape(t, 1) * inv_freq
    cos, sin = jnp.cos(ang), jnp.sin(ang)
    cos_t = jnp.concatenate([jnp.ones((t, C_NOPE), F32), cos, cos, jnp.zeros((t, LANES - C_QK), F32)], axis=1)
    sin_t = jnp.concatenate([jnp.zeros((t, C_NOPE), F32), -sin, sin, jnp.zeros((t, LANES - C_QK), F32)], axis=1)

    xf = x.reshape(t, d).astype(F32)
    ia = ib = ic = 0
    for i in range(depth):
        kind = i % 3
        g_mix = norm_mix[i]
        if kind == 0:
            q, k2, v2, qi, ki2, wi = _layer_a(xf, g_mix, a_w_in[ia], a_q_gain[ia], a_k_gain[ia])
            r3 = lambda a: a.reshape(b, l, a.shape[-1])
            attn = _dsa_call(r3(q), r3(qi), r3(wi), r3(k2), r3(v2), r3(ki2), bias3, tri)
            w_out = a_w_out[ia]
            ia += 1
        elif kind == 1:
            q, k2, v2 = _layer_b(xf, g_mix, b_w_in[ib], b_q_gain[ib], b_k_gain[ib])
            r3 = lambda a: a.reshape(b, l, a.shape[-1])
            attn = _swa_call(r3(q), r3(k2), r3(v2), band, b_sinks[ib].astype(F32))
            w_out = b_w_out[ib]
            ib += 1
        else:
            q, k, v = _layer_c(xf, g_mix, c_w_in[ic], c_q_a_gain[ic], c_w_q_b[ic], c_kv_a_gain[ic],
                               c_w_kv_b[ic], c_q_gain[ic], c_k_gain[ic], cos_t, sin_t)
            r3 = lambda a: a.reshape(b, l, a.shape[-1])
            attn = _mla_call(r3(q), r3(k), r3(v))
            w_out = c_w_out[ic]
            ic += 1
        xf = _out_mlp_call(xf, attn.reshape(t, -1), w_out.astype(MXU_DTYPE), norm_mlp[i].reshape(1, -1),
                           w_up[i].astype(MXU_DTYPE), w_down[i].astype(MXU_DTYPE))
    return xf.reshape(b, l, d).astype(x.dtype)
```

```python
import functools
import math

import numpy as np
import jax
import jax.numpy as jnp
from jax import lax
from jax.experimental import pallas as pl
from jax.experimental.pallas import tpu as pltpu

F32 = jnp.float32
MXU_DTYPE = jnp.bfloat16

D_MODEL = 1024
D_FF = 4 * D_MODEL
RMS_EPS = 1e-6
N_BUCKETS = 32
MAX_DISTANCE = 128
HEADS = 16
HEAD_DIM = 64
PAIRS = HEADS // 2
LANES = 128
BLK = 128
IDX_HEADS = 8
IDX_DIM = 64
TOPK_MAX = 256
B_KV_HEADS = 4
WINDOW = 128
C_Q_RANK = 256
C_KV_RANK = 128
C_NOPE = 64
C_ROPE = 32
C_QK = C_NOPE + C_ROPE
ROPE_THETA = 10000.0
NEG_INF = float("-inf")
VT_ROWS = HEAD_DIM + 16
VMEM_LIMIT = 56 * 1024 * 1024


def _nt_dot(a, b):
    return lax.dot_general(a, b, (((1,), (1,)), ((), ())), preferred_element_type=F32)


def _dot(a, b):
    return jnp.dot(a, b, preferred_element_type=F32)


def _rms(x, g):
    ms = jnp.mean(x * x, axis=-1, keepdims=True)
    return x * lax.rsqrt(ms + RMS_EPS) * g


def _lane_iota(shape):
    return lax.broadcasted_iota(jnp.int32, shape, len(shape) - 1)


def _pair_rms(y, gain):
    left = _lane_iota(y.shape) < HEAD_DIM
    y2 = y * y
    s_left = jnp.sum(jnp.where(left, y2, 0.0), axis=-1, keepdims=True)
    s_right = jnp.sum(y2, axis=-1, keepdims=True) - s_left
    r = jnp.where(left, lax.rsqrt(s_left / HEAD_DIM + RMS_EPS), lax.rsqrt(s_right / HEAD_DIM + RMS_EPS))
    return y * r * gain


def _split_kv(t, parity, ones_lane=None):
    lane = _lane_iota(t.shape)
    keep = (lane < HEAD_DIM) if parity == 0 else (lane >= HEAD_DIM)
    out = jnp.where(keep, t, jnp.zeros_like(t))
    if ones_lane is not None:
        out = jnp.where(lane == ones_lane, jnp.ones_like(t), out)
    return out


def _put_blocks(out_ref, slot, tile):
    for bk in range(tile.shape[0] // BLK):
        out_ref[bk, slot * BLK:(slot + 1) * BLK, :] = tile[bk * BLK:(bk + 1) * BLK].astype(out_ref.dtype)


def _proj_a_kernel(x_ref, g_ref, w_ref, gq_ref, gk_ref, q_out, kc_out, vt_out, qi_out, kic_out, wi_out):
    xn = _rms(x_ref[...], g_ref[...]).astype(MXU_DTYPE)
    for p in range(PAIRS):
        q = _dot(xn, w_ref[:, p * LANES:(p + 1) * LANES])
        _put_blocks(q_out, p, _pair_rms(q, gq_ref[...]))
    o = HEADS * HEAD_DIM
    k = _pair_rms(_dot(xn, w_ref[:, o:o + LANES]), gk_ref[...])
    _put_blocks(kc_out, 0, _split_kv(k, 0))
    _put_blocks(kc_out, 1, _split_kv(k, 1))
    o += LANES
    vt = jnp.transpose(_dot(xn, w_ref[:, o:o + LANES]))
    row = lax.broadcasted_iota(jnp.int32, vt.shape, 0)
    vt = jnp.where(row == HEAD_DIM, 1.0, vt)
    for bk in range(vt.shape[1] // BLK):
        vt_out[bk] = vt[:VT_ROWS, bk * BLK:(bk + 1) * BLK].astype(vt_out.dtype)
    o += LANES
    for hp in range(IDX_HEADS // 2):
        _put_blocks(qi_out, hp, _dot(xn, w_ref[:, o:o + LANES]) * IDX_DIM ** -0.5)
        o += LANES
    ki = _dot(xn, w_ref[:, o:o + LANES])
    _put_blocks(kic_out, 0, _split_kv(ki, 0))
    _put_blocks(kic_out, 1, _split_kv(ki, 1))
    o += LANES
    wi_out[...] = _dot(xn, w_ref[:, o:o + LANES]) * IDX_HEADS ** -0.5


def _proj_b_kernel(x_ref, g_ref, w_ref, gq_ref, gk_ref, q_out, k_out, v_out):
    xn = _rms(x_ref[...], g_ref[...]).astype(MXU_DTYPE)
    for p in range(PAIRS):
        q = _dot(xn, w_ref[:, p * LANES:(p + 1) * LANES])
        q_out[:, p * LANES:(p + 1) * LANES] = _pair_rms(q, gq_ref[...]).astype(q_out.dtype)
    o = HEADS * HEAD_DIM
    for c in range(B_KV_HEADS):
        k = _dot(xn, w_ref[:, o + c * LANES:o + (c + 1) * LANES])
        k_out[:, c * LANES:(c + 1) * LANES] = _pair_rms(k, gk_ref[...]).astype(k_out.dtype)
    o += B_KV_HEADS * LANES
    v_out[...] = _dot(xn, w_ref[:, o:o + B_KV_HEADS * LANES]).astype(v_out.dtype)


def _rope(y, cos_t, sin_t):
    lane = _lane_iota(y.shape)
    first_half = (lane >= C_NOPE) & (lane < C_NOPE + C_ROPE // 2)
    partner = jnp.where(first_half, pltpu.roll(y, LANES - C_ROPE // 2, 1), pltpu.roll(y, C_ROPE // 2, 1))
    return y * cos_t + partner * sin_t


def _proj_c_kernel(x_ref, g_ref, w_in_ref, gqa_ref, gkva_ref, wq_ref, wk_ref, wv_ref, gq_ref, gk_ref,
                   cos_ref, sin_ref, q_out, k_out, vt_out):
    xn = _rms(x_ref[...], g_ref[...]).astype(MXU_DTYPE)
    lat = _dot(xn, w_in_ref[...])
    qa = _rms(lat[:, :C_Q_RANK], gqa_ref[...]).astype(MXU_DTYPE)
    kva = _rms(lat[:, C_Q_RANK:C_Q_RANK + C_KV_RANK], gkva_ref[...]).astype(MXU_DTYPE)
    k_rope = lat[:, C_Q_RANK + C_KV_RANK:]
    cos_t, sin_t = cos_ref[...], sin_ref[...]
    v = _dot(kva, wv_ref[...])
    tm = v.shape[0]
    ones_row = jnp.where(lax.broadcasted_iota(jnp.int32, (VT_ROWS - HEAD_DIM, tm), 0) == 0, 1.0, 0.0)
    for p in range(PAIRS):
        v_t = jnp.transpose(v[:, p * LANES:(p + 1) * LANES])
        for e in range(2):
            h = 2 * p + e
            head_t = jnp.concatenate([v_t[e * HEAD_DIM:(e + 1) * HEAD_DIM], ones_row], axis=0)
            for bk in range(tm // BLK):
                vt_out[bk, h * VT_ROWS:(h + 1) * VT_ROWS, :] = (
                    head_t[:, bk * BLK:(bk + 1) * BLK].astype(vt_out.dtype))
    for h in range(HEADS):
        sl = slice(h * LANES, (h + 1) * LANES)
        q = _dot(qa, wq_ref[:, sl])
        q = q * lax.rsqrt(jnp.sum(q * q, axis=-1, keepdims=True) / C_QK + RMS_EPS) * gq_ref[...]
        q_out[:, sl] = _rope(q, cos_t, sin_t).astype(q_out.dtype)
        k = _dot(kva, wk_ref[:, sl]) + k_rope
        k = k * lax.rsqrt(jnp.sum(k * k, axis=-1, keepdims=True) / C_QK + RMS_EPS) * gk_ref[...]
        k_out[:, sl] = _rope(k, cos_t, sin_t).astype(k_out.dtype)


def _row_spec(tm, n):
    return pl.BlockSpec((tm, n), lambda i: (i, 0))


def _const_spec(shape):
    return pl.BlockSpec(shape, lambda i: (0,) * len(shape))


def _proj_call(kernel, x, consts, row_inputs, outs, tm=512):
    t = x.shape[0]
    in_specs = [_row_spec(tm, x.shape[1])] + [_const_spec(c.shape) for c in consts]
    in_specs += [_row_spec(tm, r.shape[1]) for r in row_inputs]
    out_shape, out_specs = [], []
    for kind, n, dtype in outs:
        if kind == "rows":
            out_shape.append(jax.ShapeDtypeStruct((t, n), dtype))
            out_specs.append(_row_spec(tm, n))
        else:
            out_shape.append(jax.ShapeDtypeStruct((t // BLK, n, LANES), dtype))
            out_specs.append(pl.BlockSpec((tm // BLK, n, LANES), lambda i: (i, 0, 0)))
    return pl.pallas_call(
        kernel,
        name=kernel.__name__.strip("_"),
        out_shape=out_shape,
        grid=(t // tm,),
        in_specs=in_specs,
        out_specs=out_specs,
        compiler_params=pltpu.CompilerParams(dimension_semantics=("parallel",), vmem_limit_bytes=VMEM_LIMIT),
    )(x, *consts, *row_inputs)


def _out_mlp_kernel(x_ref, a_ref, wo_ref, g_ref, wu_ref, wd_ref, o_ref, *, f_chunk):
    x1 = x_ref[...] + _dot(a_ref[...], wo_ref[...])
    h = _rms(x1, g_ref[...]).astype(MXU_DTYPE)
    acc = x1
    for c in range(D_FF // f_chunk):
        u = jnp.maximum(_dot(h, wu_ref[:, c * f_chunk:(c + 1) * f_chunk]), 0.0)
        acc = acc + _dot((u * u).astype(MXU_DTYPE), wd_ref[c * f_chunk:(c + 1) * f_chunk, :])
    o_ref[...] = acc


def _out_mlp_call(x, attn, w_out, g, w_up, w_down, tm=512, f_chunk=1024):
    t = x.shape[0]
    single = pl.Buffered(1)
    return pl.pallas_call(
        functools.partial(_out_mlp_kernel, f_chunk=f_chunk),
        name="out_mlp",
        out_shape=jax.ShapeDtypeStruct((t, D_MODEL), F32),
        grid=(t // tm,),
        in_specs=[
            _row_spec(tm, D_MODEL),
            _row_spec(tm, attn.shape[1]),
            pl.BlockSpec(w_out.shape, lambda i: (0, 0), pipeline_mode=single),
            _const_spec(g.shape),
            pl.BlockSpec(w_up.shape, lambda i: (0, 0), pipeline_mode=single),
            pl.BlockSpec(w_down.shape, lambda i: (0, 0), pipeline_mode=single),
        ],
        out_specs=_row_spec(tm, D_MODEL),
        compiler_params=pltpu.CompilerParams(dimension_semantics=("parallel",), vmem_limit_bytes=VMEM_LIMIT),
    )(x, attn, w_out, g, w_up, w_down)


def _swa_kernel(q_ref, kp_ref, kc_ref, vp_ref, vc_ref, bias_ref, sink_ref, o_ref):
    n = pl.program_id(1)
    row = lax.broadcasted_iota(jnp.int32, (BLK, 2 * BLK), 0)
    col = lax.broadcasted_iota(jnp.int32, (BLK, 2 * BLK), 1)
    rel = row + WINDOW - col
    ok = (rel >= 0) & (rel < WINDOW) & ((n > 0) | (col >= WINDOW))
    for p in range(PAIRS):
        c = p // (PAIRS // B_KV_HEADS)
        sl = slice(c * LANES, (c + 1) * LANES)
        k2 = jnp.concatenate([kp_ref[0, :, sl], kc_ref[0, :, sl]], axis=0)
        v2 = jnp.concatenate([vp_ref[0, :, sl], vc_ref[0, :, sl]], axis=0)
        q = q_ref[0, :, p * LANES:(p + 1) * LANES]
        out = jnp.zeros((BLK, LANES), F32)
        for e in range(2):
            h = 2 * p + e
            s = _nt_dot(q, _split_kv(k2, e)) + bias_ref[h]
            s = jnp.where(ok, s, NEG_INF)
            sink = sink_ref[h]
            m = jnp.maximum(jnp.max(s, axis=-1, keepdims=True), sink)
            ex = jnp.exp(s - m)
            denom = jnp.sum(ex, axis=-1, keepdims=True) + jnp.exp(sink - m)
            pr = (ex / denom).astype(MXU_DTYPE)
            out = out + _dot(pr, _split_kv(v2, e))
        o_ref[0, :, p * LANES:(p + 1) * LANES] = out.astype(o_ref.dtype)


def _swa_call(q, k2, v2, bias, sinks):
    b, l, _ = q.shape
    nb = l // BLK
    kv_w = k2.shape[-1]
    prev = lambda bi, n: (bi, jnp.maximum(n - 1, 0), 0)
    cur = lambda bi, n: (bi, n, 0)
    return pl.pallas_call(
        _swa_kernel,
        name="swa_attn",
        out_shape=jax.ShapeDtypeStruct(q.shape, MXU_DTYPE),
        grid=(b, nb),
        in_specs=[
            pl.BlockSpec((1, BLK, q.shape[-1]), cur),
            pl.BlockSpec((1, BLK, kv_w), prev),
            pl.BlockSpec((1, BLK, kv_w), cur),
            pl.BlockSpec((1, BLK, kv_w), prev),
            pl.BlockSpec((1, BLK, kv_w), cur),
            pl.BlockSpec(bias.shape, lambda bi, n: (0, 0, 0)),
            pl.BlockSpec(memory_space=pltpu.SMEM),
        ],
        out_specs=pl.BlockSpec((1, BLK, q.shape[-1]), cur),
        compiler_params=pltpu.CompilerParams(dimension_semantics=("parallel", "parallel"),
                                             vmem_limit_bytes=VMEM_LIMIT),
    )(q, k2, k2, v2, v2, bias, sinks)


def _mla_kernel(q_ref, k_ref, vt_ref, o_ref, m_scr, acc_scr, *, tq, tk):
    seq = q_ref.shape[1]
    key_row = lax.broadcasted_iota(jnp.int32, (tk, tq), 0)
    query_col = lax.broadcasted_iota(jnp.int32, (tk, tq), 1)

    def q_tile(qt, _):
        q0 = pl.multiple_of(qt * tq, tq)
        qs = [q_ref[0, pl.ds(q0, tq), e * LANES:(e + 1) * LANES] for e in range(2)]
        m_scr[...] = jnp.full(m_scr.shape, NEG_INF, F32)
        acc_scr[...] = jnp.zeros(acc_scr.shape, F32)

        def kv_step(kt, masked):
            k0 = pl.multiple_of(kt * tk, tk)
            for e in range(2):
                s = _nt_dot(k_ref[0, pl.ds(k0, tk), e * LANES:(e + 1) * LANES], qs[e])
                if masked:
                    s = jnp.where(k0 + key_row <= q0 + query_col, s, NEG_INF)
                m_old = m_scr[e]
                m_new = jnp.maximum(m_old, jnp.max(s, axis=0, keepdims=True))
                pr = jnp.exp(s - m_new).astype(MXU_DTYPE)
                vt = jnp.concatenate([vt_ref[0, kt * (tk // BLK) + c, e * VT_ROWS:(e + 1) * VT_ROWS, :]
                                      for c in range(tk // BLK)], axis=1)
                acc_scr[e] = acc_scr[e] * jnp.exp(m_old - m_new) + _dot(vt, pr)
                m_scr[e] = m_new

        def full_step(kt, carry):
            kv_step(kt, False)
            return carry

        n_full = qt * (tq // tk)
        lax.fori_loop(0, n_full, full_step, 0)
        for c in range(tq // tk):
            kv_step(n_full + c, True)
        a0, a1 = acc_scr[0], acc_scr[1]
        pair = jnp.concatenate([a0[:HEAD_DIM] / a0[HEAD_DIM:HEAD_DIM + 1],
                                a1[:HEAD_DIM] / a1[HEAD_DIM:HEAD_DIM + 1]], axis=0)
        o_ref[0, pl.ds(q0, tq), :] = jnp.transpose(pair).astype(o_ref.dtype)
        return 0

    lax.fori_loop(0, seq // tq, q_tile, 0)


def _mla_call(q, k, vt, tq=512, tk=256):
    b, l, _ = q.shape
    tq, tk = min(tq, l), min(tk, l)
    nb = l // BLK
    vt = vt.reshape(b, nb, HEADS * VT_ROWS, LANES)
    return pl.pallas_call(
        functools.partial(_mla_kernel, tq=tq, tk=tk),
        name="mla_attn",
        out_shape=jax.ShapeDtypeStruct((b, l, HEADS * HEAD_DIM), MXU_DTYPE),
        grid=(b, PAIRS),
        in_specs=[
            pl.BlockSpec((1, l, 2 * LANES), lambda bi, p: (bi, 0, p)),
            pl.BlockSpec((1, l, 2 * LANES), lambda bi, p: (bi, 0, p)),
            pl.BlockSpec((1, nb, 2 * VT_ROWS, LANES), lambda bi, p: (bi, 0, p, 0)),
        ],
        out_specs=pl.BlockSpec((1, l, LANES), lambda bi, p: (bi, 0, p)),
        scratch_shapes=[pltpu.VMEM((2, 1, tq), F32), pltpu.VMEM((2, VT_ROWS, tq), F32)],
        compiler_params=pltpu.CompilerParams(dimension_semantics=("parallel", "parallel"),
                                             vmem_limit_bytes=VMEM_LIMIT),
    )(q, k, vt)


def _dsa_kernel(q_ref, qi_ref, wi_ref, kc_ref, vt_ref, kic_ref, bias_ref, tri_ref, o_ref,
                s_scr, m_scr, acc_scr, *, k_sel):
    i = pl.program_id(1)
    n_tiles = i + 1
    key_row = lax.broadcasted_iota(jnp.int32, (BLK, BLK), 0)
    query_col = lax.broadcasted_iota(jnp.int32, (BLK, BLK), 1)
    int_min = jnp.int32(-2 ** 31)

    def admissible(j):
        return (j < i) | (key_row <= query_col)

    wt = jnp.transpose(wi_ref[0])
    qi = qi_ref[0, 0]

    def score_tile(j, carry):
        d = jnp.maximum(_nt_dot(kic_ref[0, j], qi), 0.0)
        sc = jnp.zeros((BLK, BLK), F32)
        for hp in range(IDX_HEADS // 2):
            for e in range(2):
                h = 2 * hp + e
                sc = sc + wt[h:h + 1, :] * d[e * BLK:(e + 1) * BLK, hp * BLK:(hp + 1) * BLK]
        s_scr[j] = jnp.where(admissible(j), sc, NEG_INF)
        return carry

    lax.fori_loop(0, n_tiles, score_tile, 0)

    def count(pred):
        def body(j, c):
            return c + jnp.where(pred(s_scr[j]), 1.0, 0.0)
        c = lax.fori_loop(0, n_tiles, body, jnp.zeros((BLK, BLK), F32))
        return jnp.sum(c, axis=0, keepdims=True)

    def key_to_float(key):
        bits = jnp.where(key >= 0, key, key ^ jnp.int32(0x7FFFFFFF))
        return lax.bitcast_convert_type(bits, F32)

    def search():
        def bit_step(it, u):
            cand_u = u | lax.shift_left(jnp.int32(1), 31 - it)
            cand = key_to_float(cand_u ^ int_min)
            take = count(lambda s: s >= cand) >= k_sel
            return jnp.where(take, cand_u, u)
        u = lax.fori_loop(0, 32, bit_step, jnp.zeros((1, BLK), jnp.int32))
        t = key_to_float(u ^ int_min)
        return jnp.where(t != t, NEG_INF, t)

    thr = lax.cond(n_tiles * BLK > k_sel, search, lambda: jnp.full((1, BLK), NEG_INF, F32))

    n_gt = count(lambda s: s > thr)
    n_ge = count(lambda s: s >= thr)
    tied = (n_ge > k_sel) & (thr > NEG_INF)

    @pl.when(jnp.max(jnp.where(tied, 1.0, 0.0)) > 0.0)
    def _():
        need = k_sel - n_gt

        def tie_tile(j, seen):
            s = s_scr[j]
            eq = s == thr
            eq_f = jnp.where(eq, 1.0, 0.0).astype(MXU_DTYPE)
            rank = seen + _dot(tri_ref[...], eq_f)
            s_scr[j] = jnp.where(eq & tied & (rank > need), NEG_INF, s)
            return seen + jnp.sum(eq_f.astype(F32), axis=0, keepdims=True)

        lax.fori_loop(0, n_tiles, tie_tile, jnp.zeros((1, BLK), F32))

    m_scr[...] = jnp.full(m_scr.shape, NEG_INF, F32)
    acc_scr[...] = jnp.zeros(acc_scr.shape, F32)
    q = q_ref[0, 0]

    def attn_tile(j, near):
        sel = (s_scr[j] >= thr) & admissible(j)
        mask = pltpu.repeat(jnp.where(sel, 0.0, NEG_INF), PAIRS, axis=1)
        s = _nt_dot(kc_ref[0, j], q)
        if near is not None:
            s = s + bias_ref[near]
        vt = vt_ref[0, j]
        for e in range(2):
            se = s[e * BLK:(e + 1) * BLK] + mask
            m_old = m_scr[e]
            m_new = jnp.maximum(m_old, jnp.max(se, axis=0, keepdims=True))
            m_safe = jnp.where(m_new == NEG_INF, 0.0, m_new)
            pr = jnp.exp(se - m_safe).astype(MXU_DTYPE)
            acc_scr[e] = acc_scr[e] * jnp.exp(m_old - m_safe) + _dot(vt, pr)
            m_scr[e] = m_new

    def far_tile(j, carry):
        attn_tile(j, None)
        return carry

    lax.fori_loop(0, i - 1, far_tile, 0)

    @pl.when(i >= 1)
    def _():
        attn_tile(i - 1, 0)

    attn_tile(i, 1)

    for p in range(PAIRS):
        sl = slice(p * BLK, (p + 1) * BLK)
        a0, a1 = acc_scr[0, :, sl], acc_scr[1, :, sl]
        pair = jnp.concatenate([a0[:HEAD_DIM] / a0[HEAD_DIM:HEAD_DIM + 1],
                                a1[:HEAD_DIM] / a1[HEAD_DIM:HEAD_DIM + 1]], axis=0)
        o_ref[0, :, sl] = jnp.transpose(pair).astype(o_ref.dtype)


def _dsa_call(q, qi, wi, kc, vt, kic, bias_t, tri, batch):
    nb = q.shape[0] // batch
    l = nb * BLK
    k_sel = min(TOPK_MAX, l // 4)
    per_batch = lambda a: a.reshape(batch, nb, *a.shape[1:])
    q, qi, kc, vt, kic = map(per_batch, (q, qi, kc, vt, kic))
    blk = lambda a: pl.BlockSpec((1, 1) + a.shape[2:], lambda bi, n: (bi, n, 0, 0))
    seq = lambda a: pl.BlockSpec((1,) + a.shape[1:], lambda bi, n: (bi, 0, 0, 0))
    return pl.pallas_call(
        functools.partial(_dsa_kernel, k_sel=k_sel),
        name="dsa_attn",
        out_shape=jax.ShapeDtypeStruct((batch, l, HEADS * HEAD_DIM), MXU_DTYPE),
        grid=(batch, nb),
        in_specs=[
            blk(q),
            blk(qi),
            pl.BlockSpec((1, BLK, LANES), lambda bi, n: (bi, n, 0)),
            seq(kc),
            seq(vt),
            seq(kic),
            pl.BlockSpec(bias_t.shape, lambda bi, n: (0, 0, 0)),
            pl.BlockSpec(tri.shape, lambda bi, n: (0, 0)),
        ],
        out_specs=pl.BlockSpec((1, BLK, HEADS * HEAD_DIM), lambda bi, n: (bi, n, 0)),
        scratch_shapes=[
            pltpu.VMEM((nb, BLK, BLK), F32),
            pltpu.VMEM((2, 1, PAIRS * BLK), F32),
            pltpu.VMEM((2, VT_ROWS, PAIRS * BLK), F32),
        ],
        compiler_params=pltpu.CompilerParams(dimension_semantics=("parallel", "arbitrary"),
                                             vmem_limit_bytes=VMEM_LIMIT),
    )(q, qi, wi.reshape(batch, l, LANES), kc, vt, kic, bias_t, tri)


def _t5_bucket(rel):
    n = jnp.maximum(rel, 0)
    max_exact = N_BUCKETS // 2
    nf = jnp.maximum(n, 1).astype(F32)
    large = max_exact + (jnp.log(nf / max_exact) / math.log(MAX_DISTANCE / max_exact)
                         * (N_BUCKETS - max_exact)).astype(jnp.int32)
    large = jnp.minimum(large, N_BUCKETS - 1)
    return jnp.where(n < max_exact, n, large)


def _band_bias(rel_bias):
    qi = jnp.arange(BLK, dtype=jnp.int32)[:, None] + BLK
    kj = jnp.arange(2 * BLK, dtype=jnp.int32)[None, :]
    return jnp.transpose(rel_bias[_t5_bucket(qi - kj)], (2, 0, 1))


def _pair_gain(g, scale=1.0):
    return (jnp.concatenate([g, g]) * scale).reshape(1, LANES).astype(F32)


def _dup(w, n_heads):
    d = w.shape[0]
    w = w.reshape(d, n_heads, 1, HEAD_DIM)
    return jnp.broadcast_to(w, (d, n_heads, 2, HEAD_DIM)).reshape(d, n_heads * LANES)


def _layer_a(x, g_mix, w_in, q_gain, k_gain):
    nq = HEADS * HEAD_DIM
    sizes = [nq, HEAD_DIM, HEAD_DIM, IDX_HEADS * IDX_DIM, IDX_DIM, IDX_HEADS]
    wq, wk, wv, wqi, wki, wwi = jnp.split(w_in, np.cumsum(sizes)[:-1].tolist(), axis=1)
    pad_to_lanes = lambda a: jnp.pad(a, ((0, 0), (0, LANES - a.shape[1])))
    w = jnp.concatenate([wq, _dup(wk, 1), pad_to_lanes(wv), wqi, _dup(wki, 1), pad_to_lanes(wwi)],
                        axis=1).astype(MXU_DTYPE)
    consts = [g_mix.reshape(1, -1), w, _pair_gain(q_gain, HEAD_DIM ** -0.5), _pair_gain(k_gain)]
    outs = [("stacked", PAIRS * BLK, MXU_DTYPE), ("stacked", 2 * BLK, MXU_DTYPE), ("stacked", VT_ROWS, MXU_DTYPE),
            ("stacked", IDX_HEADS // 2 * BLK, MXU_DTYPE), ("stacked", 2 * BLK, MXU_DTYPE), ("rows", LANES, F32)]
    return _proj_call(_proj_a_kernel, x, consts, [], outs)


def _layer_b(x, g_mix, w_in, q_gain, k_gain):
    nq = HEADS * HEAD_DIM
    nkv = B_KV_HEADS * HEAD_DIM
    wq, wk, wv = jnp.split(w_in, [nq, nq + nkv], axis=1)
    w = jnp.concatenate([wq, _dup(wk, B_KV_HEADS), _dup(wv, B_KV_HEADS)], axis=1).astype(MXU_DTYPE)
    consts = [g_mix.reshape(1, -1), w, _pair_gain(q_gain, HEAD_DIM ** -0.5), _pair_gain(k_gain)]
    outs = [("rows", n, MXU_DTYPE) for n in (nq, B_KV_HEADS * LANES, B_KV_HEADS * LANES)]
    return _proj_call(_proj_b_kernel, x, consts, [], outs)


def _pad_heads(w, width):
    d = w.shape[0]
    w = w.reshape(d, HEADS, width)
    return jnp.pad(w, ((0, 0), (0, 0), (0, LANES - width))).reshape(d, HEADS * LANES)


def _layer_c(x, g_mix, w_in, q_a_gain, w_q_b, kv_a_gain, w_kv_b, q_gain, k_gain, cos_t, sin_t):
    d = w_in.shape[0]
    w_rope = jnp.zeros((d, LANES), w_in.dtype).at[:, C_NOPE:C_QK].set(w_in[:, C_Q_RANK + C_KV_RANK:])
    w_in_p = jnp.concatenate([w_in[:, :C_Q_RANK + C_KV_RANK], w_rope], axis=1).astype(MXU_DTYPE)
    wq = _pad_heads(w_q_b, C_QK).astype(MXU_DTYPE)
    w_kv = w_kv_b.reshape(C_KV_RANK, HEADS, C_NOPE + HEAD_DIM)
    wk = _pad_heads(w_kv[:, :, :C_NOPE].reshape(C_KV_RANK, -1), C_NOPE).astype(MXU_DTYPE)
    wv = w_kv[:, :, C_NOPE:].reshape(C_KV_RANK, -1).astype(MXU_DTYPE)
    pad_gain = lambda g, s: (jnp.pad(g, (0, LANES - C_QK)) * s).reshape(1, LANES).astype(F32)
    consts = [g_mix.reshape(1, -1), w_in_p, q_a_gain.reshape(1, -1), kv_a_gain.reshape(1, -1), wq, wk, wv,
              pad_gain(q_gain, C_QK ** -0.5), pad_gain(k_gain, 1.0)]
    outs = [("rows", HEADS * LANES, MXU_DTYPE), ("rows", HEADS * LANES, MXU_DTYPE),
            ("stacked", HEADS * VT_ROWS, MXU_DTYPE)]
    return _proj_call(_proj_c_kernel, x, consts, [cos_t, sin_t], outs, tm=256)


def kernel(x, positions, rel_bias, norm_mix, norm_mlp, w_up, w_down, a_w_in, a_q_gain, a_k_gain, a_w_out,
           b_w_in, b_q_gain, b_k_gain, b_sinks, b_w_out, c_w_in, c_q_a_gain, c_w_q_b, c_kv_a_gain, c_w_kv_b,
           c_q_gain, c_k_gain, c_w_out):
    b, l, d = x.shape
    t = b * l
    depth = norm_mix.shape[0]

    band = _band_bias(rel_bias.astype(F32))
    far = rel_bias[N_BUCKETS - 1].astype(F32)[:, None, None]
    bias_t = jnp.transpose((band - far).reshape(PAIRS, 2, BLK, 2, BLK), (3, 1, 4, 0, 2)).reshape(
        2, 2 * BLK, PAIRS * BLK)
    tri = jnp.asarray(np.arange(BLK)[:, None] >= np.arange(BLK)[None, :], MXU_DTYPE)

    inv_freq = ROPE_THETA ** (-jnp.arange(0, C_ROPE, 2, dtype=F32) / C_ROPE)
    ang = positions.astype(F32).reshape(t, 1) * inv_freq
    cos, sin = jnp.cos(ang), jnp.sin(ang)
    cos_t = jnp.concatenate([jnp.ones((t, C_NOPE), F32), cos, cos, jnp.zeros((t, LANES - C_QK), F32)], axis=1)
    sin_t = jnp.concatenate([jnp.zeros((t, C_NOPE), F32), -sin, sin, jnp.zeros((t, LANES - C_QK), F32)], axis=1)

    xf = x.reshape(t, d).astype(F32)
    ia = ib = ic = 0
    for i in range(depth):
        kind = i % 3
        g_mix = norm_mix[i]
        if kind == 0:
            q, kc, vt, qi, kic, wi = _layer_a(xf, g_mix, a_w_in[ia], a_q_gain[ia], a_k_gain[ia])
            attn = _dsa_call(q, qi, wi, kc, vt, kic, bias_t, tri, b)
            w_out = a_w_out[ia]
            ia += 1
        elif kind == 1:
            q, k2, v2 = _layer_b(xf, g_mix, b_w_in[ib], b_q_gain[ib], b_k_gain[ib])
            r3 = lambda a: a.reshape(b, l, a.shape[-1])
            attn = _swa_call(r3(q), r3(k2), r3(v2), band, b_sinks[ib].astype(F32))
            w_out = b_w_out[ib]
            ib += 1
        else:
            q, k, vt = _layer_c(xf, g_mix, c_w_in[ic], c_q_a_gain[ic], c_w_q_b[ic], c_kv_a_gain[ic],
                               c_w_kv_b[ic], c_q_gain[ic], c_k_gain[ic], cos_t, sin_t)
            r3 = lambda a: a.reshape(b, l, a.shape[-1])
            attn = _mla_call(r3(q), r3(k), vt)
            w_out = c_w_out[ic]
            ic += 1
        xf = _out_mlp_call(xf, attn.reshape(t, -1), w_out.astype(MXU_DTYPE), norm_mlp[i].reshape(1, -1),
                           w_up[i].astype(MXU_DTYPE), w_down[i].astype(MXU_DTYPE))
    return xf.reshape(b, l, d).astype(x.dtype)
```

```python
import functools
import math

import numpy as np
import jax
import jax.numpy as jnp
from jax import lax
from jax.experimental import pallas as pl
from jax.experimental.pallas import tpu as pltpu

F32 = jnp.float32
MXU_DTYPE = jnp.bfloat16

D_MODEL = 1024
D_FF = 4 * D_MODEL
RMS_EPS = 1e-6
N_BUCKETS = 32
MAX_DISTANCE = 128
HEADS = 16
HEAD_DIM = 64
PAIRS = HEADS // 2
LANES = 128
BLK = 128
IDX_HEADS = 8
IDX_DIM = 64
TOPK_MAX = 256
B_KV_HEADS = 4
WINDOW = 128
C_Q_RANK = 256
C_KV_RANK = 128
C_NOPE = 64
C_ROPE = 32
C_QK = C_NOPE + C_ROPE
ROPE_THETA = 10000.0
NEG_INF = float("-inf")
LOG2_E = math.log2(math.e)
TILE_GROUP = 4
VT_ROWS = HEAD_DIM + 16
VMEM_LIMIT = 56 * 1024 * 1024


def _nt_dot(a, b):
    return lax.dot_general(a, b, (((1,), (1,)), ((), ())), preferred_element_type=F32)


def _dot(a, b):
    return jnp.dot(a, b, preferred_element_type=F32)


def _rms(x, g):
    ms = jnp.mean(x * x, axis=-1, keepdims=True)
    return x * lax.rsqrt(ms + RMS_EPS) * g


def _lane_iota(shape):
    return lax.broadcasted_iota(jnp.int32, shape, len(shape) - 1)


def _pair_rms(y, gain):
    left = _lane_iota(y.shape) < HEAD_DIM
    y2 = y * y
    s_left = jnp.sum(jnp.where(left, y2, 0.0), axis=-1, keepdims=True)
    s_right = jnp.sum(y2, axis=-1, keepdims=True) - s_left
    r = jnp.where(left, lax.rsqrt(s_left / HEAD_DIM + RMS_EPS), lax.rsqrt(s_right / HEAD_DIM + RMS_EPS))
    return y * r * gain


def _split_kv(t, parity, ones_lane=None):
    lane = _lane_iota(t.shape)
    keep = (lane < HEAD_DIM) if parity == 0 else (lane >= HEAD_DIM)
    out = jnp.where(keep, t, jnp.zeros_like(t))
    if ones_lane is not None:
        out = jnp.where(lane == ones_lane, jnp.ones_like(t), out)
    return out


def _put_blocks(out_ref, slot, tile):
    for bk in range(tile.shape[0] // BLK):
        out_ref[bk, slot * BLK:(slot + 1) * BLK, :] = tile[bk * BLK:(bk + 1) * BLK].astype(out_ref.dtype)


def _proj_a_kernel(x_ref, g_ref, w_ref, gq_ref, gk_ref, q_out, kc_out, vt_out, qi_out, kic_out, wi_out):
    xn = _rms(x_ref[...], g_ref[...]).astype(MXU_DTYPE)
    for p in range(PAIRS):
        q = _dot(xn, w_ref[:, p * LANES:(p + 1) * LANES])
        _put_blocks(q_out, p, _pair_rms(q, gq_ref[...]))
    o = HEADS * HEAD_DIM
    k = _pair_rms(_dot(xn, w_ref[:, o:o + LANES]), gk_ref[...])
    _put_blocks(kc_out, 0, _split_kv(k, 0))
    _put_blocks(kc_out, 1, _split_kv(k, 1))
    o += LANES
    vt = jnp.transpose(_dot(xn, w_ref[:, o:o + LANES]))
    row = lax.broadcasted_iota(jnp.int32, vt.shape, 0)
    vt = jnp.where(row == HEAD_DIM, 1.0, vt)
    for bk in range(vt.shape[1] // BLK):
        vt_out[bk] = vt[:VT_ROWS, bk * BLK:(bk + 1) * BLK].astype(vt_out.dtype)
    o += LANES
    for hp in range(IDX_HEADS // 2):
        _put_blocks(qi_out, hp, _dot(xn, w_ref[:, o:o + LANES]) * IDX_DIM ** -0.5)
        o += LANES
    ki = _dot(xn, w_ref[:, o:o + LANES])
    _put_blocks(kic_out, 0, _split_kv(ki, 0))
    _put_blocks(kic_out, 1, _split_kv(ki, 1))
    o += LANES
    wi_out[...] = _dot(xn, w_ref[:, o:o + LANES]) * IDX_HEADS ** -0.5


def _proj_b_kernel(x_ref, g_ref, w_ref, gq_ref, gk_ref, q_out, k_out, v_out):
    xn = _rms(x_ref[...], g_ref[...]).astype(MXU_DTYPE)
    for p in range(PAIRS):
        q = _dot(xn, w_ref[:, p * LANES:(p + 1) * LANES])
        q_out[:, p * LANES:(p + 1) * LANES] = _pair_rms(q, gq_ref[...]).astype(q_out.dtype)
    o = HEADS * HEAD_DIM
    for c in range(B_KV_HEADS):
        k = _dot(xn, w_ref[:, o + c * LANES:o + (c + 1) * LANES])
        k_out[:, c * LANES:(c + 1) * LANES] = _pair_rms(k, gk_ref[...]).astype(k_out.dtype)
    o += B_KV_HEADS * LANES
    v_out[...] = _dot(xn, w_ref[:, o:o + B_KV_HEADS * LANES]).astype(v_out.dtype)


def _rope(y, cos_t, sin_t):
    lane = _lane_iota(y.shape)
    first_half = (lane >= C_NOPE) & (lane < C_NOPE + C_ROPE // 2)
    partner = jnp.where(first_half, pltpu.roll(y, LANES - C_ROPE // 2, 1), pltpu.roll(y, C_ROPE // 2, 1))
    return y * cos_t + partner * sin_t


def _proj_c_kernel(x_ref, g_ref, w_in_ref, gqa_ref, gkva_ref, wq_ref, wk_ref, wv_ref, gq_ref, gk_ref,
                   cos_ref, sin_ref, q_out, k_out, vt_out):
    xn = _rms(x_ref[...], g_ref[...]).astype(MXU_DTYPE)
    lat = _dot(xn, w_in_ref[...])
    qa = _rms(lat[:, :C_Q_RANK], gqa_ref[...]).astype(MXU_DTYPE)
    kva = _rms(lat[:, C_Q_RANK:C_Q_RANK + C_KV_RANK], gkva_ref[...]).astype(MXU_DTYPE)
    k_rope = lat[:, C_Q_RANK + C_KV_RANK:]
    cos_t, sin_t = cos_ref[...], sin_ref[...]
    v = _dot(kva, wv_ref[...])
    tm = v.shape[0]
    ones_row = jnp.where(lax.broadcasted_iota(jnp.int32, (VT_ROWS - HEAD_DIM, tm), 0) == 0, 1.0, 0.0)
    for p in range(PAIRS):
        v_t = jnp.transpose(v[:, p * LANES:(p + 1) * LANES])
        for e in range(2):
            h = 2 * p + e
            head_t = jnp.concatenate([v_t[e * HEAD_DIM:(e + 1) * HEAD_DIM], ones_row], axis=0)
            for bk in range(tm // BLK):
                vt_out[bk, h * VT_ROWS:(h + 1) * VT_ROWS, :] = (
                    head_t[:, bk * BLK:(bk + 1) * BLK].astype(vt_out.dtype))
    for h in range(HEADS):
        sl = slice(h * LANES, (h + 1) * LANES)
        q = _dot(qa, wq_ref[:, sl])
        q = q * lax.rsqrt(jnp.sum(q * q, axis=-1, keepdims=True) / C_QK + RMS_EPS) * gq_ref[...]
        q_out[:, sl] = _rope(q, cos_t, sin_t).astype(q_out.dtype)
        k = _dot(kva, wk_ref[:, sl]) + k_rope
        k = k * lax.rsqrt(jnp.sum(k * k, axis=-1, keepdims=True) / C_QK + RMS_EPS) * gk_ref[...]
        k_out[:, sl] = _rope(k, cos_t, sin_t).astype(k_out.dtype)


def _row_spec(tm, n):
    return pl.BlockSpec((tm, n), lambda i: (i, 0))


def _const_spec(shape):
    return pl.BlockSpec(shape, lambda i: (0,) * len(shape))


def _proj_call(kernel, x, consts, row_inputs, outs, tm=512):
    t = x.shape[0]
    in_specs = [_row_spec(tm, x.shape[1])] + [_const_spec(c.shape) for c in consts]
    in_specs += [_row_spec(tm, r.shape[1]) for r in row_inputs]
    out_shape, out_specs = [], []
    for kind, n, dtype in outs:
        if kind == "rows":
            out_shape.append(jax.ShapeDtypeStruct((t, n), dtype))
            out_specs.append(_row_spec(tm, n))
        else:
            out_shape.append(jax.ShapeDtypeStruct((t // BLK, n, LANES), dtype))
            out_specs.append(pl.BlockSpec((tm // BLK, n, LANES), lambda i: (i, 0, 0)))
    return pl.pallas_call(
        kernel,
        name=kernel.__name__.strip("_"),
        out_shape=out_shape,
        grid=(t // tm,),
        in_specs=in_specs,
        out_specs=out_specs,
        compiler_params=pltpu.CompilerParams(dimension_semantics=("parallel",), vmem_limit_bytes=VMEM_LIMIT),
    )(x, *consts, *row_inputs)


def _out_mlp_kernel(x_ref, a_ref, wo_ref, g_ref, wu_ref, wd_ref, o_ref, *, f_chunk):
    x1 = x_ref[...] + _dot(a_ref[...], wo_ref[...])
    h = _rms(x1, g_ref[...]).astype(MXU_DTYPE)
    acc = x1
    for c in range(D_FF // f_chunk):
        u = jnp.maximum(_dot(h, wu_ref[:, c * f_chunk:(c + 1) * f_chunk]), 0.0)
        acc = acc + _dot((u * u).astype(MXU_DTYPE), wd_ref[c * f_chunk:(c + 1) * f_chunk, :])
    o_ref[...] = acc


def _out_mlp_call(x, attn, w_out, g, w_up, w_down, tm=512, f_chunk=1024):
    t = x.shape[0]
    single = pl.Buffered(1)
    return pl.pallas_call(
        functools.partial(_out_mlp_kernel, f_chunk=f_chunk),
        name="out_mlp",
        out_shape=jax.ShapeDtypeStruct((t, D_MODEL), F32),
        grid=(t // tm,),
        in_specs=[
            _row_spec(tm, D_MODEL),
            _row_spec(tm, attn.shape[1]),
            pl.BlockSpec(w_out.shape, lambda i: (0, 0), pipeline_mode=single),
            _const_spec(g.shape),
            pl.BlockSpec(w_up.shape, lambda i: (0, 0), pipeline_mode=single),
            pl.BlockSpec(w_down.shape, lambda i: (0, 0), pipeline_mode=single),
        ],
        out_specs=_row_spec(tm, D_MODEL),
        compiler_params=pltpu.CompilerParams(dimension_semantics=("parallel",), vmem_limit_bytes=VMEM_LIMIT),
    )(x, attn, w_out, g, w_up, w_down)


def _swa_kernel(q_ref, kp_ref, kc_ref, vp_ref, vc_ref, bias_ref, sink_ref, o_ref):
    n = pl.program_id(1)
    row = lax.broadcasted_iota(jnp.int32, (BLK, 2 * BLK), 0)
    col = lax.broadcasted_iota(jnp.int32, (BLK, 2 * BLK), 1)
    rel = row + WINDOW - col
    ok = (rel >= 0) & (rel < WINDOW) & ((n > 0) | (col >= WINDOW))
    for p in range(PAIRS):
        c = p // (PAIRS // B_KV_HEADS)
        sl = slice(c * LANES, (c + 1) * LANES)
        k2 = jnp.concatenate([kp_ref[0, :, sl], kc_ref[0, :, sl]], axis=0)
        v2 = jnp.concatenate([vp_ref[0, :, sl], vc_ref[0, :, sl]], axis=0)
        q = q_ref[0, :, p * LANES:(p + 1) * LANES]
        out = jnp.zeros((BLK, LANES), F32)
        for e in range(2):
            h = 2 * p + e
            s = _nt_dot(q, _split_kv(k2, e)) + bias_ref[h]
            s = jnp.where(ok, s, NEG_INF)
            sink = sink_ref[h]
            m = jnp.maximum(jnp.max(s, axis=-1, keepdims=True), sink)
            ex = jnp.exp(s - m)
            denom = jnp.sum(ex, axis=-1, keepdims=True) + jnp.exp(sink - m)
            pr = (ex / denom).astype(MXU_DTYPE)
            out = out + _dot(pr, _split_kv(v2, e))
        o_ref[0, :, p * LANES:(p + 1) * LANES] = out.astype(o_ref.dtype)


def _swa_call(q, k2, v2, bias, sinks):
    b, l, _ = q.shape
    nb = l // BLK
    kv_w = k2.shape[-1]
    prev = lambda bi, n: (bi, jnp.maximum(n - 1, 0), 0)
    cur = lambda bi, n: (bi, n, 0)
    return pl.pallas_call(
        _swa_kernel,
        name="swa_attn",
        out_shape=jax.ShapeDtypeStruct(q.shape, MXU_DTYPE),
        grid=(b, nb),
        in_specs=[
            pl.BlockSpec((1, BLK, q.shape[-1]), cur),
            pl.BlockSpec((1, BLK, kv_w), prev),
            pl.BlockSpec((1, BLK, kv_w), cur),
            pl.BlockSpec((1, BLK, kv_w), prev),
            pl.BlockSpec((1, BLK, kv_w), cur),
            pl.BlockSpec(bias.shape, lambda bi, n: (0, 0, 0)),
            pl.BlockSpec(memory_space=pltpu.SMEM),
        ],
        out_specs=pl.BlockSpec((1, BLK, q.shape[-1]), cur),
        compiler_params=pltpu.CompilerParams(dimension_semantics=("parallel", "parallel"),
                                             vmem_limit_bytes=VMEM_LIMIT),
    )(q, k2, k2, v2, v2, bias, sinks)


def _pipeline_scratch(tk, lanes):
    return [pltpu.VMEM((2, 2, tk, lanes), F32), pltpu.VMEM((2, 2, tk, lanes), MXU_DTYPE),
            pltpu.VMEM((2, 2, 1, lanes), F32), pltpu.VMEM((2, 1, lanes), F32),
            pltpu.VMEM((2, VT_ROWS, lanes), F32)]


def _init_pipeline(s_scr, p_scr, alpha_scr, m_scr, acc_scr):
    m_scr[...] = jnp.full(m_scr.shape, NEG_INF, F32)
    acc_scr[...] = jnp.zeros(acc_scr.shape, F32)
    s_scr[1] = jnp.full(s_scr.shape[1:], NEG_INF, F32)
    p_scr[0] = jnp.zeros(p_scr.shape[1:], p_scr.dtype)
    alpha_scr[0] = jnp.ones(alpha_scr.shape[1:], F32)


def _softmax_stage(s_scr, p_scr, alpha_scr, m_scr, slot):
    for e in range(2):
        s = s_scr[slot, e]
        m_old = m_scr[e]
        m_new = jnp.maximum(m_old, jnp.max(s, axis=0, keepdims=True))
        m_safe = jnp.where(m_new == NEG_INF, 0.0, m_new)
        p_scr[slot, e] = jnp.exp2(s - m_safe).astype(p_scr.dtype)
        alpha_scr[slot, e] = jnp.exp2(m_old - m_safe)
        m_scr[e] = m_new


def _pv_stage(vts, p_scr, alpha_scr, acc_scr, slot):
    for e in range(2):
        acc_scr[e] = acc_scr[e] * alpha_scr[slot, e] + _dot(vts[e], p_scr[slot, e])


def _normalized_pair(acc_scr, lanes):
    a0, a1 = acc_scr[0, :, lanes], acc_scr[1, :, lanes]
    return jnp.concatenate([a0[:HEAD_DIM] / a0[HEAD_DIM:HEAD_DIM + 1],
                            a1[:HEAD_DIM] / a1[HEAD_DIM:HEAD_DIM + 1]], axis=0)


def _mla_kernel(q_ref, k_ref, vt_ref, o_ref, s_scr, p_scr, alpha_scr, m_scr, acc_scr, *, tq, tk):
    seq = q_ref.shape[1]
    n_diag = tq // tk
    sub = tk // BLK
    key_row = lax.broadcasted_iota(jnp.int32, (tk, tq), 0)
    query_col = lax.broadcasted_iota(jnp.int32, (tk, tq), 1)

    def q_tile(qt, _):
        q0 = pl.multiple_of(qt * tq, tq)
        qs = [q_ref[0, pl.ds(q0, tq), e * LANES:(e + 1) * LANES] for e in range(2)]
        n_full = qt * n_diag
        _init_pipeline(s_scr, p_scr, alpha_scr, m_scr, acc_scr)

        def vts(kt):
            kt = jnp.maximum(kt, 0)
            return [jnp.concatenate([vt_ref[0, kt * sub + c, e * VT_ROWS:(e + 1) * VT_ROWS, :]
                                     for c in range(sub)], axis=1) for e in range(2)]

        def qk_stage(kt, slot, diag):
            k0 = pl.multiple_of(kt * tk, tk)
            for e in range(2):
                s = _nt_dot(k_ref[0, pl.ds(k0, tk), e * LANES:(e + 1) * LANES], qs[e])
                if diag is not None:
                    s = jnp.where(diag * tk + key_row <= query_col, s, NEG_INF)
                s_scr[slot, e] = s

        def pair_step(kk, carry):
            for par in range(2):
                kt = 2 * kk + par
                _pv_stage(vts(kt - 2), p_scr, alpha_scr, acc_scr, par)
                qk_stage(kt, par, None)
                _softmax_stage(s_scr, p_scr, alpha_scr, m_scr, 1 - par)
            return carry

        lax.fori_loop(0, n_full // 2, pair_step, 0)
        for c in range(n_diag + 2):
            par = c % 2
            _pv_stage(vts(n_full + c - 2), p_scr, alpha_scr, acc_scr, par)
            if c < n_diag:
                qk_stage(n_full + c, par, c)
            if c < n_diag + 1:
                _softmax_stage(s_scr, p_scr, alpha_scr, m_scr, 1 - par)
        o_ref[0, pl.ds(q0, tq), :] = jnp.transpose(_normalized_pair(acc_scr, slice(None))).astype(o_ref.dtype)
        return 0

    lax.fori_loop(0, seq // tq, q_tile, 0)


def _mla_call(q, k, vt, tq=1024, tk=256):
    b, l, _ = q.shape
    tq, tk = min(tq, l), min(tk, l)
    assert (tq // tk) % 2 == 0 and l % tq == 0
    nb = l // BLK
    vt = vt.reshape(b, nb, HEADS * VT_ROWS, LANES)
    return pl.pallas_call(
        functools.partial(_mla_kernel, tq=tq, tk=tk),
        name="mla_attn",
        out_shape=jax.ShapeDtypeStruct((b, l, HEADS * HEAD_DIM), MXU_DTYPE),
        grid=(b, PAIRS),
        in_specs=[
            pl.BlockSpec((1, l, 2 * LANES), lambda bi, p: (bi, 0, p)),
            pl.BlockSpec((1, l, 2 * LANES), lambda bi, p: (bi, 0, p)),
            pl.BlockSpec((1, nb, 2 * VT_ROWS, LANES), lambda bi, p: (bi, 0, p, 0)),
        ],
        out_specs=pl.BlockSpec((1, l, LANES), lambda bi, p: (bi, 0, p)),
        scratch_shapes=_pipeline_scratch(tk, tq),
        compiler_params=pltpu.CompilerParams(dimension_semantics=("parallel", "parallel"),
                                             vmem_limit_bytes=VMEM_LIMIT),
    )(q, k, vt)


def _dsa_kernel(q_ref, qi_ref, wi_ref, kc_ref, vt_ref, kic_ref, bias_ref, tri_ref, o_ref,
                idx_scr, s_scr, p_scr, alpha_scr, m_scr, acc_scr, *, k_sel):
    i = pl.program_id(1)
    n_tiles = i + 1
    n_groups = i // TILE_GROUP + 1
    key_row = lax.broadcasted_iota(jnp.int32, (BLK, BLK), 0)
    query_col = lax.broadcasted_iota(jnp.int32, (BLK, BLK), 1)
    int_min = jnp.int32(-2 ** 31)

    def admissible(j):
        return (j < i) | ((j == i) & (key_row <= query_col))

    wt = jnp.transpose(wi_ref[0])
    qi = qi_ref[0, 0]

    def score_group(g, carry):
        for u in range(TILE_GROUP):
            j = g * TILE_GROUP + u
            d = jnp.maximum(_nt_dot(kic_ref[0, j], qi), 0.0)
            sc = jnp.zeros((BLK, BLK), F32)
            for hp in range(IDX_HEADS // 2):
                for e in range(2):
                    h = 2 * hp + e
                    sc = sc + wt[h:h + 1, :] * d[e * BLK:(e + 1) * BLK, hp * BLK:(hp + 1) * BLK]
            idx_scr[j] = jnp.where(admissible(j), sc, NEG_INF)
        return carry

    lax.fori_loop(0, n_groups, score_group, 0)

    def count(pred):
        def body(g, c):
            for u in range(TILE_GROUP):
                c = c + jnp.where(pred(idx_scr[g * TILE_GROUP + u]), 1.0, 0.0)
            return c
        c = lax.fori_loop(0, n_groups, body, jnp.zeros((BLK, BLK), F32))
        return jnp.sum(c, axis=0, keepdims=True)

    def key_to_float(key):
        bits = jnp.where(key >= 0, key, key ^ jnp.int32(0x7FFFFFFF))
        return lax.bitcast_convert_type(bits, F32)

    def search():
        def bit_step(it, u):
            cand_u = u | lax.shift_left(jnp.int32(1), 31 - it)
            cand = key_to_float(cand_u ^ int_min)
            take = count(lambda s: s >= cand) >= k_sel
            return jnp.where(take, cand_u, u)
        u = lax.fori_loop(0, 32, bit_step, jnp.zeros((1, BLK), jnp.int32))
        t = key_to_float(u ^ int_min)
        return jnp.where(t != t, NEG_INF, t)

    thr = lax.cond(n_tiles * BLK > k_sel, search, lambda: jnp.full((1, BLK), NEG_INF, F32))

    n_gt = count(lambda s: s > thr)
    n_ge = count(lambda s: s >= thr)
    tied = (n_ge > k_sel) & (thr > NEG_INF)

    @pl.when(jnp.max(jnp.where(tied, 1.0, 0.0)) > 0.0)
    def _():
        need = k_sel - n_gt

        def tie_tile(j, seen):
            s = idx_scr[j]
            eq = s == thr
            eq_f = jnp.where(eq, 1.0, 0.0).astype(MXU_DTYPE)
            rank = seen + _dot(tri_ref[...], eq_f)
            idx_scr[j] = jnp.where(eq & tied & (rank > need), NEG_INF, s)
            return seen + jnp.sum(eq_f.astype(F32), axis=0, keepdims=True)

        lax.fori_loop(0, n_tiles, tie_tile, jnp.zeros((1, BLK), F32))

    q = q_ref[0, 0]
    _init_pipeline(s_scr, p_scr, alpha_scr, m_scr, acc_scr)

    def qk_stage(j, slot, near, valid=None):
        sel = idx_scr[j] >= thr
        if near is not None:
            sel = sel & admissible(j)
        if valid is not None:
            sel = sel & valid
        mask = jnp.where(sel, 0.0, NEG_INF)
        mask = jnp.concatenate([mask] * PAIRS, axis=1)
        s = _nt_dot(kc_ref[0, j], q)
        if near is not None:
            s = s + bias_ref[near]
        for e in range(2):
            s_scr[slot, e] = s[e * BLK:(e + 1) * BLK] + mask

    def vts(j):
        vt = vt_ref[0, jnp.maximum(j, 0)]
        return [vt, vt]

    n_far = jnp.maximum(i - 1, 0)
    n_pairs = n_far // 2

    def pair_step(kk, carry):
        for par in range(2):
            j = 2 * kk + par
            _pv_stage(vts(j - 2), p_scr, alpha_scr, acc_scr, par)
            qk_stage(j, par, None)
            _softmax_stage(s_scr, p_scr, alpha_scr, m_scr, 1 - par)
        return carry

    lax.fori_loop(0, n_pairs, pair_step, 0)
    j0 = 2 * n_pairs
    tail = [(j0, None, n_far > j0), (n_far, 0, i >= 1), (i, 1, None)]
    done = [j0 - 2, j0 - 1] + [t[0] for t in tail]
    for c in range(len(tail) + 2):
        par = c % 2
        _pv_stage(vts(done[c]), p_scr, alpha_scr, acc_scr, par)
        if c < len(tail):
            qk_stage(tail[c][0], par, tail[c][1], tail[c][2])
        if c < len(tail) + 1:
            _softmax_stage(s_scr, p_scr, alpha_scr, m_scr, 1 - par)

    for p in range(PAIRS):
        sl = slice(p * BLK, (p + 1) * BLK)
        o_ref[0, :, sl] = jnp.transpose(_normalized_pair(acc_scr, sl)).astype(o_ref.dtype)


def _dsa_call(q, qi, wi, kc, vt, kic, bias_t, tri, batch):
    nb = q.shape[0] // batch
    l = nb * BLK
    k_sel = min(TOPK_MAX, l // 4)
    assert nb % TILE_GROUP == 0
    per_batch =lambda a: a.reshape(batch, nb, *a.shape[1:])
    q, qi, kc, vt, kic = map(per_batch, (q, qi, kc, vt, kic))
    blk = lambda a: pl.BlockSpec((1, 1) + a.shape[2:], lambda bi, n: (bi, n, 0, 0))
    seq = lambda a: pl.BlockSpec((1,) + a.shape[1:], lambda bi, n: (bi, 0, 0, 0))
    return pl.pallas_call(
        functools.partial(_dsa_kernel, k_sel=k_sel),
        name="dsa_attn",
        out_shape=jax.ShapeDtypeStruct((batch, l, HEADS * HEAD_DIM), MXU_DTYPE),
        grid=(batch, nb),
        in_specs=[
            blk(q),
            blk(qi),
            pl.BlockSpec((1, BLK, LANES), lambda bi, n: (bi, n, 0)),
            seq(kc),
            seq(vt),
            seq(kic),
            pl.BlockSpec(bias_t.shape, lambda bi, n: (0, 0, 0)),
            pl.BlockSpec(tri.shape, lambda bi, n: (0, 0)),
        ],
        out_specs=pl.BlockSpec((1, BLK, HEADS * HEAD_DIM), lambda bi, n: (bi, n, 0)),
        scratch_shapes=[pltpu.VMEM((nb, BLK, BLK), F32)] + _pipeline_scratch(BLK, PAIRS * BLK),
        compiler_params=pltpu.CompilerParams(dimension_semantics=("parallel", "arbitrary"),
                                             vmem_limit_bytes=VMEM_LIMIT),
    )(q, qi, wi.reshape(batch, l, LANES), kc, vt, kic, bias_t, tri)


def _t5_bucket(rel):
    n = jnp.maximum(rel, 0)
    max_exact = N_BUCKETS // 2
    nf = jnp.maximum(n, 1).astype(F32)
    large = max_exact + (jnp.log(nf / max_exact) / math.log(MAX_DISTANCE / max_exact)
                         * (N_BUCKETS - max_exact)).astype(jnp.int32)
    large = jnp.minimum(large, N_BUCKETS - 1)
    return jnp.where(n < max_exact, n, large)


def _band_bias(rel_bias):
    qi = jnp.arange(BLK, dtype=jnp.int32)[:, None] + BLK
    kj = jnp.arange(2 * BLK, dtype=jnp.int32)[None, :]
    return jnp.transpose(rel_bias[_t5_bucket(qi - kj)], (2, 0, 1))


def _pair_gain(g, scale=1.0):
    return (jnp.concatenate([g, g]) * scale).reshape(1, LANES).astype(F32)


def _dup(w, n_heads):
    d = w.shape[0]
    w = w.reshape(d, n_heads, 1, HEAD_DIM)
    return jnp.broadcast_to(w, (d, n_heads, 2, HEAD_DIM)).reshape(d, n_heads * LANES)


def _layer_a(x, g_mix, w_in, q_gain, k_gain):
    nq = HEADS * HEAD_DIM
    sizes = [nq, HEAD_DIM, HEAD_DIM, IDX_HEADS * IDX_DIM, IDX_DIM, IDX_HEADS]
    wq, wk, wv, wqi, wki, wwi = jnp.split(w_in, np.cumsum(sizes)[:-1].tolist(), axis=1)
    pad_to_lanes = lambda a: jnp.pad(a, ((0, 0), (0, LANES - a.shape[1])))
    w = jnp.concatenate([wq, _dup(wk, 1), pad_to_lanes(wv), wqi, _dup(wki, 1), pad_to_lanes(wwi)],
                        axis=1).astype(MXU_DTYPE)
    consts = [g_mix.reshape(1, -1), w, _pair_gain(q_gain, HEAD_DIM ** -0.5 * LOG2_E), _pair_gain(k_gain)]
    outs = [("stacked", PAIRS * BLK, MXU_DTYPE), ("stacked", 2 * BLK, MXU_DTYPE), ("stacked", VT_ROWS, MXU_DTYPE),
            ("stacked", IDX_HEADS // 2 * BLK, MXU_DTYPE), ("stacked", 2 * BLK, MXU_DTYPE), ("rows", LANES, F32)]
    return _proj_call(_proj_a_kernel, x, consts, [], outs)


def _layer_b(x, g_mix, w_in, q_gain, k_gain):
    nq = HEADS * HEAD_DIM
    nkv = B_KV_HEADS * HEAD_DIM
    wq, wk, wv = jnp.split(w_in, [nq, nq + nkv], axis=1)
    w = jnp.concatenate([wq, _dup(wk, B_KV_HEADS), _dup(wv, B_KV_HEADS)], axis=1).astype(MXU_DTYPE)
    consts = [g_mix.reshape(1, -1), w, _pair_gain(q_gain, HEAD_DIM ** -0.5), _pair_gain(k_gain)]
    outs = [("rows", n, MXU_DTYPE) for n in (nq, B_KV_HEADS * LANES, B_KV_HEADS * LANES)]
    return _proj_call(_proj_b_kernel, x, consts, [], outs)


def _pad_heads(w, width):
    d = w.shape[0]
    w = w.reshape(d, HEADS, width)
    return jnp.pad(w, ((0, 0), (0, 0), (0, LANES - width))).reshape(d, HEADS * LANES)


def _layer_c(x, g_mix, w_in, q_a_gain, w_q_b, kv_a_gain, w_kv_b, q_gain, k_gain, cos_t, sin_t):
    d = w_in.shape[0]
    w_rope = jnp.zeros((d, LANES), w_in.dtype).at[:, C_NOPE:C_QK].set(w_in[:, C_Q_RANK + C_KV_RANK:])
    w_in_p = jnp.concatenate([w_in[:, :C_Q_RANK + C_KV_RANK], w_rope], axis=1).astype(MXU_DTYPE)
    wq = _pad_heads(w_q_b, C_QK).astype(MXU_DTYPE)
    w_kv = w_kv_b.reshape(C_KV_RANK, HEADS, C_NOPE + HEAD_DIM)
    wk = _pad_heads(w_kv[:, :, :C_NOPE].reshape(C_KV_RANK, -1), C_NOPE).astype(MXU_DTYPE)
    wv = w_kv[:, :, C_NOPE:].reshape(C_KV_RANK, -1).astype(MXU_DTYPE)
    pad_gain = lambda g, s: (jnp.pad(g, (0, LANES - C_QK)) * s).reshape(1, LANES).astype(F32)
    consts = [g_mix.reshape(1, -1), w_in_p, q_a_gain.reshape(1, -1), kv_a_gain.reshape(1, -1), wq, wk, wv,
              pad_gain(q_gain, C_QK ** -0.5 * LOG2_E), pad_gain(k_gain, 1.0)]
    outs = [("rows", HEADS * LANES, MXU_DTYPE), ("rows", HEADS * LANES, MXU_DTYPE),
            ("stacked", HEADS * VT_ROWS, MXU_DTYPE)]
    return _proj_call(_proj_c_kernel, x, consts, [cos_t, sin_t], outs, tm=256)


def kernel(x, positions, rel_bias, norm_mix, norm_mlp, w_up, w_down, a_w_in, a_q_gain, a_k_gain, a_w_out,
           b_w_in, b_q_gain, b_k_gain, b_sinks, b_w_out, c_w_in, c_q_a_gain, c_w_q_b, c_kv_a_gain, c_w_kv_b,
           c_q_gain, c_k_gain, c_w_out):
    b, l, d = x.shape
    t = b * l
    depth = norm_mix.shape[0]

    band = _band_bias(rel_bias.astype(F32))
    far = rel_bias[N_BUCKETS - 1].astype(F32)[:, None, None]
    bias_t = jnp.transpose(((band - far) * LOG2_E).reshape(PAIRS, 2, BLK, 2, BLK), (3, 1, 4, 0, 2)).reshape(
        2, 2 * BLK, PAIRS * BLK)
    tri = jnp.asarray(np.arange(BLK)[:, None] >= np.arange(BLK)[None, :], MXU_DTYPE)

    inv_freq = ROPE_THETA ** (-jnp.arange(0, C_ROPE, 2, dtype=F32) / C_ROPE)
    ang = positions.astype(F32).reshape(t, 1) * inv_freq
    cos, sin = jnp.cos(ang), jnp.sin(ang)
    cos_t = jnp.concatenate([jnp.ones((t, C_NOPE), F32), cos, cos, jnp.zeros((t, LANES - C_QK), F32)], axis=1)
    sin_t = jnp.concatenate([jnp.zeros((t, C_NOPE), F32), -sin, sin, jnp.zeros((t, LANES - C_QK), F32)], axis=1)

    xf = x.reshape(t, d).astype(F32)
    ia = ib = ic = 0
    for i in range(depth):
        kind = i % 3
        g_mix = norm_mix[i]
        if kind == 0:
            q, kc, vt, qi, kic, wi = _layer_a(xf, g_mix, a_w_in[ia], a_q_gain[ia], a_k_gain[ia])
            attn = _dsa_call(q, qi, wi, kc, vt, kic, bias_t, tri, b)
            w_out = a_w_out[ia]
            ia += 1
        elif kind == 1:
            q, k2, v2 = _layer_b(xf, g_mix, b_w_in[ib], b_q_gain[ib], b_k_gain[ib])
            r3 = lambda a: a.reshape(b, l, a.shape[-1])
            attn = _swa_call(r3(q), r3(k2), r3(v2), band, b_sinks[ib].astype(F32))
            w_out = b_w_out[ib]
            ib += 1
        else:
            q, k, vt = _layer_c(xf, g_mix, c_w_in[ic], c_q_a_gain[ic], c_w_q_b[ic], c_kv_a_gain[ic],
                               c_w_kv_b[ic], c_q_gain[ic], c_k_gain[ic], cos_t, sin_t)
            r3 = lambda a: a.reshape(b, l, a.shape[-1])
            attn = _mla_call(r3(q), r3(k), vt)
            w_out = c_w_out[ic]
            ic += 1
        xf = _out_mlp_call(xf, attn.reshape(t, -1), w_out.astype(MXU_DTYPE), norm_mlp[i].reshape(1, -1),
                           w_up[i].astype(MXU_DTYPE), w_down[i].astype(MXU_DTYPE))
    return xf.reshape(b, l, d).astype(x.dtype)
```

```python
import functools
import math

import numpy as np
import jax
import jax.numpy as jnp
from jax import lax
from jax.experimental import pallas as pl
from jax.experimental.pallas import tpu as pltpu

F32 = jnp.float32
MXU_DTYPE = jnp.bfloat16

D_MODEL = 1024
D_FF = 4 * D_MODEL
RMS_EPS = 1e-6
N_BUCKETS = 32
MAX_DISTANCE = 128
HEADS = 16
HEAD_DIM = 64
PAIRS = HEADS // 2
LANES = 128
BLK = 128
IDX_HEADS = 8
IDX_DIM = 64
TOPK_MAX = 256
B_KV_HEADS = 4
WINDOW = 128
C_Q_RANK = 256
C_KV_RANK = 128
C_NOPE = 64
C_ROPE = 32
C_QK = C_NOPE + C_ROPE
ROPE_THETA = 10000.0
NEG_INF = float("-inf")
LOG2_E = math.log2(math.e)
HALF_RANGE = 1 << 15
TILE_GROUP = 4
VT_ROWS = HEAD_DIM + 16
VMEM_LIMIT = 56 * 1024 * 1024


def _nt_dot(a, b):
    return lax.dot_general(a, b, (((1,), (1,)), ((), ())), preferred_element_type=F32)


def _dot(a, b):
    return jnp.dot(a, b, preferred_element_type=F32)


def _rms(x, g):
    ms = jnp.mean(x * x, axis=-1, keepdims=True)
    return x * lax.rsqrt(ms + RMS_EPS) * g


def _lane_iota(shape):
    return lax.broadcasted_iota(jnp.int32, shape, len(shape) - 1)


def _pair_rms(y, gain):
    left = _lane_iota(y.shape) < HEAD_DIM
    y2 = y * y
    s_left = jnp.sum(jnp.where(left, y2, 0.0), axis=-1, keepdims=True)
    s_right = jnp.sum(y2, axis=-1, keepdims=True) - s_left
    r = jnp.where(left, lax.rsqrt(s_left / HEAD_DIM + RMS_EPS), lax.rsqrt(s_right / HEAD_DIM + RMS_EPS))
    return y * r * gain


def _split_kv(t, parity, ones_lane=None):
    lane = _lane_iota(t.shape)
    keep = (lane < HEAD_DIM) if parity == 0 else (lane >= HEAD_DIM)
    out = jnp.where(keep, t, jnp.zeros_like(t))
    if ones_lane is not None:
        out = jnp.where(lane == ones_lane, jnp.ones_like(t), out)
    return out


def _put_blocks(out_ref, slot, tile):
    for bk in range(tile.shape[0] // BLK):
        out_ref[bk, slot * BLK:(slot + 1) * BLK, :] = tile[bk * BLK:(bk + 1) * BLK].astype(out_ref.dtype)


def _proj_a_kernel(x_ref, g_ref, w_ref, gq_ref, gk_ref, q_out, kc_out, vt_out, qi_out, kic_out, wi_out):
    xn = _rms(x_ref[...], g_ref[...]).astype(MXU_DTYPE)
    for p in range(PAIRS):
        q = _dot(xn, w_ref[:, p * LANES:(p + 1) * LANES])
        _put_blocks(q_out, p, _pair_rms(q, gq_ref[...]))
    o = HEADS * HEAD_DIM
    k = _pair_rms(_dot(xn, w_ref[:, o:o + LANES]), gk_ref[...])
    _put_blocks(kc_out, 0, _split_kv(k, 0))
    _put_blocks(kc_out, 1, _split_kv(k, 1))
    o += LANES
    _put_vt(vt_out, 0, _dot(xn, w_ref[:, o:o + LANES]))
    o += LANES
    for hp in range(IDX_HEADS // 2):
        _put_blocks(qi_out, hp, _dot(xn, w_ref[:, o:o + LANES]) * IDX_DIM ** -0.5)
        o += LANES
    ki = _dot(xn, w_ref[:, o:o + LANES])
    _put_blocks(kic_out, 0, _split_kv(ki, 0))
    _put_blocks(kic_out, 1, _split_kv(ki, 1))
    o += LANES
    wi_out[...] = _dot(xn, w_ref[:, o:o + LANES]) * IDX_HEADS ** -0.5


def _put_vt(vt_out, slot, v):
    vt = jnp.transpose(v)
    row = lax.broadcasted_iota(jnp.int32, vt.shape, 0)
    vt = jnp.where(row == HEAD_DIM, 1.0, vt)
    for bk in range(vt.shape[1] // BLK):
        vt_out[bk, slot * VT_ROWS:(slot + 1) * VT_ROWS, :] = (
            vt[:VT_ROWS, bk * BLK:(bk + 1) * BLK].astype(vt_out.dtype))


def _proj_b_kernel(x_ref, g_ref, w_ref, gq_ref, gk_ref, q_out, kc_out, vt_out):
    xn = _rms(x_ref[...], g_ref[...]).astype(MXU_DTYPE)
    for p in range(PAIRS):
        q = _dot(xn, w_ref[:, p * LANES:(p + 1) * LANES])
        _put_blocks(q_out, p, _pair_rms(q, gq_ref[...]))
    o = HEADS * HEAD_DIM
    for c in range(B_KV_HEADS):
        k = _pair_rms(_dot(xn, w_ref[:, o + c * LANES:o + (c + 1) * LANES]), gk_ref[...])
        _put_blocks(kc_out, 2 * c, _split_kv(k, 0))
        _put_blocks(kc_out, 2 * c + 1, _split_kv(k, 1))
    o += B_KV_HEADS * LANES
    for c in range(B_KV_HEADS):
        _put_vt(vt_out, c, _dot(xn, w_ref[:, o + c * LANES:o + (c + 1) * LANES]))


def _rope(y, cos_t, sin_t):
    lane = _lane_iota(y.shape)
    first_half = (lane >= C_NOPE) & (lane < C_NOPE + C_ROPE // 2)
    partner = jnp.where(first_half, pltpu.roll(y, LANES - C_ROPE // 2, 1), pltpu.roll(y, C_ROPE // 2, 1))
    return y * cos_t + partner * sin_t


def _proj_c_kernel(x_ref, g_ref, w_in_ref, gqa_ref, gkva_ref, wq_ref, wk_ref, wv_ref, gq_ref, gk_ref,
                   cos_ref, sin_ref, q_out, k_out, vt_out):
    xn = _rms(x_ref[...], g_ref[...]).astype(MXU_DTYPE)
    lat = _dot(xn, w_in_ref[...])
    qa = _rms(lat[:, :C_Q_RANK], gqa_ref[...]).astype(MXU_DTYPE)
    kva = _rms(lat[:, C_Q_RANK:C_Q_RANK + C_KV_RANK], gkva_ref[...]).astype(MXU_DTYPE)
    k_rope = lat[:, C_Q_RANK + C_KV_RANK:]
    cos_t, sin_t = cos_ref[...], sin_ref[...]
    v = _dot(kva, wv_ref[...])
    tm = v.shape[0]
    ones_row = jnp.where(lax.broadcasted_iota(jnp.int32, (VT_ROWS - HEAD_DIM, tm), 0) == 0, 1.0, 0.0)
    for p in range(PAIRS):
        v_t = jnp.transpose(v[:, p * LANES:(p + 1) * LANES])
        for e in range(2):
            h = 2 * p + e
            head_t = jnp.concatenate([v_t[e * HEAD_DIM:(e + 1) * HEAD_DIM], ones_row], axis=0)
            for bk in range(tm // BLK):
                vt_out[bk, h * VT_ROWS:(h + 1) * VT_ROWS, :] = (
                    head_t[:, bk * BLK:(bk + 1) * BLK].astype(vt_out.dtype))
    for h in range(HEADS):
        sl = slice(h * LANES, (h + 1) * LANES)
        q = _dot(qa, wq_ref[:, sl])
        q = q * lax.rsqrt(jnp.sum(q * q, axis=-1, keepdims=True) / C_QK + RMS_EPS) * gq_ref[...]
        q_out[:, sl] = _rope(q, cos_t, sin_t).astype(q_out.dtype)
        k = _dot(kva, wk_ref[:, sl]) + k_rope
        k = k * lax.rsqrt(jnp.sum(k * k, axis=-1, keepdims=True) / C_QK + RMS_EPS) * gk_ref[...]
        k_out[:, sl] = _rope(k, cos_t, sin_t).astype(k_out.dtype)


def _row_spec(tm, n):
    return pl.BlockSpec((tm, n), lambda i: (i, 0))


def _const_spec(shape):
    return pl.BlockSpec(shape, lambda i: (0,) * len(shape))


def _proj_call(kernel, x, consts, row_inputs, outs, tm=512):
    t = x.shape[0]
    in_specs = [_row_spec(tm, x.shape[1])] + [_const_spec(c.shape) for c in consts]
    in_specs += [_row_spec(tm, r.shape[1]) for r in row_inputs]
    out_shape, out_specs = [], []
    for kind, n, dtype in outs:
        if kind == "rows":
            out_shape.append(jax.ShapeDtypeStruct((t, n), dtype))
            out_specs.append(_row_spec(tm, n))
        else:
            out_shape.append(jax.ShapeDtypeStruct((t // BLK, n, LANES), dtype))
            out_specs.append(pl.BlockSpec((tm // BLK, n, LANES), lambda i: (i, 0, 0)))
    return pl.pallas_call(
        kernel,
        name=kernel.__name__.strip("_"),
        out_shape=out_shape,
        grid=(t // tm,),
        in_specs=in_specs,
        out_specs=out_specs,
        compiler_params=pltpu.CompilerParams(dimension_semantics=("parallel",), vmem_limit_bytes=VMEM_LIMIT),
    )(x, *consts, *row_inputs)


def _out_mlp_kernel(x_ref, a_ref, wo_ref, g_ref, wu_ref, wd_ref, o_ref, *, f_chunk):
    x1 = x_ref[...] + _dot(a_ref[...], wo_ref[...])
    h = _rms(x1, g_ref[...]).astype(MXU_DTYPE)
    acc = x1
    for c in range(D_FF // f_chunk):
        u = jnp.maximum(_dot(h, wu_ref[:, c * f_chunk:(c + 1) * f_chunk]), 0.0)
        acc = acc + _dot((u * u).astype(MXU_DTYPE), wd_ref[c * f_chunk:(c + 1) * f_chunk, :])
    o_ref[...] = acc


def _out_mlp_call(x, attn, w_out, g, w_up, w_down, tm=512, f_chunk=1024):
    t = x.shape[0]
    single = pl.Buffered(1)
    return pl.pallas_call(
        functools.partial(_out_mlp_kernel, f_chunk=f_chunk),
        name="out_mlp",
        out_shape=jax.ShapeDtypeStruct((t, D_MODEL), F32),
        grid=(t // tm,),
        in_specs=[
            _row_spec(tm, D_MODEL),
            _row_spec(tm, attn.shape[1]),
            pl.BlockSpec(w_out.shape, lambda i: (0, 0), pipeline_mode=single),
            _const_spec(g.shape),
            pl.BlockSpec(w_up.shape, lambda i: (0, 0), pipeline_mode=single),
            pl.BlockSpec(w_down.shape, lambda i: (0, 0), pipeline_mode=single),
        ],
        out_specs=_row_spec(tm, D_MODEL),
        compiler_params=pltpu.CompilerParams(dimension_semantics=("parallel",), vmem_limit_bytes=VMEM_LIMIT),
    )(x, attn, w_out, g, w_up, w_down)


def _swa_kernel(q_ref, kp_ref, kc_ref, vp_ref, vc_ref, bias_ref, sink_ref, o_ref):
    n = pl.program_id(1)
    group_pairs = PAIRS // B_KV_HEADS
    w = group_pairs * BLK
    key = lax.broadcasted_iota(jnp.int32, (2 * BLK, w), 0)
    query = lax.broadcasted_iota(jnp.int32, (2 * BLK, w), 1) & (BLK - 1)
    rel = query + WINDOW - key
    ok = (rel >= 0) & (rel < WINDOW) & ((n > 0) | (key >= WINDOW))
    for c in range(B_KV_HEADS):
        kk = jnp.concatenate([kp_ref[0, 0, c * 2 * BLK:(c + 1) * 2 * BLK, :],
                              kc_ref[0, 0, c * 2 * BLK:(c + 1) * 2 * BLK, :]], axis=0)
        s = _nt_dot(kk, q_ref[0, 0, c * w:(c + 1) * w, :])
        vt = jnp.concatenate([vp_ref[0, 0, c * VT_ROWS:(c + 1) * VT_ROWS, :],
                              vc_ref[0, 0, c * VT_ROWS:(c + 1) * VT_ROWS, :]], axis=1)
        halves = []
        for e in range(2):
            se = jnp.concatenate([s[e * BLK:(e + 1) * BLK], s[(2 + e) * BLK:(3 + e) * BLK]], axis=0)
            se = jnp.where(ok, se + bias_ref[c, e], NEG_INF)
            sink = sink_ref[c, e]
            m = jnp.maximum(jnp.max(se, axis=0, keepdims=True), sink)
            acc = _dot(vt, jnp.exp(se - m).astype(MXU_DTYPE))
            halves.append(acc[:HEAD_DIM] / (acc[HEAD_DIM:HEAD_DIM + 1] + jnp.exp(sink - m)))
        for pg in range(group_pairs):
            sl = slice(pg * BLK, (pg + 1) * BLK)
            pair = jnp.concatenate([halves[0][:, sl], halves[1][:, sl]], axis=0)
            p = c * group_pairs + pg
            o_ref[0, :, p * LANES:(p + 1) * LANES] = jnp.transpose(pair).astype(o_ref.dtype)


def _swa_call(q, kc, vt, bias_t, sinks_t, batch):
    nb = q.shape[0] // batch
    per_batch = lambda a: a.reshape(batch, nb, *a.shape[1:])
    q, kc, vt = map(per_batch, (q, kc, vt))
    prev = lambda a: pl.BlockSpec((1, 1) + a.shape[2:], lambda bi, n: (bi, jnp.maximum(n - 1, 0), 0, 0))
    cur = lambda a: pl.BlockSpec((1, 1) + a.shape[2:], lambda bi, n: (bi, n, 0, 0))
    return pl.pallas_call(
        _swa_kernel,
        name="swa_attn",
        out_shape=jax.ShapeDtypeStruct((batch, nb * BLK, HEADS * HEAD_DIM), MXU_DTYPE),
        grid=(batch, nb),
        in_specs=[cur(q), prev(kc), cur(kc), prev(vt), cur(vt),
                  pl.BlockSpec(bias_t.shape, lambda bi, n: (0, 0, 0, 0)),
                  pl.BlockSpec(sinks_t.shape, lambda bi, n: (0, 0, 0, 0))],
        out_specs=pl.BlockSpec((1, BLK, HEADS * HEAD_DIM), lambda bi, n: (bi, n, 0)),
        compiler_params=pltpu.CompilerParams(dimension_semantics=("parallel", "parallel"),
                                             vmem_limit_bytes=VMEM_LIMIT),
    )(q, kc, kc, vt, vt, bias_t, sinks_t)


def _pipeline_scratch(tk, n_chunks, lanes):
    return [pltpu.VMEM((2, 2, n_chunks, tk, lanes), F32), pltpu.VMEM((2, 2, n_chunks, tk, lanes), MXU_DTYPE),
            pltpu.VMEM((2, 2, n_chunks, 1, lanes), F32), pltpu.VMEM((2, n_chunks, 1, lanes), F32),
            pltpu.VMEM((2, n_chunks, VT_ROWS, lanes), F32)]


def _init_pipeline(s_scr, p_scr, alpha_scr, m_scr, acc_scr):
    m_scr[...] = jnp.full(m_scr.shape, NEG_INF, F32)
    acc_scr[...] = jnp.zeros(acc_scr.shape, F32)
    s_scr[1] = jnp.full(s_scr.shape[1:], NEG_INF, F32)
    p_scr[0] = jnp.zeros(p_scr.shape[1:], p_scr.dtype)
    alpha_scr[0] = jnp.ones(alpha_scr.shape[1:], F32)


def _softmax_stage(s_scr, p_scr, alpha_scr, m_scr, slot, chunks=(0,)):
    for e in range(2):
        for ch in chunks:
            s = s_scr[slot, e, ch]
            m_old = m_scr[e, ch]
            m_new = jnp.maximum(m_old, jnp.max(s, axis=0, keepdims=True))
            m_safe = jnp.where(m_new == NEG_INF, 0.0, m_new)
            p_scr[slot, e, ch] = jnp.exp2(s - m_safe).astype(p_scr.dtype)
            alpha_scr[slot, e, ch] = jnp.exp2(m_old - m_safe)
            m_scr[e, ch] = m_new


def _pv_stage(vts, p_scr, alpha_scr, acc_scr, slot, chunks=(0,)):
    for e in range(2):
        for ch in chunks:
            acc_scr[e, ch] = acc_scr[e, ch] * alpha_scr[slot, e, ch] + _dot(vts[e], p_scr[slot, e, ch])


def _normalized_pair(acc_scr, ch, lanes):
    a0, a1 = acc_scr[0, ch, :, lanes], acc_scr[1, ch, :, lanes]
    return jnp.concatenate([a0[:HEAD_DIM] / a0[HEAD_DIM:HEAD_DIM + 1],
                            a1[:HEAD_DIM] / a1[HEAD_DIM:HEAD_DIM + 1]], axis=0)


def _mla_kernel(q_ref, k_ref, vt_ref, o_ref, s_scr, p_scr, alpha_scr, m_scr, acc_scr, *, tq, tk):
    seq = q_ref.shape[1]
    n_diag = tq // tk
    sub = tk // BLK
    all_chunks = tuple(range(n_diag))
    below_diagonal = (lax.broadcasted_iota(jnp.int32, (tk, tk), 0)
                      <= lax.broadcasted_iota(jnp.int32, (tk, tk), 1))

    def q_tile(qt, _):
        q0 = pl.multiple_of(qt * tq, tq)
        qs = [q_ref[0, pl.ds(q0, tq), e * LANES:(e + 1) * LANES] for e in range(2)]
        n_full = qt * n_diag
        _init_pipeline(s_scr, p_scr, alpha_scr, m_scr, acc_scr)

        def vts(kt):
            kt = jnp.maximum(kt, 0)
            return [jnp.concatenate([vt_ref[0, kt * sub + c, e * VT_ROWS:(e + 1) * VT_ROWS, :]
                                     for c in range(sub)], axis=1) for e in range(2)]

        def qk_stage(kt, slot, diag):
            k0 = pl.multiple_of(kt * tk, tk)
            first = 0 if diag is None else diag
            for e in range(2):
                s = _nt_dot(k_ref[0, pl.ds(k0, tk), e * LANES:(e + 1) * LANES], qs[e][first * tk:])
                for ch in range(first, n_diag):
                    sc = s[:, (ch - first) * tk:(ch - first + 1) * tk]
                    if ch == diag:
                        sc = jnp.where(below_diagonal, sc, NEG_INF)
                    s_scr[slot, e, ch] = sc

        def chunks_of(c):
            return all_chunks[max(c, 0):]

        def pair_step(kk, carry):
            for par in range(2):
                kt = 2 * kk + par
                _pv_stage(vts(kt - 2), p_scr, alpha_scr, acc_scr, par, all_chunks)
                qk_stage(kt, par, None)
                _softmax_stage(s_scr, p_scr, alpha_scr, m_scr, 1 - par, all_chunks)
            return carry

        lax.fori_loop(0, n_full // 2, pair_step, 0)
        for c in range(n_diag + 2):
            par = c % 2
            _pv_stage(vts(n_full + c - 2), p_scr, alpha_scr, acc_scr, par, chunks_of(c - 2))
            if c < n_diag:
                qk_stage(n_full + c, par, c)
            if c < n_diag + 1:
                _softmax_stage(s_scr, p_scr, alpha_scr, m_scr, 1 - par, chunks_of(c - 1))
        for ch in all_chunks:
            o_ref[0, pl.ds(pl.multiple_of(q0 + ch * tk, tk), tk), :] = jnp.transpose(
                _normalized_pair(acc_scr, ch, slice(None))).astype(o_ref.dtype)
        return 0

    lax.fori_loop(0, seq // tq, q_tile, 0)


def _mla_call(q, k, vt, tq=1024, tk=256):
    b, l, _ = q.shape
    tq, tk = min(tq, l), min(tk, l)
    assert (tq // tk) % 2 == 0 and l % tq == 0
    nb = l // BLK
    vt = vt.reshape(b, nb, HEADS * VT_ROWS, LANES)
    return pl.pallas_call(
        functools.partial(_mla_kernel, tq=tq, tk=tk),
        name="mla_attn",
        out_shape=jax.ShapeDtypeStruct((b, l, HEADS * HEAD_DIM), MXU_DTYPE),
        grid=(b, PAIRS),
        in_specs=[
            pl.BlockSpec((1, l, 2 * LANES), lambda bi, p: (bi, 0, p)),
            pl.BlockSpec((1, l, 2 * LANES), lambda bi, p: (bi, 0, p)),
            pl.BlockSpec((1, nb, 2 * VT_ROWS, LANES), lambda bi, p: (bi, 0, p, 0)),
        ],
        out_specs=pl.BlockSpec((1, l, LANES), lambda bi, p: (bi, 0, p)),
        scratch_shapes=_pipeline_scratch(tk, tq // tk, tk),
        compiler_params=pltpu.CompilerParams(dimension_semantics=("parallel", "parallel"),
                                             vmem_limit_bytes=VMEM_LIMIT),
    )(q, k, vt)


def _dsa_kernel(q_ref, qi_ref, wi_ref, kc_ref, vt_ref, kic_ref, bias_ref, tri_ref, o_ref,
                idx_scr, hi_scr, lo_scr, s_scr, p_scr, alpha_scr, m_scr, acc_scr, *, k_sel):
    i = pl.program_id(1)
    n_tiles = i + 1
    n_groups = i // TILE_GROUP + 1
    key_row = lax.broadcasted_iota(jnp.int32, (BLK, BLK), 0)
    query_col = lax.broadcasted_iota(jnp.int32, (BLK, BLK), 1)

    def admissible(j):
        return (j < i) | ((j == i) & (key_row <= query_col))

    wt = jnp.transpose(wi_ref[0])
    qi = qi_ref[0, 0]

    def score_group(g, carry):
        for u in range(TILE_GROUP):
            j = g * TILE_GROUP + u
            d = jnp.maximum(_nt_dot(kic_ref[0, j], qi), 0.0)
            sc = jnp.zeros((BLK, BLK), F32)
            for hp in range(IDX_HEADS // 2):
                for e in range(2):
                    h = 2 * hp + e
                    sc = sc + wt[h:h + 1, :] * d[e * BLK:(e + 1) * BLK, hp * BLK:(hp + 1) * BLK]
            sc = jnp.where(admissible(j), sc, NEG_INF)
            idx_scr[j] = sc
            bits = lax.bitcast_convert_type(sc, jnp.int32)
            key = jnp.where(bits >= 0, bits, bits ^ jnp.int32(0x7FFFFFFF))
            hi_scr[j] = lax.shift_right_arithmetic(key, 16).astype(jnp.int16)
            lo_scr[j] = ((key & 0xFFFF) - HALF_RANGE).astype(jnp.int16)
        return carry

    lax.fori_loop(0, n_groups, score_group, 0)

    def count(pred):
        def body(g, c):
            for u in range(TILE_GROUP):
                c = c + jnp.where(pred(idx_scr[g * TILE_GROUP + u]), 1.0, 0.0)
            return c
        c = lax.fori_loop(0, n_groups, body, jnp.zeros((BLK, BLK), F32))
        return jnp.sum(c, axis=0, keepdims=True)

    def count16(ref, cand):
        cand16 = cand.astype(jnp.int16)

        def body(g, c):
            for u in range(TILE_GROUP):
                c = c + jnp.where(ref[g * TILE_GROUP + u] >= cand16, jnp.int16(1), jnp.int16(0))
            return c
        c = lax.fori_loop(0, n_groups, body, jnp.zeros((BLK, BLK), jnp.int16))
        return jnp.sum(c.astype(jnp.int32), axis=0, keepdims=True)

    def search16(ref, need):
        def bit_step(it, v):
            cand_v = v | lax.shift_left(jnp.int32(1), 15 - it)
            return jnp.where(count16(ref, cand_v - HALF_RANGE) >= need, cand_v, v)
        return lax.fori_loop(0, 16, bit_step, jnp.zeros((1, BLK), jnp.int32))

    def search():
        t_hi = search16(hi_scr, k_sel) - HALF_RANGE
        above = jnp.where(t_hi == HALF_RANGE - 1, 0, count16(hi_scr, jnp.minimum(t_hi + 1, HALF_RANGE - 1)))
        t_hi16 = t_hi.astype(jnp.int16)

        def keep_prefix(g, carry):
            for u in range(TILE_GROUP):
                j = g * TILE_GROUP + u
                lo_scr[j] = jnp.where(hi_scr[j] == t_hi16, lo_scr[j], jnp.int16(-HALF_RANGE))
            return carry

        lax.fori_loop(0, n_groups, keep_prefix, 0)
        t_lo = search16(lo_scr, k_sel - above)
        key = lax.shift_left(t_hi, 16) | t_lo
        bits = jnp.where(key >= 0, key, key ^ jnp.int32(0x7FFFFFFF))
        t = lax.bitcast_convert_type(bits, F32)
        return jnp.where(t != t, NEG_INF, t)

    thr = lax.cond(n_tiles * BLK > k_sel, search, lambda: jnp.full((1, BLK), NEG_INF, F32))

    n_gt = count(lambda s: s > thr)
    n_ge = count(lambda s: s >= thr)
    tied = (n_ge > k_sel) & (thr > NEG_INF)

    @pl.when(jnp.max(jnp.where(tied, 1.0, 0.0)) > 0.0)
    def _():
        need = k_sel - n_gt

        def tie_tile(j, seen):
            s = idx_scr[j]
            eq = s == thr
            eq_f = jnp.where(eq, 1.0, 0.0).astype(MXU_DTYPE)
            rank = seen + _dot(tri_ref[...], eq_f)
            idx_scr[j] = jnp.where(eq & tied & (rank > need), NEG_INF, s)
            return seen + jnp.sum(eq_f.astype(F32), axis=0, keepdims=True)

        lax.fori_loop(0, n_tiles, tie_tile, jnp.zeros((1, BLK), F32))

    q = q_ref[0, 0]
    _init_pipeline(s_scr, p_scr, alpha_scr, m_scr, acc_scr)

    def qk_stage(j, slot, near, valid=None):
        sel = idx_scr[j] >= thr
        if near is not None:
            sel = sel & admissible(j)
        if valid is not None:
            sel = sel & valid
        mask = jnp.where(sel, 0.0, NEG_INF)
        mask = jnp.concatenate([mask] * PAIRS, axis=1)
        s = _nt_dot(kc_ref[0, j], q)
        if near is not None:
            s = s + bias_ref[near]
        for e in range(2):
            s_scr[slot, e, 0] = s[e * BLK:(e + 1) * BLK] + mask

    def vts(j):
        vt = vt_ref[0, jnp.maximum(j, 0)]
        return [vt, vt]

    n_far = jnp.maximum(i - 1, 0)
    n_trips = (n_far + TILE_GROUP - 1) // TILE_GROUP

    def far_steps(kk, carry):
        for u in range(TILE_GROUP):
            j = TILE_GROUP * kk + u
            _pv_stage(vts(j - 2), p_scr, alpha_scr, acc_scr, u % 2)
            qk_stage(jnp.minimum(j, i), u % 2, None, j < n_far)
            _softmax_stage(s_scr, p_scr, alpha_scr, m_scr, 1 - u % 2)
        return carry

    lax.fori_loop(0, n_trips, far_steps, 0)
    j0 = TILE_GROUP * n_trips
    tail = [(n_far, 0, i >= 1), (i, 1, None)]
    done = [jnp.minimum(j0 - 2, i), jnp.minimum(j0 - 1, i)] + [t[0] for t in tail]
    for c in range(len(tail) + 2):
        par = c % 2
        _pv_stage(vts(done[c]), p_scr, alpha_scr, acc_scr, par)
        if c < len(tail):
            qk_stage(tail[c][0], par, tail[c][1], tail[c][2])
        if c < len(tail) + 1:
            _softmax_stage(s_scr, p_scr, alpha_scr, m_scr, 1 - par)

    for p in range(PAIRS):
        sl = slice(p * BLK, (p + 1) * BLK)
        o_ref[0, :, sl] = jnp.transpose(_normalized_pair(acc_scr, 0, sl)).astype(o_ref.dtype)


def _dsa_call(q, qi, wi, kc, vt, kic, bias_t, tri, batch):
    nb = q.shape[0] // batch
    l = nb * BLK
    k_sel = min(TOPK_MAX, l // 4)
    assert nb % TILE_GROUP == 0
    per_batch =lambda a: a.reshape(batch, nb, *a.shape[1:])
    q, qi, kc, vt, kic = map(per_batch, (q, qi, kc, vt, kic))
    blk = lambda a: pl.BlockSpec((1, 1) + a.shape[2:], lambda bi, n: (bi, n, 0, 0))
    seq = lambda a: pl.BlockSpec((1,) + a.shape[1:], lambda bi, n: (bi, 0, 0, 0))
    return pl.pallas_call(
        functools.partial(_dsa_kernel, k_sel=k_sel),
        name="dsa_attn",
        out_shape=jax.ShapeDtypeStruct((batch, l, HEADS * HEAD_DIM), MXU_DTYPE),
        grid=(batch, nb),
        in_specs=[
            blk(q),
            blk(qi),
            pl.BlockSpec((1, BLK, LANES), lambda bi, n: (bi, n, 0)),
            seq(kc),
            seq(vt),
            seq(kic),
            pl.BlockSpec(bias_t.shape, lambda bi, n: (0, 0, 0)),
            pl.BlockSpec(tri.shape, lambda bi, n: (0, 0)),
        ],
        out_specs=pl.BlockSpec((1, BLK, HEADS * HEAD_DIM), lambda bi, n: (bi, n, 0)),
        scratch_shapes=[pltpu.VMEM((nb, BLK, BLK), F32), pltpu.VMEM((nb, BLK, BLK), jnp.int16),
                        pltpu.VMEM((nb, BLK, BLK), jnp.int16)] + _pipeline_scratch(BLK, 1, PAIRS * BLK),
        compiler_params=pltpu.CompilerParams(dimension_semantics=("parallel", "arbitrary"),
                                             vmem_limit_bytes=VMEM_LIMIT),
    )(q, qi, wi.reshape(batch, l, LANES), kc, vt, kic, bias_t, tri)


def _t5_bucket(rel):
    n = jnp.maximum(rel, 0)
    max_exact = N_BUCKETS // 2
    nf = jnp.maximum(n, 1).astype(F32)
    large = max_exact + (jnp.log(nf / max_exact) / math.log(MAX_DISTANCE / max_exact)
                         * (N_BUCKETS - max_exact)).astype(jnp.int32)
    large = jnp.minimum(large, N_BUCKETS - 1)
    return jnp.where(n < max_exact, n, large)


def _band_bias(rel_bias):
    qi = jnp.arange(BLK, dtype=jnp.int32)[:, None] + BLK
    kj = jnp.arange(2 * BLK, dtype=jnp.int32)[None, :]
    return jnp.transpose(rel_bias[_t5_bucket(qi - kj)], (2, 0, 1))


def _pair_gain(g, scale=1.0):
    return (jnp.concatenate([g, g]) * scale).reshape(1, LANES).astype(F32)


def _dup(w, n_heads):
    d = w.shape[0]
    w = w.reshape(d, n_heads, 1, HEAD_DIM)
    return jnp.broadcast_to(w, (d, n_heads, 2, HEAD_DIM)).reshape(d, n_heads * LANES)


def _layer_a(x, g_mix, w_in, q_gain, k_gain):
    nq = HEADS * HEAD_DIM
    sizes = [nq, HEAD_DIM, HEAD_DIM, IDX_HEADS * IDX_DIM, IDX_DIM, IDX_HEADS]
    wq, wk, wv, wqi, wki, wwi = jnp.split(w_in, np.cumsum(sizes)[:-1].tolist(), axis=1)
    pad_to_lanes = lambda a: jnp.pad(a, ((0, 0), (0, LANES - a.shape[1])))
    w = jnp.concatenate([wq, _dup(wk, 1), pad_to_lanes(wv), wqi, _dup(wki, 1), pad_to_lanes(wwi)],
                        axis=1).astype(MXU_DTYPE)
    consts = [g_mix.reshape(1, -1), w, _pair_gain(q_gain, HEAD_DIM ** -0.5 * LOG2_E), _pair_gain(k_gain)]
    outs = [("stacked", PAIRS * BLK, MXU_DTYPE), ("stacked", 2 * BLK, MXU_DTYPE), ("stacked", VT_ROWS, MXU_DTYPE),
            ("stacked", IDX_HEADS // 2 * BLK, MXU_DTYPE), ("stacked", 2 * BLK, MXU_DTYPE), ("rows", LANES, F32)]
    return _proj_call(_proj_a_kernel, x, consts, [], outs)


def _layer_b(x, g_mix, w_in, q_gain, k_gain):
    nq = HEADS * HEAD_DIM
    nkv = B_KV_HEADS * HEAD_DIM
    wq, wk, wv = jnp.split(w_in, [nq, nq + nkv], axis=1)
    wv = jnp.pad(wv.reshape(-1, B_KV_HEADS, HEAD_DIM), ((0, 0), (0, 0), (0, LANES - HEAD_DIM)))
    w = jnp.concatenate([wq, _dup(wk, B_KV_HEADS), wv.reshape(-1, B_KV_HEADS * LANES)], axis=1).astype(MXU_DTYPE)
    consts = [g_mix.reshape(1, -1), w, _pair_gain(q_gain, HEAD_DIM ** -0.5), _pair_gain(k_gain)]
    outs = [("stacked", PAIRS * BLK, MXU_DTYPE), ("stacked", B_KV_HEADS * 2 * BLK, MXU_DTYPE),
            ("stacked", B_KV_HEADS * VT_ROWS, MXU_DTYPE)]
    return _proj_call(_proj_b_kernel, x, consts, [], outs)


def _pad_heads(w, width):
    d = w.shape[0]
    w = w.reshape(d, HEADS, width)
    return jnp.pad(w, ((0, 0), (0, 0), (0, LANES - width))).reshape(d, HEADS * LANES)


def _layer_c(x, g_mix, w_in, q_a_gain, w_q_b, kv_a_gain, w_kv_b, q_gain, k_gain, cos_t, sin_t):
    d = w_in.shape[0]
    w_rope = jnp.zeros((d, LANES), w_in.dtype).at[:, C_NOPE:C_QK].set(w_in[:, C_Q_RANK + C_KV_RANK:])
    w_in_p = jnp.concatenate([w_in[:, :C_Q_RANK + C_KV_RANK], w_rope], axis=1).astype(MXU_DTYPE)
    wq = _pad_heads(w_q_b, C_QK).astype(MXU_DTYPE)
    w_kv = w_kv_b.reshape(C_KV_RANK, HEADS, C_NOPE + HEAD_DIM)
    wk = _pad_heads(w_kv[:, :, :C_NOPE].reshape(C_KV_RANK, -1), C_NOPE).astype(MXU_DTYPE)
    wv = w_kv[:, :, C_NOPE:].reshape(C_KV_RANK, -1).astype(MXU_DTYPE)
    pad_gain = lambda g, s: (jnp.pad(g, (0, LANES - C_QK)) * s).reshape(1, LANES).astype(F32)
    consts = [g_mix.reshape(1, -1), w_in_p, q_a_gain.reshape(1, -1), kv_a_gain.reshape(1, -1), wq, wk, wv,
              pad_gain(q_gain, C_QK ** -0.5 * LOG2_E), pad_gain(k_gain, 1.0)]
    outs = [("rows", HEADS * LANES, MXU_DTYPE), ("rows", HEADS * LANES, MXU_DTYPE),
            ("stacked", HEADS * VT_ROWS, MXU_DTYPE)]
    return _proj_call(_proj_c_kernel, x, consts, [cos_t, sin_t], outs)


def kernel(x, positions, rel_bias, norm_mix, norm_mlp, w_up, w_down, a_w_in, a_q_gain, a_k_gain, a_w_out,
           b_w_in, b_q_gain, b_k_gain, b_sinks, b_w_out, c_w_in, c_q_a_gain, c_w_q_b, c_kv_a_gain, c_w_kv_b,
           c_q_gain, c_k_gain, c_w_out):
    b, l, d = x.shape
    t = b * l
    depth = norm_mix.shape[0]

    band = _band_bias(rel_bias.astype(F32))
    far = rel_bias[N_BUCKETS - 1].astype(F32)[:, None, None]
    bias_t = jnp.transpose(((band - far) * LOG2_E).reshape(PAIRS, 2, BLK, 2, BLK), (3, 1, 4, 0, 2)).reshape(
        2, 2 * BLK, PAIRS * BLK)
    tri = jnp.asarray(np.arange(BLK)[:, None] >= np.arange(BLK)[None, :], MXU_DTYPE)

    inv_freq = ROPE_THETA ** (-jnp.arange(0, C_ROPE, 2, dtype=F32) / C_ROPE)
    ang = positions.astype(F32).reshape(t, 1) * inv_freq
    cos, sin = jnp.cos(ang), jnp.sin(ang)
    cos_t = jnp.concatenate([jnp.ones((t, C_NOPE), F32), cos, cos, jnp.zeros((t, LANES - C_QK), F32)], axis=1)
    sin_t = jnp.concatenate([jnp.zeros((t, C_NOPE), F32), -sin, sin, jnp.zeros((t, LANES - C_QK), F32)], axis=1)

    xf = x.reshape(t, d).astype(F32)
    ia = ib = ic = 0
    for i in range(depth):
        kind = i % 3
        g_mix = norm_mix[i]
        if kind == 0:
            q, kc, vt, qi, kic, wi = _layer_a(xf, g_mix, a_w_in[ia], a_q_gain[ia], a_k_gain[ia])
            attn = _dsa_call(q, qi, wi, kc, vt, kic, bias_t, tri, b)
            w_out = a_w_out[ia]
            ia += 1
        elif kind == 1:
            q, kc, vt = _layer_b(xf, g_mix, b_w_in[ib], b_q_gain[ib], b_k_gain[ib])
            group_pairs = PAIRS // B_KV_HEADS
            swa_bias = jnp.transpose(band.reshape(B_KV_HEADS, group_pairs, 2, BLK, 2 * BLK),
                                     (0, 2, 4, 1, 3)).reshape(B_KV_HEADS, 2, 2 * BLK, group_pairs * BLK)
            sinks = jnp.transpose(b_sinks[ib].astype(F32).reshape(B_KV_HEADS, group_pairs, 2), (0, 2, 1))
            sinks = jnp.broadcast_to(sinks[..., None], (B_KV_HEADS, 2, group_pairs, BLK)).reshape(
                B_KV_HEADS, 2, 1, group_pairs * BLK)
            attn = _swa_call(q, kc, vt, swa_bias, sinks, b)
            w_out = b_w_out[ib]
            ib += 1
        else:
            q, k, vt = _layer_c(xf, g_mix, c_w_in[ic], c_q_a_gain[ic], c_w_q_b[ic], c_kv_a_gain[ic],
                               c_w_kv_b[ic], c_q_gain[ic], c_k_gain[ic], cos_t, sin_t)
            r3 = lambda a: a.reshape(b, l, a.shape[-1])
            attn = _mla_call(r3(q), r3(k), vt)
            w_out = c_w_out[ic]
            ic += 1
        xf = _out_mlp_call(xf, attn.reshape(t, -1), w_out.astype(MXU_DTYPE), norm_mlp[i].reshape(1, -1),
                           w_up[i].astype(MXU_DTYPE), w_down[i].astype(MXU_DTYPE))
    return xf.reshape(b, l, d).astype(x.dtype)
```

```python
import functools
import math

import numpy as np
import jax
import jax.numpy as jnp
from jax import lax
from jax.experimental import pallas as pl
from jax.experimental.pallas import tpu as pltpu

F32 = jnp.float32
MXU_DTYPE = jnp.bfloat16

D_MODEL = 1024
D_FF = 4 * D_MODEL
RMS_EPS = 1e-6
N_BUCKETS = 32
MAX_DISTANCE = 128
HEADS = 16
HEAD_DIM = 64
PAIRS = HEADS // 2
LANES = 128
BLK = 128
IDX_HEADS = 8
IDX_DIM = 64
TOPK_MAX = 256
B_KV_HEADS = 4
WINDOW = 128
C_Q_RANK = 256
C_KV_RANK = 128
C_NOPE = 64
C_ROPE = 32
C_QK = C_NOPE + C_ROPE
ROPE_THETA = 10000.0
NEG_INF = float("-inf")
LOG2_E = math.log2(math.e)
HALF_RANGE = 1 << 15
TILE_GROUP = 4
SEARCH_STEP = 2 * TILE_GROUP
VT_ROWS = HEAD_DIM + 16
VMEM_LIMIT = 56 * 1024 * 1024


def _nt_dot(a, b):
    return lax.dot_general(a, b, (((1,), (1,)), ((), ())), preferred_element_type=F32)


def _dot(a, b):
    return jnp.dot(a, b, preferred_element_type=F32)


def _rms(x, g):
    ms = jnp.mean(x * x, axis=-1, keepdims=True)
    return x * lax.rsqrt(ms + RMS_EPS) * g


def _lane_iota(shape):
    return lax.broadcasted_iota(jnp.int32, shape, len(shape) - 1)


def _pair_rms(y, gain):
    left = _lane_iota(y.shape) < HEAD_DIM
    y2 = y * y
    s_left = jnp.sum(jnp.where(left, y2, 0.0), axis=-1, keepdims=True)
    s_right = jnp.sum(y2, axis=-1, keepdims=True) - s_left
    r = jnp.where(left, lax.rsqrt(s_left / HEAD_DIM + RMS_EPS), lax.rsqrt(s_right / HEAD_DIM + RMS_EPS))
    return y * r * gain


def _split_kv(t, parity, ones_lane=None):
    lane = _lane_iota(t.shape)
    keep = (lane < HEAD_DIM) if parity == 0 else (lane >= HEAD_DIM)
    out = jnp.where(keep, t, jnp.zeros_like(t))
    if ones_lane is not None:
        out = jnp.where(lane == ones_lane, jnp.ones_like(t), out)
    return out


def _put_blocks(out_ref, slot, tile):
    for bk in range(tile.shape[0] // BLK):
        out_ref[bk, slot * BLK:(slot + 1) * BLK, :] = tile[bk * BLK:(bk + 1) * BLK].astype(out_ref.dtype)


def _proj_a_kernel(x_ref, g_ref, w_ref, gq_ref, gk_ref, q_out, kc_out, vt_out, qi_out, kic_out, wi_out):
    xn = _rms(x_ref[...], g_ref[...]).astype(MXU_DTYPE)
    for p in range(PAIRS):
        q = _dot(xn, w_ref[:, p * LANES:(p + 1) * LANES])
        _put_blocks(q_out, p, _pair_rms(q, gq_ref[...]))
    o = HEADS * HEAD_DIM
    k = _pair_rms(_dot(xn, w_ref[:, o:o + LANES]), gk_ref[...])
    _put_blocks(kc_out, 0, _split_kv(k, 0))
    _put_blocks(kc_out, 1, _split_kv(k, 1))
    o += LANES
    _put_vt(vt_out, 0, _dot(xn, w_ref[:, o:o + LANES]))
    o += LANES
    for hp in range(IDX_HEADS // 2):
        _put_blocks(qi_out, hp, _dot(xn, w_ref[:, o:o + LANES]) * IDX_DIM ** -0.5)
        o += LANES
    ki = _dot(xn, w_ref[:, o:o + LANES])
    _put_blocks(kic_out, 0, _split_kv(ki, 0))
    _put_blocks(kic_out, 1, _split_kv(ki, 1))
    o += LANES
    wi_out[...] = _dot(xn, w_ref[:, o:o + LANES]) * IDX_HEADS ** -0.5


def _put_vt(vt_out, slot, v):
    vt = jnp.transpose(v)
    row = lax.broadcasted_iota(jnp.int32, vt.shape, 0)
    vt = jnp.where(row == HEAD_DIM, 1.0, vt)
    for bk in range(vt.shape[1] // BLK):
        vt_out[bk, slot * VT_ROWS:(slot + 1) * VT_ROWS, :] = (
            vt[:VT_ROWS, bk * BLK:(bk + 1) * BLK].astype(vt_out.dtype))


def _proj_b_kernel(x_ref, g_ref, w_ref, gq_ref, gk_ref, q_out, kc_out, vt_out):
    xn = _rms(x_ref[...], g_ref[...]).astype(MXU_DTYPE)
    for p in range(PAIRS):
        q = _dot(xn, w_ref[:, p * LANES:(p + 1) * LANES])
        _put_blocks(q_out, p, _pair_rms(q, gq_ref[...]))
    o = HEADS * HEAD_DIM
    for c in range(B_KV_HEADS):
        k = _pair_rms(_dot(xn, w_ref[:, o + c * LANES:o + (c + 1) * LANES]), gk_ref[...])
        _put_blocks(kc_out, 2 * c, _split_kv(k, 0))
        _put_blocks(kc_out, 2 * c + 1, _split_kv(k, 1))
    o += B_KV_HEADS * LANES
    for c in range(B_KV_HEADS):
        _put_vt(vt_out, c, _dot(xn, w_ref[:, o + c * LANES:o + (c + 1) * LANES]))


def _rope(y, cos_t, sin_t):
    lane = _lane_iota(y.shape)
    first_half = (lane >= C_NOPE) & (lane < C_NOPE + C_ROPE // 2)
    partner = jnp.where(first_half, pltpu.roll(y, LANES - C_ROPE // 2, 1), pltpu.roll(y, C_ROPE // 2, 1))
    return y * cos_t + partner * sin_t


def _proj_c_kernel(x_ref, g_ref, w_in_ref, gqa_ref, gkva_ref, wq_ref, wk_ref, wv_ref, gq_ref, gk_ref,
                   cos_ref, sin_ref, q_out, k_out, vt_out):
    xn = _rms(x_ref[...], g_ref[...]).astype(MXU_DTYPE)
    lat = _dot(xn, w_in_ref[...])
    qa = _rms(lat[:, :C_Q_RANK], gqa_ref[...]).astype(MXU_DTYPE)
    kva = _rms(lat[:, C_Q_RANK:C_Q_RANK + C_KV_RANK], gkva_ref[...]).astype(MXU_DTYPE)
    k_rope = lat[:, C_Q_RANK + C_KV_RANK:]
    cos_t, sin_t = cos_ref[...], sin_ref[...]
    v = _dot(kva, wv_ref[...])
    tm = v.shape[0]
    ones_row = jnp.where(lax.broadcasted_iota(jnp.int32, (VT_ROWS - HEAD_DIM, tm), 0) == 0, 1.0, 0.0)
    for p in range(PAIRS):
        v_t = jnp.transpose(v[:, p * LANES:(p + 1) * LANES])
        for e in range(2):
            h = 2 * p + e
            head_t = jnp.concatenate([v_t[e * HEAD_DIM:(e + 1) * HEAD_DIM], ones_row], axis=0)
            for bk in range(tm // BLK):
                vt_out[bk, h * VT_ROWS:(h + 1) * VT_ROWS, :] = (
                    head_t[:, bk * BLK:(bk + 1) * BLK].astype(vt_out.dtype))
    for h in range(HEADS):
        sl = slice(h * LANES, (h + 1) * LANES)
        q = _dot(qa, wq_ref[:, sl])
        q = q * lax.rsqrt(jnp.sum(q * q, axis=-1, keepdims=True) / C_QK + RMS_EPS) * gq_ref[...]
        q_out[:, sl] = _rope(q, cos_t, sin_t).astype(q_out.dtype)
        k = _dot(kva, wk_ref[:, sl]) + k_rope
        k = k * lax.rsqrt(jnp.sum(k * k, axis=-1, keepdims=True) / C_QK + RMS_EPS) * gk_ref[...]
        k_out[:, sl] = _rope(k, cos_t, sin_t).astype(k_out.dtype)


def _row_spec(tm, n):
    return pl.BlockSpec((tm, n), lambda i: (i, 0))


def _const_spec(shape):
    return pl.BlockSpec(shape, lambda i: (0,) * len(shape))


def _proj_call(kernel, x, consts, row_inputs, outs, tm=512):
    t = x.shape[0]
    in_specs = [_row_spec(tm, x.shape[1])] + [_const_spec(c.shape) for c in consts]
    in_specs += [_row_spec(tm, r.shape[1]) for r in row_inputs]
    out_shape, out_specs = [], []
    for kind, n, dtype in outs:
        if kind == "rows":
            out_shape.append(jax.ShapeDtypeStruct((t, n), dtype))
            out_specs.append(_row_spec(tm, n))
        else:
            out_shape.append(jax.ShapeDtypeStruct((t // BLK, n, LANES), dtype))
            out_specs.append(pl.BlockSpec((tm // BLK, n, LANES), lambda i: (i, 0, 0)))
    return pl.pallas_call(
        kernel,
        name=kernel.__name__.strip("_"),
        out_shape=out_shape,
        grid=(t // tm,),
        in_specs=in_specs,
        out_specs=out_specs,
        compiler_params=pltpu.CompilerParams(dimension_semantics=("parallel",), vmem_limit_bytes=VMEM_LIMIT),
    )(x, *consts, *row_inputs)


def _out_mlp_kernel(x_ref, a_ref, wo_ref, g_ref, wu_ref, wd_ref, o_ref, *, f_chunk):
    x1 = x_ref[...] + _dot(a_ref[...], wo_ref[...])
    h = _rms(x1, g_ref[...]).astype(MXU_DTYPE)
    acc = x1
    for c in range(D_FF // f_chunk):
        u = jnp.maximum(_dot(h, wu_ref[:, c * f_chunk:(c + 1) * f_chunk]), 0.0)
        acc = acc + _dot((u * u).astype(MXU_DTYPE), wd_ref[c * f_chunk:(c + 1) * f_chunk, :])
    o_ref[...] = acc


def _out_mlp_call(x, attn, w_out, g, w_up, w_down, tm=512, f_chunk=1024):
    t = x.shape[0]
    single = pl.Buffered(1)
    return pl.pallas_call(
        functools.partial(_out_mlp_kernel, f_chunk=f_chunk),
        name="out_mlp",
        out_shape=jax.ShapeDtypeStruct((t, D_MODEL), F32),
        grid=(t // tm,),
        in_specs=[
            _row_spec(tm, D_MODEL),
            _row_spec(tm, attn.shape[1]),
            pl.BlockSpec(w_out.shape, lambda i: (0, 0), pipeline_mode=single),
            _const_spec(g.shape),
            pl.BlockSpec(w_up.shape, lambda i: (0, 0), pipeline_mode=single),
            pl.BlockSpec(w_down.shape, lambda i: (0, 0), pipeline_mode=single),
        ],
        out_specs=_row_spec(tm, D_MODEL),
        compiler_params=pltpu.CompilerParams(dimension_semantics=("parallel",), vmem_limit_bytes=VMEM_LIMIT),
    )(x, attn, w_out, g, w_up, w_down)


def _swa_kernel(q_ref, kp_ref, kc_ref, vp_ref, vc_ref, bias_ref, sink_ref, o_ref):
    n = pl.program_id(1)
    group_pairs = PAIRS // B_KV_HEADS
    w = group_pairs * BLK
    key = lax.broadcasted_iota(jnp.int32, (2 * BLK, w), 0)
    query = lax.broadcasted_iota(jnp.int32, (2 * BLK, w), 1) & (BLK - 1)
    rel = query + WINDOW - key
    ok = (rel >= 0) & (rel < WINDOW) & ((n > 0) | (key >= WINDOW))
    for c in range(B_KV_HEADS):
        kk = jnp.concatenate([kp_ref[0, 0, c * 2 * BLK:(c + 1) * 2 * BLK, :],
                              kc_ref[0, 0, c * 2 * BLK:(c + 1) * 2 * BLK, :]], axis=0)
        s = _nt_dot(kk, q_ref[0, 0, c * w:(c + 1) * w, :])
        vt = jnp.concatenate([vp_ref[0, 0, c * VT_ROWS:(c + 1) * VT_ROWS, :],
                              vc_ref[0, 0, c * VT_ROWS:(c + 1) * VT_ROWS, :]], axis=1)
        halves = []
        for e in range(2):
            se = jnp.concatenate([s[e * BLK:(e + 1) * BLK], s[(2 + e) * BLK:(3 + e) * BLK]], axis=0)
            se = jnp.where(ok, se + bias_ref[c, e], NEG_INF)
            sink = sink_ref[c, e]
            m = jnp.maximum(jnp.max(se, axis=0, keepdims=True), sink)
            acc = _dot(vt, jnp.exp(se - m).astype(MXU_DTYPE))
            halves.append(acc[:HEAD_DIM] / (acc[HEAD_DIM:HEAD_DIM + 1] + jnp.exp(sink - m)))
        for pg in range(group_pairs):
            sl = slice(pg * BLK, (pg + 1) * BLK)
            pair = jnp.concatenate([halves[0][:, sl], halves[1][:, sl]], axis=0)
            p = c * group_pairs + pg
            o_ref[0, :, p * LANES:(p + 1) * LANES] = jnp.transpose(pair).astype(o_ref.dtype)


def _swa_call(q, kc, vt, bias_t, sinks_t, batch):
    nb = q.shape[0] // batch
    per_batch = lambda a: a.reshape(batch, nb, *a.shape[1:])
    q, kc, vt = map(per_batch, (q, kc, vt))
    prev = lambda a: pl.BlockSpec((1, 1) + a.shape[2:], lambda bi, n: (bi, jnp.maximum(n - 1, 0), 0, 0))
    cur = lambda a: pl.BlockSpec((1, 1) + a.shape[2:], lambda bi, n: (bi, n, 0, 0))
    return pl.pallas_call(
        _swa_kernel,
        name="swa_attn",
        out_shape=jax.ShapeDtypeStruct((batch, nb * BLK, HEADS * HEAD_DIM), MXU_DTYPE),
        grid=(batch, nb),
        in_specs=[cur(q), prev(kc), cur(kc), prev(vt), cur(vt),
                  pl.BlockSpec(bias_t.shape, lambda bi, n: (0, 0, 0, 0)),
                  pl.BlockSpec(sinks_t.shape, lambda bi, n: (0, 0, 0, 0))],
        out_specs=pl.BlockSpec((1, BLK, HEADS * HEAD_DIM), lambda bi, n: (bi, n, 0)),
        compiler_params=pltpu.CompilerParams(dimension_semantics=("parallel", "parallel"),
                                             vmem_limit_bytes=VMEM_LIMIT),
    )(q, kc, kc, vt, vt, bias_t, sinks_t)


def _pipeline_scratch(tk, n_chunks, lanes):
    return [pltpu.VMEM((2, 2, n_chunks, tk, lanes), F32), pltpu.VMEM((2, 2, n_chunks, tk, lanes), MXU_DTYPE),
            pltpu.VMEM((2, 2, n_chunks, 1, lanes), F32), pltpu.VMEM((2, n_chunks, 1, lanes), F32),
            pltpu.VMEM((2, n_chunks, VT_ROWS, lanes), F32)]


def _init_pipeline(s_scr, p_scr, alpha_scr, m_scr, acc_scr):
    m_scr[...] = jnp.full(m_scr.shape, NEG_INF, F32)
    acc_scr[...] = jnp.zeros(acc_scr.shape, F32)
    s_scr[1] = jnp.full(s_scr.shape[1:], NEG_INF, F32)
    p_scr[0] = jnp.zeros(p_scr.shape[1:], p_scr.dtype)
    alpha_scr[0] = jnp.ones(alpha_scr.shape[1:], F32)


def _softmax_stage(s_scr, p_scr, alpha_scr, m_scr, slot, chunks=(0,)):
    for e in range(2):
        for ch in chunks:
            s = s_scr[slot, e, ch]
            m_old = m_scr[e, ch]
            m_new = jnp.maximum(m_old, jnp.max(s, axis=0, keepdims=True))
            m_safe = jnp.where(m_new == NEG_INF, 0.0, m_new)
            p_scr[slot, e, ch] = jnp.exp2(s - m_safe).astype(p_scr.dtype)
            alpha_scr[slot, e, ch] = jnp.exp2(m_old - m_safe)
            m_scr[e, ch] = m_new


def _pv_stage(vts, p_scr, alpha_scr, acc_scr, slot, chunks=(0,)):
    for e in range(2):
        for ch in chunks:
            acc_scr[e, ch] = acc_scr[e, ch] * alpha_scr[slot, e, ch] + _dot(vts[e], p_scr[slot, e, ch])


def _normalized_pair(acc_scr, ch, lanes):
    a0, a1 = acc_scr[0, ch, :, lanes], acc_scr[1, ch, :, lanes]
    return jnp.concatenate([a0[:HEAD_DIM] / a0[HEAD_DIM:HEAD_DIM + 1],
                            a1[:HEAD_DIM] / a1[HEAD_DIM:HEAD_DIM + 1]], axis=0)


def _mla_kernel(q_ref, k_ref, vt_ref, o_ref, s_scr, p_scr, alpha_scr, m_scr, acc_scr, *, tq, tk):
    seq = q_ref.shape[1]
    n_diag = tq // tk
    sub = tk // BLK
    all_chunks = tuple(range(n_diag))
    below_diagonal = (lax.broadcasted_iota(jnp.int32, (tk, tk), 0)
                      <= lax.broadcasted_iota(jnp.int32, (tk, tk), 1))

    def q_tile(qt, _):
        q0 = pl.multiple_of(qt * tq, tq)
        qs = [q_ref[0, pl.ds(q0, tq), e * LANES:(e + 1) * LANES] for e in range(2)]
        n_full = qt * n_diag
        _init_pipeline(s_scr, p_scr, alpha_scr, m_scr, acc_scr)

        def vts(kt):
            kt = jnp.maximum(kt, 0)
            return [jnp.concatenate([vt_ref[0, kt * sub + c, e * VT_ROWS:(e + 1) * VT_ROWS, :]
                                     for c in range(sub)], axis=1) for e in range(2)]

        def qk_stage(kt, slot, diag):
            k0 = pl.multiple_of(kt * tk, tk)
            first = 0 if diag is None else diag
            for e in range(2):
                s = _nt_dot(k_ref[0, pl.ds(k0, tk), e * LANES:(e + 1) * LANES], qs[e][first * tk:])
                for ch in range(first, n_diag):
                    sc = s[:, (ch - first) * tk:(ch - first + 1) * tk]
                    if ch == diag:
                        sc = jnp.where(below_diagonal, sc, NEG_INF)
                    s_scr[slot, e, ch] = sc

        def chunks_of(c):
            return all_chunks[max(c, 0):]

        def pair_step(kk, carry):
            for par in range(2):
                kt = 2 * kk + par
                _pv_stage(vts(kt - 2), p_scr, alpha_scr, acc_scr, par, all_chunks)
                qk_stage(kt, par, None)
                _softmax_stage(s_scr, p_scr, alpha_scr, m_scr, 1 - par, all_chunks)
            return carry

        lax.fori_loop(0, n_full // 2, pair_step, 0)
        for c in range(n_diag + 2):
            par = c % 2
            _pv_stage(vts(n_full + c - 2), p_scr, alpha_scr, acc_scr, par, chunks_of(c - 2))
            if c < n_diag:
                qk_stage(n_full + c, par, c)
            if c < n_diag + 1:
                _softmax_stage(s_scr, p_scr, alpha_scr, m_scr, 1 - par, chunks_of(c - 1))
        for ch in all_chunks:
            o_ref[0, pl.ds(pl.multiple_of(q0 + ch * tk, tk), tk), :] = jnp.transpose(
                _normalized_pair(acc_scr, ch, slice(None))).astype(o_ref.dtype)
        return 0

    lax.fori_loop(0, seq // tq, q_tile, 0)


def _mla_call(q, k, vt, tq=1024, tk=256):
    b, l, _ = q.shape
    tq, tk = min(tq, l), min(tk, l)
    assert (tq // tk) % 2 == 0 and l % tq == 0
    nb = l // BLK
    vt = vt.reshape(b, nb, HEADS * VT_ROWS, LANES)
    return pl.pallas_call(
        functools.partial(_mla_kernel, tq=tq, tk=tk),
        name="mla_attn",
        out_shape=jax.ShapeDtypeStruct((b, l, HEADS * HEAD_DIM), MXU_DTYPE),
        grid=(b, PAIRS),
        in_specs=[
            pl.BlockSpec((1, l, 2 * LANES), lambda bi, p: (bi, 0, p)),
            pl.BlockSpec((1, l, 2 * LANES), lambda bi, p: (bi, 0, p)),
            pl.BlockSpec((1, nb, 2 * VT_ROWS, LANES), lambda bi, p: (bi, 0, p, 0)),
        ],
        out_specs=pl.BlockSpec((1, l, LANES), lambda bi, p: (bi, 0, p)),
        scratch_shapes=_pipeline_scratch(tk, tq // tk, tk),
        compiler_params=pltpu.CompilerParams(dimension_semantics=("parallel", "parallel"),
                                             vmem_limit_bytes=VMEM_LIMIT),
    )(q, k, vt)


def _dsa_kernel(q_ref, qi_ref, wi_ref, kc_ref, vt_ref, kic_ref, bias_ref, tri_ref, o_ref,
                idx_scr, hi_scr, lo_scr, s_scr, p_scr, alpha_scr, m_scr, acc_scr, *, k_sel):
    i = pl.program_id(1)
    n_tiles = i + 1
    n_groups = i // TILE_GROUP + 1
    key_row = lax.broadcasted_iota(jnp.int32, (BLK, BLK), 0)
    query_col = lax.broadcasted_iota(jnp.int32, (BLK, BLK), 1)

    def admissible(j):
        return (j < i) | ((j == i) & (key_row <= query_col))

    wt = jnp.transpose(wi_ref[0])
    qi = qi_ref[0, 0]

    def score_group(g, carry):
        for u in range(TILE_GROUP):
            j = g * TILE_GROUP + u
            d = jnp.maximum(_nt_dot(kic_ref[0, j], qi), 0.0)
            sc = jnp.zeros((BLK, BLK), F32)
            for hp in range(IDX_HEADS // 2):
                for e in range(2):
                    h = 2 * hp + e
                    sc = sc + wt[h:h + 1, :] * d[e * BLK:(e + 1) * BLK, hp * BLK:(hp + 1) * BLK]
            sc = jnp.where(admissible(j), sc, NEG_INF)
            idx_scr[j] = sc
            bits = lax.bitcast_convert_type(sc, jnp.int32)
            key = jnp.where(bits >= 0, bits, bits ^ jnp.int32(0x7FFFFFFF))
            hi_scr[j] = lax.shift_right_arithmetic(key, 16).astype(jnp.int16)
            lo_scr[j] = ((key & 0xFFFF) - HALF_RANGE).astype(jnp.int16)
        return carry

    lax.fori_loop(0, n_groups, score_group, 0)

    def count(pred):
        def body(g, c):
            for u in range(TILE_GROUP):
                c = c + jnp.where(pred(idx_scr[g * TILE_GROUP + u]), 1.0, 0.0)
            return c
        c = lax.fori_loop(0, n_groups, body, jnp.zeros((BLK, BLK), F32))
        return jnp.sum(c, axis=0, keepdims=True)

    n_blocks = idx_scr.shape[0]
    step = min(SEARCH_STEP, n_blocks)
    sizes = tuple(range(step, n_blocks + 1, step))
    lowest = jnp.full((BLK, BLK), -HALF_RANGE, jnp.int16)
    for u in range(step - TILE_GROUP):
        hi_scr[n_groups * TILE_GROUP + u] = lowest
        lo_scr[n_groups * TILE_GROUP + u] = lowest

    def count16(ref, cand, n_static):
        cand16 = cand.astype(jnp.int16)
        c = jnp.zeros((BLK, BLK), jnp.int16)
        for j in range(n_static):
            c = c + jnp.where(ref[j] >= cand16, jnp.int16(1), jnp.int16(0))
        return jnp.sum(c.astype(jnp.int32), axis=0, keepdims=True)

    def search16(ref, need, n_static):
        def bit_step(it, v):
            cand_v = v | lax.shift_left(jnp.int32(1), 15 - it)
            return jnp.where(count16(ref, cand_v - HALF_RANGE, n_static) >= need, cand_v, v)
        return lax.fori_loop(0, 16, bit_step, jnp.zeros((1, BLK), jnp.int32))

    def search(n_static):
        t_hi = search16(hi_scr, k_sel, n_static) - HALF_RANGE
        above = jnp.where(t_hi == HALF_RANGE - 1, 0,
                          count16(hi_scr, jnp.minimum(t_hi + 1, HALF_RANGE - 1), n_static))
        t_hi16 = t_hi.astype(jnp.int16)
        for j in range(n_static):
            lo_scr[j] = jnp.where(hi_scr[j] == t_hi16, lo_scr[j], jnp.int16(-HALF_RANGE))
        t_lo = search16(lo_scr, k_sel - above, n_static)
        key = lax.shift_left(t_hi, 16) | t_lo
        bits = jnp.where(key >= 0, key, key ^ jnp.int32(0x7FFFFFFF))
        t = lax.bitcast_convert_type(bits, F32)
        return jnp.where(t != t, NEG_INF, t)

    def run_search():
        which = (n_groups * TILE_GROUP + step - 1) // step - 1
        return lax.switch(which, [functools.partial(search, n) for n in sizes])

    thr = lax.cond(n_tiles * BLK > k_sel, run_search, lambda: jnp.full((1, BLK), NEG_INF, F32))

    n_gt = count(lambda s: s > thr)
    n_ge = count(lambda s: s >= thr)
    tied = (n_ge > k_sel) & (thr > NEG_INF)

    @pl.when(jnp.max(jnp.where(tied, 1.0, 0.0)) > 0.0)
    def _():
        need = k_sel - n_gt

        def tie_tile(j, seen):
            s = idx_scr[j]
            eq = s == thr
            eq_f = jnp.where(eq, 1.0, 0.0).astype(MXU_DTYPE)
            rank = seen + _dot(tri_ref[...], eq_f)
            idx_scr[j] = jnp.where(eq & tied & (rank > need), NEG_INF, s)
            return seen + jnp.sum(eq_f.astype(F32), axis=0, keepdims=True)

        lax.fori_loop(0, n_tiles, tie_tile, jnp.zeros((1, BLK), F32))

    q = q_ref[0, 0]
    _init_pipeline(s_scr, p_scr, alpha_scr, m_scr, acc_scr)

    def qk_stage(j, slot, near, valid=None):
        sel = idx_scr[j] >= thr
        if near is not None:
            sel = sel & admissible(j)
        if valid is not None:
            sel = sel & valid
        mask = jnp.where(sel, 0.0, NEG_INF)
        mask = jnp.concatenate([mask] * PAIRS, axis=1)
        s = _nt_dot(kc_ref[0, j], q)
        if near is not None:
            s = s + bias_ref[near]
        for e in range(2):
            s_scr[slot, e, 0] = s[e * BLK:(e + 1) * BLK] + mask

    def vts(j):
        vt = vt_ref[0, jnp.maximum(j, 0)]
        return [vt, vt]

    n_far = jnp.maximum(i - 1, 0)
    n_trips = (n_far + TILE_GROUP - 1) // TILE_GROUP

    def far_steps(kk, carry):
        for u in range(TILE_GROUP):
            j = TILE_GROUP * kk + u
            _pv_stage(vts(j - 2), p_scr, alpha_scr, acc_scr, u % 2)
            qk_stage(jnp.minimum(j, i), u % 2, None, j < n_far)
            _softmax_stage(s_scr, p_scr, alpha_scr, m_scr, 1 - u % 2)
        return carry

    lax.fori_loop(0, n_trips, far_steps, 0)
    j0 = TILE_GROUP * n_trips
    tail = [(n_far, 0, i >= 1), (i, 1, None)]
    done = [jnp.minimum(j0 - 2, i), jnp.minimum(j0 - 1, i)] + [t[0] for t in tail]
    for c in range(len(tail) + 2):
        par = c % 2
        _pv_stage(vts(done[c]), p_scr, alpha_scr, acc_scr, par)
        if c < len(tail):
            qk_stage(tail[c][0], par, tail[c][1], tail[c][2])
        if c < len(tail) + 1:
            _softmax_stage(s_scr, p_scr, alpha_scr, m_scr, 1 - par)

    for p in range(PAIRS):
        sl = slice(p * BLK, (p + 1) * BLK)
        o_ref[0, :, sl] = jnp.transpose(_normalized_pair(acc_scr, 0, sl)).astype(o_ref.dtype)


def _dsa_call(q, qi, wi, kc, vt, kic, bias_t, tri, batch):
    nb = q.shape[0] // batch
    l = nb * BLK
    k_sel = min(TOPK_MAX, l // 4)
    assert nb % TILE_GROUP == 0
    per_batch =lambda a: a.reshape(batch, nb, *a.shape[1:])
    q, qi, kc, vt, kic = map(per_batch, (q, qi, kc, vt, kic))
    blk = lambda a: pl.BlockSpec((1, 1) + a.shape[2:], lambda bi, n: (bi, n, 0, 0))
    seq = lambda a: pl.BlockSpec((1,) + a.shape[1:], lambda bi, n: (bi, 0, 0, 0))
    return pl.pallas_call(
        functools.partial(_dsa_kernel, k_sel=k_sel),
        name="dsa_attn",
        out_shape=jax.ShapeDtypeStruct((batch, l, HEADS * HEAD_DIM), MXU_DTYPE),
        grid=(batch, nb),
        in_specs=[
            blk(q),
            blk(qi),
            pl.BlockSpec((1, BLK, LANES), lambda bi, n: (bi, n, 0)),
            seq(kc),
            seq(vt),
            seq(kic),
            pl.BlockSpec(bias_t.shape, lambda bi, n: (0, 0, 0)),
            pl.BlockSpec(tri.shape, lambda bi, n: (0, 0)),
        ],
        out_specs=pl.BlockSpec((1, BLK, HEADS * HEAD_DIM), lambda bi, n: (bi, n, 0)),
        scratch_shapes=[pltpu.VMEM((nb, BLK, BLK), F32), pltpu.VMEM((nb + SEARCH_STEP, BLK, BLK), jnp.int16),
                        pltpu.VMEM((nb + SEARCH_STEP, BLK, BLK), jnp.int16)]
        + _pipeline_scratch(BLK, 1, PAIRS * BLK),
        compiler_params=pltpu.CompilerParams(dimension_semantics=("parallel", "arbitrary"),
                                             vmem_limit_bytes=VMEM_LIMIT),
    )(q, qi, wi.reshape(batch, l, LANES), kc, vt, kic, bias_t, tri)


def _t5_bucket(rel):
    n = jnp.maximum(rel, 0)
    max_exact = N_BUCKETS // 2
    nf = jnp.maximum(n, 1).astype(F32)
    large = max_exact + (jnp.log(nf / max_exact) / math.log(MAX_DISTANCE / max_exact)
                         * (N_BUCKETS - max_exact)).astype(jnp.int32)
    large = jnp.minimum(large, N_BUCKETS - 1)
    return jnp.where(n < max_exact, n, large)


def _band_bias(rel_bias):
    qi = jnp.arange(BLK, dtype=jnp.int32)[:, None] + BLK
    kj = jnp.arange(2 * BLK, dtype=jnp.int32)[None, :]
    return jnp.transpose(rel_bias[_t5_bucket(qi - kj)], (2, 0, 1))


def _pair_gain(g, scale=1.0):
    return (jnp.concatenate([g, g]) * scale).reshape(1, LANES).astype(F32)


def _dup(w, n_heads):
    d = w.shape[0]
    w = w.reshape(d, n_heads, 1, HEAD_DIM)
    return jnp.broadcast_to(w, (d, n_heads, 2, HEAD_DIM)).reshape(d, n_heads * LANES)


def _layer_a(x, g_mix, w_in, q_gain, k_gain):
    nq = HEADS * HEAD_DIM
    sizes = [nq, HEAD_DIM, HEAD_DIM, IDX_HEADS * IDX_DIM, IDX_DIM, IDX_HEADS]
    wq, wk, wv, wqi, wki, wwi = jnp.split(w_in, np.cumsum(sizes)[:-1].tolist(), axis=1)
    pad_to_lanes = lambda a: jnp.pad(a, ((0, 0), (0, LANES - a.shape[1])))
    w = jnp.concatenate([wq, _dup(wk, 1), pad_to_lanes(wv), wqi, _dup(wki, 1), pad_to_lanes(wwi)],
                        axis=1).astype(MXU_DTYPE)
    consts = [g_mix.reshape(1, -1), w, _pair_gain(q_gain, HEAD_DIM ** -0.5 * LOG2_E), _pair_gain(k_gain)]
    outs = [("stacked", PAIRS * BLK, MXU_DTYPE), ("stacked", 2 * BLK, MXU_DTYPE), ("stacked", VT_ROWS, MXU_DTYPE),
            ("stacked", IDX_HEADS // 2 * BLK, MXU_DTYPE), ("stacked", 2 * BLK, MXU_DTYPE), ("rows", LANES, F32)]
    return _proj_call(_proj_a_kernel, x, consts, [], outs)


def _layer_b(x, g_mix, w_in, q_gain, k_gain):
    nq = HEADS * HEAD_DIM
    nkv = B_KV_HEADS * HEAD_DIM
    wq, wk, wv = jnp.split(w_in, [nq, nq + nkv], axis=1)
    wv = jnp.pad(wv.reshape(-1, B_KV_HEADS, HEAD_DIM), ((0, 0), (0, 0), (0, LANES - HEAD_DIM)))
    w = jnp.concatenate([wq, _dup(wk, B_KV_HEADS), wv.reshape(-1, B_KV_HEADS * LANES)], axis=1).astype(MXU_DTYPE)
    consts = [g_mix.reshape(1, -1), w, _pair_gain(q_gain, HEAD_DIM ** -0.5), _pair_gain(k_gain)]
    outs = [("stacked", PAIRS * BLK, MXU_DTYPE), ("stacked", B_KV_HEADS * 2 * BLK, MXU_DTYPE),
            ("stacked", B_KV_HEADS * VT_ROWS, MXU_DTYPE)]
    return _proj_call(_proj_b_kernel, x, consts, [], outs)


def _pad_heads(w, width):
    d = w.shape[0]
    w = w.reshape(d, HEADS, width)
    return jnp.pad(w, ((0, 0), (0, 0), (0, LANES - width))).reshape(d, HEADS * LANES)


def _layer_c(x, g_mix, w_in, q_a_gain, w_q_b, kv_a_gain, w_kv_b, q_gain, k_gain, cos_t, sin_t):
    d = w_in.shape[0]
    w_rope = jnp.zeros((d, LANES), w_in.dtype).at[:, C_NOPE:C_QK].set(w_in[:, C_Q_RANK + C_KV_RANK:])
    w_in_p = jnp.concatenate([w_in[:, :C_Q_RANK + C_KV_RANK], w_rope], axis=1).astype(MXU_DTYPE)
    wq = _pad_heads(w_q_b, C_QK).astype(MXU_DTYPE)
    w_kv = w_kv_b.reshape(C_KV_RANK, HEADS, C_NOPE + HEAD_DIM)
    wk = _pad_heads(w_kv[:, :, :C_NOPE].reshape(C_KV_RANK, -1), C_NOPE).astype(MXU_DTYPE)
    wv = w_kv[:, :, C_NOPE:].reshape(C_KV_RANK, -1).astype(MXU_DTYPE)
    pad_gain = lambda g, s: (jnp.pad(g, (0, LANES - C_QK)) * s).reshape(1, LANES).astype(F32)
    consts = [g_mix.reshape(1, -1), w_in_p, q_a_gain.reshape(1, -1), kv_a_gain.reshape(1, -1), wq, wk, wv,
              pad_gain(q_gain, C_QK ** -0.5 * LOG2_E), pad_gain(k_gain, 1.0)]
    outs = [("rows", HEADS * LANES, MXU_DTYPE), ("rows", HEADS * LANES, MXU_DTYPE),
            ("stacked", HEADS * VT_ROWS, MXU_DTYPE)]
    return _proj_call(_proj_c_kernel, x, consts, [cos_t, sin_t], outs)


def kernel(x, positions, rel_bias, norm_mix, norm_mlp, w_up, w_down, a_w_in, a_q_gain, a_k_gain, a_w_out,
           b_w_in, b_q_gain, b_k_gain, b_sinks, b_w_out, c_w_in, c_q_a_gain, c_w_q_b, c_kv_a_gain, c_w_kv_b,
           c_q_gain, c_k_gain, c_w_out):
    b, l, d = x.shape
    t = b * l
    depth = norm_mix.shape[0]

    band = _band_bias(rel_bias.astype(F32))
    far = rel_bias[N_BUCKETS - 1].astype(F32)[:, None, None]
    bias_t = jnp.transpose(((band - far) * LOG2_E).reshape(PAIRS, 2, BLK, 2, BLK), (3, 1, 4, 0, 2)).reshape(
        2, 2 * BLK, PAIRS * BLK)
    tri = jnp.asarray(np.arange(BLK)[:, None] >= np.arange(BLK)[None, :], MXU_DTYPE)

    inv_freq = ROPE_THETA ** (-jnp.arange(0, C_ROPE, 2, dtype=F32) / C_ROPE)
    ang = positions.astype(F32).reshape(t, 1) * inv_freq
    cos, sin = jnp.cos(ang), jnp.sin(ang)
    cos_t = jnp.concatenate([jnp.ones((t, C_NOPE), F32), cos, cos, jnp.zeros((t, LANES - C_QK), F32)], axis=1)
    sin_t = jnp.concatenate([jnp.zeros((t, C_NOPE), F32), -sin, sin, jnp.zeros((t, LANES - C_QK), F32)], axis=1)

    xf = x.reshape(t, d).astype(F32)
    ia = ib = ic = 0
    for i in range(depth):
        kind = i % 3
        g_mix = norm_mix[i]
        if kind == 0:
            q, kc, vt, qi, kic, wi = _layer_a(xf, g_mix, a_w_in[ia], a_q_gain[ia], a_k_gain[ia])
            attn = _dsa_call(q, qi, wi, kc, vt, kic, bias_t, tri, b)
            w_out = a_w_out[ia]
            ia += 1
        elif kind == 1:
            q, kc, vt = _layer_b(xf, g_mix, b_w_in[ib], b_q_gain[ib], b_k_gain[ib])
            group_pairs = PAIRS // B_KV_HEADS
            swa_bias = jnp.transpose(band.reshape(B_KV_HEADS, group_pairs, 2, BLK, 2 * BLK),
                                     (0, 2, 4, 1, 3)).reshape(B_KV_HEADS, 2, 2 * BLK, group_pairs * BLK)
            sinks = jnp.transpose(b_sinks[ib].astype(F32).reshape(B_KV_HEADS, group_pairs, 2), (0, 2, 1))
            sinks = jnp.broadcast_to(sinks[..., None], (B_KV_HEADS, 2, group_pairs, BLK)).reshape(
                B_KV_HEADS, 2, 1, group_pairs * BLK)
            attn = _swa_call(q, kc, vt, swa_bias, sinks, b)
            w_out = b_w_out[ib]
            ib += 1
        else:
            q, k, vt = _layer_c(xf, g_mix, c_w_in[ic], c_q_a_gain[ic], c_w_q_b[ic], c_kv_a_gain[ic],
                               c_w_kv_b[ic], c_q_gain[ic], c_k_gain[ic], cos_t, sin_t)
            r3 = lambda a: a.reshape(b, l, a.shape[-1])
            attn = _mla_call(r3(q), r3(k), vt)
            w_out = c_w_out[ic]
            ic += 1
        xf = _out_mlp_call(xf, attn.reshape(t, -1), w_out.astype(MXU_DTYPE), norm_mlp[i].reshape(1, -1),
                           w_up[i].astype(MXU_DTYPE), w_down[i].astype(MXU_DTYPE))
    return xf.reshape(b, l, d).astype(x.dtype)
```

```python
import functools
import math

import numpy as np
import jax
import jax.numpy as jnp
from jax import lax
from jax.experimental import pallas as pl
from jax.experimental.pallas import tpu as pltpu

F32 = jnp.float32
MXU_DTYPE = jnp.bfloat16

D_MODEL = 1024
D_FF = 4 * D_MODEL
RMS_EPS = 1e-6
N_BUCKETS = 32
MAX_DISTANCE = 128
HEADS = 16
HEAD_DIM = 64
PAIRS = HEADS // 2
LANES = 128
BLK = 128
IDX_HEADS = 8
IDX_DIM = 64
TOPK_MAX = 256
B_KV_HEADS = 4
WINDOW = 128
C_Q_RANK = 256
C_KV_RANK = 128
C_NOPE = 64
C_ROPE = 32
C_QK = C_NOPE + C_ROPE
ROPE_THETA = 10000.0
NEG_INF = float("-inf")
LOG2_E = math.log2(math.e)
HALF_RANGE = 1 << 15
TILE_GROUP = 4
SEARCH_STEP = 2 * TILE_GROUP
VT_ROWS = HEAD_DIM + 16
VMEM_LIMIT = 56 * 1024 * 1024


def _nt_dot(a, b):
    return lax.dot_general(a, b, (((1,), (1,)), ((), ())), preferred_element_type=F32)


def _dot(a, b):
    return jnp.dot(a, b, preferred_element_type=F32)


def _rms(x, g):
    ms = jnp.mean(x * x, axis=-1, keepdims=True)
    return x * lax.rsqrt(ms + RMS_EPS) * g


def _lane_iota(shape):
    return lax.broadcasted_iota(jnp.int32, shape, len(shape) - 1)


def _pair_rms(y, gain):
    left = _lane_iota(y.shape) < HEAD_DIM
    y2 = y * y
    s_left = jnp.sum(jnp.where(left, y2, 0.0), axis=-1, keepdims=True)
    s_right = jnp.sum(y2, axis=-1, keepdims=True) - s_left
    r = jnp.where(left, lax.rsqrt(s_left / HEAD_DIM + RMS_EPS), lax.rsqrt(s_right / HEAD_DIM + RMS_EPS))
    return y * r * gain


def _split_kv(t, parity, ones_lane=None):
    lane = _lane_iota(t.shape)
    keep = (lane < HEAD_DIM) if parity == 0 else (lane >= HEAD_DIM)
    out = jnp.where(keep, t, jnp.zeros_like(t))
    if ones_lane is not None:
        out = jnp.where(lane == ones_lane, jnp.ones_like(t), out)
    return out


def _put_blocks(out_ref, slot, tile):
    for bk in range(tile.shape[0] // BLK):
        out_ref[bk, slot * BLK:(slot + 1) * BLK, :] = tile[bk * BLK:(bk + 1) * BLK].astype(out_ref.dtype)


def _two_tiles(xn, w_ref, o):
    y = _dot(xn, w_ref[:, o:o + 2 * LANES])
    return y[:, :LANES], y[:, LANES:]


def _proj_a_kernel(x_ref, g_ref, w_ref, gq_ref, gk_ref, q_out, kc_out, vt_out, qi_out, kic_out, wi_out):
    xn = _rms(x_ref[...], g_ref[...]).astype(MXU_DTYPE)
    two = functools.partial(_two_tiles, xn, w_ref)
    for p in range(0, PAIRS, 2):
        qa, qb = two(p * LANES)
        _put_blocks(q_out, p, _pair_rms(qa, gq_ref[...]))
        _put_blocks(q_out, p + 1, _pair_rms(qb, gq_ref[...]))
    o = HEADS * HEAD_DIM
    k, v = two(o)
    k = _pair_rms(k, gk_ref[...])
    _put_blocks(kc_out, 0, _split_kv(k, 0))
    _put_blocks(kc_out, 1, _split_kv(k, 1))
    _put_vt(vt_out, 0, v)
    o += 2 * LANES
    for hp in range(0, IDX_HEADS // 2, 2):
        qia, qib = two(o)
        _put_blocks(qi_out, hp, qia * IDX_DIM ** -0.5)
        _put_blocks(qi_out, hp + 1, qib * IDX_DIM ** -0.5)
        o += 2 * LANES
    ki, wi = two(o)
    _put_blocks(kic_out, 0, _split_kv(ki, 0))
    _put_blocks(kic_out, 1, _split_kv(ki, 1))
    wi_out[...] = wi * IDX_HEADS ** -0.5


def _put_vt(vt_out, slot, v):
    vt = jnp.transpose(v)
    row = lax.broadcasted_iota(jnp.int32, vt.shape, 0)
    vt = jnp.where(row == HEAD_DIM, 1.0, vt)
    for bk in range(vt.shape[1] // BLK):
        vt_out[bk, slot * VT_ROWS:(slot + 1) * VT_ROWS, :] = (
            vt[:VT_ROWS, bk * BLK:(bk + 1) * BLK].astype(vt_out.dtype))


def _proj_b_kernel(x_ref, g_ref, w_ref, gq_ref, gk_ref, q_out, kc_out, vt_out):
    xn = _rms(x_ref[...], g_ref[...]).astype(MXU_DTYPE)
    two = functools.partial(_two_tiles, xn, w_ref)
    for p in range(0, PAIRS, 2):
        qa, qb = two(p * LANES)
        _put_blocks(q_out, p, _pair_rms(qa, gq_ref[...]))
        _put_blocks(q_out, p + 1, _pair_rms(qb, gq_ref[...]))
    o = HEADS * HEAD_DIM
    for c in range(0, B_KV_HEADS, 2):
        for c1, k in zip((c, c + 1), two(o + c * LANES)):
            k = _pair_rms(k, gk_ref[...])
            _put_blocks(kc_out, 2 * c1, _split_kv(k, 0))
            _put_blocks(kc_out, 2 * c1 + 1, _split_kv(k, 1))
    o += B_KV_HEADS * LANES
    for c in range(0, B_KV_HEADS, 2):
        for c1, v in zip((c, c + 1), two(o + c * LANES)):
            _put_vt(vt_out, c1, v)


def _rope(y, cos_t, sin_t):
    lane = _lane_iota(y.shape)
    first_half = (lane >= C_NOPE) & (lane < C_NOPE + C_ROPE // 2)
    partner = jnp.where(first_half, pltpu.roll(y, LANES - C_ROPE // 2, 1), pltpu.roll(y, C_ROPE // 2, 1))
    return y * cos_t + partner * sin_t


def _proj_c_kernel(x_ref, g_ref, w_in_ref, gqa_ref, gkva_ref, wq_ref, wk_ref, wv_ref, gq_ref, gk_ref,
                   cos_ref, sin_ref, q_out, k_out, vt_out):
    xn = _rms(x_ref[...], g_ref[...]).astype(MXU_DTYPE)
    lat = _dot(xn, w_in_ref[...])
    qa = _rms(lat[:, :C_Q_RANK], gqa_ref[...]).astype(MXU_DTYPE)
    kva = _rms(lat[:, C_Q_RANK:C_Q_RANK + C_KV_RANK], gkva_ref[...]).astype(MXU_DTYPE)
    k_rope = lat[:, C_Q_RANK + C_KV_RANK:]
    cos_t, sin_t = cos_ref[...], sin_ref[...]
    v = _dot(kva, wv_ref[...])
    tm = v.shape[0]
    ones_row = jnp.where(lax.broadcasted_iota(jnp.int32, (VT_ROWS - HEAD_DIM, tm), 0) == 0, 1.0, 0.0)
    for p in range(PAIRS):
        v_t = jnp.transpose(v[:, p * LANES:(p + 1) * LANES])
        for e in range(2):
            h = 2 * p + e
            head_t = jnp.concatenate([v_t[e * HEAD_DIM:(e + 1) * HEAD_DIM], ones_row], axis=0)
            for bk in range(tm // BLK):
                vt_out[bk, h * VT_ROWS:(h + 1) * VT_ROWS, :] = (
                    head_t[:, bk * BLK:(bk + 1) * BLK].astype(vt_out.dtype))
    for h0 in range(0, HEADS, 2):
        q2 = _dot(qa, wq_ref[:, h0 * LANES:(h0 + 2) * LANES])
        k2 = _dot(kva, wk_ref[:, h0 * LANES:(h0 + 2) * LANES])
        for e in range(2):
            sl = slice((h0 + e) * LANES, (h0 + e + 1) * LANES)
            q = q2[:, e * LANES:(e + 1) * LANES]
            q = q * lax.rsqrt(jnp.sum(q * q, axis=-1, keepdims=True) / C_QK + RMS_EPS) * gq_ref[...]
            q_out[:, sl] = _rope(q, cos_t, sin_t).astype(q_out.dtype)
            k = k2[:, e * LANES:(e + 1) * LANES] + k_rope
            k = k * lax.rsqrt(jnp.sum(k * k, axis=-1, keepdims=True) / C_QK + RMS_EPS) * gk_ref[...]
            k_out[:, sl] = _rope(k, cos_t, sin_t).astype(k_out.dtype)


def _row_spec(tm, n):
    return pl.BlockSpec((tm, n), lambda i: (i, 0))


def _const_spec(shape):
    return pl.BlockSpec(shape, lambda i: (0,) * len(shape))


def _proj_call(kernel, x, consts, row_inputs, outs, tm=512):
    t = x.shape[0]
    in_specs = [_row_spec(tm, x.shape[1])] + [_const_spec(c.shape) for c in consts]
    in_specs += [_row_spec(tm, r.shape[1]) for r in row_inputs]
    out_shape, out_specs = [], []
    for kind, n, dtype in outs:
        if kind == "rows":
            out_shape.append(jax.ShapeDtypeStruct((t, n), dtype))
            out_specs.append(_row_spec(tm, n))
        else:
            out_shape.append(jax.ShapeDtypeStruct((t // BLK, n, LANES), dtype))
            out_specs.append(pl.BlockSpec((tm // BLK, n, LANES), lambda i: (i, 0, 0)))
    return pl.pallas_call(
        kernel,
        name=kernel.__name__.strip("_"),
        out_shape=out_shape,
        grid=(t // tm,),
        in_specs=in_specs,
        out_specs=out_specs,
        compiler_params=pltpu.CompilerParams(dimension_semantics=("parallel",), vmem_limit_bytes=VMEM_LIMIT),
    )(x, *consts, *row_inputs)


def _out_mlp_kernel(x_ref, a_ref, wo_ref, g_ref, wu_ref, wd_ref, o_ref, *, f_chunk):
    x1 = x_ref[...] + _dot(a_ref[...], wo_ref[...])
    h = _rms(x1, g_ref[...]).astype(MXU_DTYPE)
    acc = x1
    for c in range(D_FF // f_chunk):
        u = jnp.maximum(_dot(h, wu_ref[:, c * f_chunk:(c + 1) * f_chunk]), 0.0)
        acc = acc + _dot((u * u).astype(MXU_DTYPE), wd_ref[c * f_chunk:(c + 1) * f_chunk, :])
    o_ref[...] = acc


def _out_mlp_call(x, attn, w_out, g, w_up, w_down, tm=512, f_chunk=1024):
    t = x.shape[0]
    single = pl.Buffered(1)
    return pl.pallas_call(
        functools.partial(_out_mlp_kernel, f_chunk=f_chunk),
        name="out_mlp",
        out_shape=jax.ShapeDtypeStruct((t, D_MODEL), F32),
        grid=(t // tm,),
        in_specs=[
            _row_spec(tm, D_MODEL),
            _row_spec(tm, attn.shape[1]),
            pl.BlockSpec(w_out.shape, lambda i: (0, 0), pipeline_mode=single),
            _const_spec(g.shape),
            pl.BlockSpec(w_up.shape, lambda i: (0, 0), pipeline_mode=single),
            pl.BlockSpec(w_down.shape, lambda i: (0, 0), pipeline_mode=single),
        ],
        out_specs=_row_spec(tm, D_MODEL),
        compiler_params=pltpu.CompilerParams(dimension_semantics=("parallel",), vmem_limit_bytes=VMEM_LIMIT),
    )(x, attn, w_out, g, w_up, w_down)


def _swa_kernel(q_ref, kp_ref, kc_ref, vp_ref, vc_ref, bias_ref, sink_ref, o_ref):
    first_step = pl.program_id(1) == 0
    group_pairs = PAIRS // B_KV_HEADS
    w = group_pairs * BLK
    key = lax.broadcasted_iota(jnp.int32, (2 * BLK, w), 0)
    query = lax.broadcasted_iota(jnp.int32, (2 * BLK, w), 1) & (BLK - 1)
    rel = query + WINDOW - key
    in_band = (rel >= 0) & (rel < WINDOW)
    for blk in range(q_ref.shape[1]):
        k_prev, v_prev = (kp_ref, vp_ref) if blk == 0 else (kc_ref, vc_ref)
        prev = max(blk - 1, 0)
        ok = in_band & ((key >= WINDOW) | jnp.logical_not(first_step)) if blk == 0 else in_band
        for c in range(B_KV_HEADS):
            kk = jnp.concatenate([k_prev[0, prev, c * 2 * BLK:(c + 1) * 2 * BLK, :],
                                  kc_ref[0, blk, c * 2 * BLK:(c + 1) * 2 * BLK, :]], axis=0)
            s = _nt_dot(kk, q_ref[0, blk, c * w:(c + 1) * w, :])
            vt = jnp.concatenate([v_prev[0, prev, c * VT_ROWS:(c + 1) * VT_ROWS, :],
                                  vc_ref[0, blk, c * VT_ROWS:(c + 1) * VT_ROWS, :]], axis=1)
            halves = []
            for e in range(2):
                se = jnp.concatenate([s[e * BLK:(e + 1) * BLK], s[(2 + e) * BLK:(3 + e) * BLK]], axis=0)
                se = jnp.where(ok, se + bias_ref[c, e], NEG_INF)
                sink = sink_ref[c, e]
                m = jnp.maximum(jnp.max(se, axis=0, keepdims=True), sink)
                acc = _dot(vt, jnp.exp(se - m).astype(MXU_DTYPE))
                halves.append(acc[:HEAD_DIM] / (acc[HEAD_DIM:HEAD_DIM + 1] + jnp.exp(sink - m)))
            for pg in range(group_pairs):
                sl = slice(pg * BLK, (pg + 1) * BLK)
                pair = jnp.concatenate([halves[0][:, sl], halves[1][:, sl]], axis=0)
                p = c * group_pairs + pg
                o_ref[0, blk * BLK:(blk + 1) * BLK, p * LANES:(p + 1) * LANES] = (
                    jnp.transpose(pair).astype(o_ref.dtype))


def _swa_call(q, kc, vt, bias_t, sinks_t, batch, blocks_per_step=2):
    nb = q.shape[0] // batch
    g = blocks_per_step
    assert nb % g == 0
    per_batch = lambda a: a.reshape(batch, nb, *a.shape[1:])
    q, kc, vt = map(per_batch, (q, kc, vt))
    prev = lambda a: pl.BlockSpec((1, 1) + a.shape[2:], lambda bi, n: (bi, jnp.maximum(n * g - 1, 0), 0, 0))
    cur = lambda a: pl.BlockSpec((1, g) + a.shape[2:], lambda bi, n: (bi, n, 0, 0))
    return pl.pallas_call(
        _swa_kernel,
        name="swa_attn",
        out_shape=jax.ShapeDtypeStruct((batch, nb * BLK, HEADS * HEAD_DIM), MXU_DTYPE),
        grid=(batch, nb // g),
        in_specs=[cur(q), prev(kc), cur(kc), prev(vt), cur(vt),
                  pl.BlockSpec(bias_t.shape, lambda bi, n: (0, 0, 0, 0)),
                  pl.BlockSpec(sinks_t.shape, lambda bi, n: (0, 0, 0, 0))],
        out_specs=pl.BlockSpec((1, g * BLK, HEADS * HEAD_DIM), lambda bi, n: (bi, n, 0)),
        compiler_params=pltpu.CompilerParams(dimension_semantics=("parallel", "parallel"),
                                             vmem_limit_bytes=VMEM_LIMIT),
    )(q, kc, kc, vt, vt, bias_t, sinks_t)


def _pipeline_scratch(tk, n_chunks, lanes):
    return [pltpu.VMEM((2, 2, n_chunks, tk, lanes), F32), pltpu.VMEM((2, 2, n_chunks, tk, lanes), MXU_DTYPE),
            pltpu.VMEM((2, 2, n_chunks, 1, lanes), F32), pltpu.VMEM((2, n_chunks, 1, lanes), F32),
            pltpu.VMEM((2, n_chunks, VT_ROWS, lanes), F32)]


def _init_pipeline(s_scr, p_scr, alpha_scr, m_scr, acc_scr):
    m_scr[...] = jnp.full(m_scr.shape, NEG_INF, F32)
    acc_scr[...] = jnp.zeros(acc_scr.shape, F32)
    s_scr[1] = jnp.full(s_scr.shape[1:], NEG_INF, F32)
    p_scr[0] = jnp.zeros(p_scr.shape[1:], p_scr.dtype)
    alpha_scr[0] = jnp.ones(alpha_scr.shape[1:], F32)


def _softmax_stage(s_scr, p_scr, alpha_scr, m_scr, slot, chunks=(0,)):
    for e in range(2):
        for ch in chunks:
            s = s_scr[slot, e, ch]
            m_old = m_scr[e, ch]
            m_new = jnp.maximum(m_old, jnp.max(s, axis=0, keepdims=True))
            m_safe = jnp.where(m_new == NEG_INF, 0.0, m_new)
            p_scr[slot, e, ch] = jnp.exp2(s - m_safe).astype(p_scr.dtype)
            alpha_scr[slot, e, ch] = jnp.exp2(m_old - m_safe)
            m_scr[e, ch] = m_new


def _pv_stage(vts, p_scr, alpha_scr, acc_scr, slot, chunks=(0,)):
    for e in range(2):
        for ch in chunks:
            acc_scr[e, ch] = acc_scr[e, ch] * alpha_scr[slot, e, ch] + _dot(vts[e], p_scr[slot, e, ch])


def _normalized_pair(acc_scr, ch, lanes):
    a0, a1 = acc_scr[0, ch, :, lanes], acc_scr[1, ch, :, lanes]
    return jnp.concatenate([a0[:HEAD_DIM] / a0[HEAD_DIM:HEAD_DIM + 1],
                            a1[:HEAD_DIM] / a1[HEAD_DIM:HEAD_DIM + 1]], axis=0)


def _mla_kernel(q_ref, k_ref, vt_ref, o_ref, s_scr, p_scr, alpha_scr, m_scr, acc_scr, *, tq, tk):
    seq = q_ref.shape[1]
    n_diag = tq // tk
    sub = tk // BLK
    all_chunks = tuple(range(n_diag))
    below_diagonal = (lax.broadcasted_iota(jnp.int32, (tk, tk), 0)
                      <= lax.broadcasted_iota(jnp.int32, (tk, tk), 1))

    def q_tile(qt, _):
        q0 = pl.multiple_of(qt * tq, tq)
        qs = [q_ref[0, pl.ds(q0, tq), e * LANES:(e + 1) * LANES] for e in range(2)]
        n_full = qt * n_diag
        _init_pipeline(s_scr, p_scr, alpha_scr, m_scr, acc_scr)

        def vts(kt):
            kt = jnp.maximum(kt, 0)
            return [jnp.concatenate([vt_ref[0, kt * sub + c, e * VT_ROWS:(e + 1) * VT_ROWS, :]
                                     for c in range(sub)], axis=1) for e in range(2)]

        def qk_stage(kt, slot, diag):
            k0 = pl.multiple_of(kt * tk, tk)
            first = 0 if diag is None else diag
            for e in range(2):
                s = _nt_dot(k_ref[0, pl.ds(k0, tk), e * LANES:(e + 1) * LANES], qs[e][first * tk:])
                for ch in range(first, n_diag):
                    sc = s[:, (ch - first) * tk:(ch - first + 1) * tk]
                    if ch == diag:
                        sc = jnp.where(below_diagonal, sc, NEG_INF)
                    s_scr[slot, e, ch] = sc

        def chunks_of(c):
            return all_chunks[max(c, 0):]

        def full_steps(kk, carry):
            for u in range(n_diag):
                kt = n_diag * kk + u
                _pv_stage(vts(kt - 2), p_scr, alpha_scr, acc_scr, u % 2, all_chunks)
                qk_stage(kt, u % 2, None)
                _softmax_stage(s_scr, p_scr, alpha_scr, m_scr, 1 - u % 2, all_chunks)
            return carry

        lax.fori_loop(0, qt, full_steps, 0)
        for c in range(n_diag + 2):
            par = c % 2
            _pv_stage(vts(n_full + c - 2), p_scr, alpha_scr, acc_scr, par, chunks_of(c - 2))
            if c < n_diag:
                qk_stage(n_full + c, par, c)
            if c < n_diag + 1:
                _softmax_stage(s_scr, p_scr, alpha_scr, m_scr, 1 - par, chunks_of(c - 1))
        for ch in all_chunks:
            o_ref[0, pl.ds(pl.multiple_of(q0 + ch * tk, tk), tk), :] = jnp.transpose(
                _normalized_pair(acc_scr, ch, slice(None))).astype(o_ref.dtype)
        return 0

    lax.fori_loop(0, seq // tq, q_tile, 0)


def _mla_call(q, k, vt, tq=1024, tk=256):
    b, l, _ = q.shape
    tq, tk = min(tq, l), min(tk, l)
    assert (tq // tk) % 2 == 0 and l % tq == 0
    nb = l // BLK
    vt = vt.reshape(b, nb, HEADS * VT_ROWS, LANES)
    return pl.pallas_call(
        functools.partial(_mla_kernel, tq=tq, tk=tk),
        name="mla_attn",
        out_shape=jax.ShapeDtypeStruct((b, l, HEADS * HEAD_DIM), MXU_DTYPE),
        grid=(b, PAIRS),
        in_specs=[
            pl.BlockSpec((1, l, 2 * LANES), lambda bi, p: (bi, 0, p)),
            pl.BlockSpec((1, l, 2 * LANES), lambda bi, p: (bi, 0, p)),
            pl.BlockSpec((1, nb, 2 * VT_ROWS, LANES), lambda bi, p: (bi, 0, p, 0)),
        ],
        out_specs=pl.BlockSpec((1, l, LANES), lambda bi, p: (bi, 0, p)),
        scratch_shapes=_pipeline_scratch(tk, tq // tk, tk),
        compiler_params=pltpu.CompilerParams(dimension_semantics=("parallel", "parallel"),
                                             vmem_limit_bytes=VMEM_LIMIT),
    )(q, k, vt)


def _dsa_kernel(q_ref, qi_ref, wi_ref, kc_ref, vt_ref, kic_ref, bias_ref, tri_ref, o_ref,
                idx_scr, hi_scr, lo_scr, s_scr, p_scr, alpha_scr, m_scr, acc_scr, *, k_sel):
    i = pl.program_id(1)
    n_tiles = i + 1
    n_groups = i // TILE_GROUP + 1
    key_row = lax.broadcasted_iota(jnp.int32, (BLK, BLK), 0)
    query_col = lax.broadcasted_iota(jnp.int32, (BLK, BLK), 1)

    def admissible(j):
        return (j < i) | ((j == i) & (key_row <= query_col))

    wt = jnp.transpose(wi_ref[0])
    qi = qi_ref[0, 0]

    def score_group(g, carry):
        for u in range(TILE_GROUP):
            j = g * TILE_GROUP + u
            d = jnp.maximum(_nt_dot(kic_ref[0, j], qi), 0.0)
            sc = jnp.zeros((BLK, BLK), F32)
            for hp in range(IDX_HEADS // 2):
                for e in range(2):
                    h = 2 * hp + e
                    sc = sc + wt[h:h + 1, :] * d[e * BLK:(e + 1) * BLK, hp * BLK:(hp + 1) * BLK]
            sc = jnp.where(admissible(j), sc, NEG_INF)
            idx_scr[j] = sc
            bits = lax.bitcast_convert_type(sc, jnp.int32)
            key = jnp.where(bits >= 0, bits, bits ^ jnp.int32(0x7FFFFFFF))
            hi_scr[j] = lax.shift_right_arithmetic(key, 16).astype(jnp.int16)
            lo_scr[j] = ((key & 0xFFFF) - HALF_RANGE).astype(jnp.int16)
        return carry

    lax.fori_loop(0, n_groups, score_group, 0)

    def count(pred):
        def body(g, c):
            for u in range(TILE_GROUP):
                c = c + jnp.where(pred(idx_scr[g * TILE_GROUP + u]), 1.0, 0.0)
            return c
        c = lax.fori_loop(0, n_groups, body, jnp.zeros((BLK, BLK), F32))
        return jnp.sum(c, axis=0, keepdims=True)

    n_blocks = idx_scr.shape[0]
    step = min(SEARCH_STEP, n_blocks)
    sizes = tuple(range(step, n_blocks + 1, step))
    lowest = jnp.full((BLK, BLK), -HALF_RANGE, jnp.int16)
    for u in range(step - TILE_GROUP):
        hi_scr[n_groups * TILE_GROUP + u] = lowest
        lo_scr[n_groups * TILE_GROUP + u] = lowest

    def count16(ref, cand, n_static):
        cand16 = cand.astype(jnp.int16)
        c = jnp.zeros((BLK, BLK), jnp.int16)
        for j in range(n_static):
            c = c + jnp.where(ref[j] >= cand16, jnp.int16(1), jnp.int16(0))
        return jnp.sum(c.astype(jnp.int32), axis=0, keepdims=True)

    def search16(ref, need, n_static):
        def bit_step(it, v):
            cand_v = v | lax.shift_left(jnp.int32(1), 15 - it)
            return jnp.where(count16(ref, cand_v - HALF_RANGE, n_static) >= need, cand_v, v)
        return lax.fori_loop(0, 16, bit_step, jnp.zeros((1, BLK), jnp.int32))

    def search(n_static):
        t_hi = search16(hi_scr, k_sel, n_static) - HALF_RANGE
        above = jnp.where(t_hi == HALF_RANGE - 1, 0,
                          count16(hi_scr, jnp.minimum(t_hi + 1, HALF_RANGE - 1), n_static))
        t_hi16 = t_hi.astype(jnp.int16)
        for j in range(n_static):
            lo_scr[j] = jnp.where(hi_scr[j] == t_hi16, lo_scr[j], jnp.int16(-HALF_RANGE))
        t_lo = search16(lo_scr, k_sel - above, n_static)
        key = lax.shift_left(t_hi, 16) | t_lo
        bits = jnp.where(key >= 0, key, key ^ jnp.int32(0x7FFFFFFF))
        t = lax.bitcast_convert_type(bits, F32)
        return jnp.where(t != t, NEG_INF, t)

    def run_search():
        which = (n_groups * TILE_GROUP + step - 1) // step - 1
        return lax.switch(which, [functools.partial(search, n) for n in sizes])

    thr = lax.cond(n_tiles * BLK > k_sel, run_search, lambda: jnp.full((1, BLK), NEG_INF, F32))

    n_gt = count(lambda s: s > thr)
    n_ge = count(lambda s: s >= thr)
    tied = (n_ge > k_sel) & (thr > NEG_INF)

    @pl.when(jnp.max(jnp.where(tied, 1.0, 0.0)) > 0.0)
    def _():
        need = k_sel - n_gt

        def tie_tile(j, seen):
            s = idx_scr[j]
            eq = s == thr
            eq_f = jnp.where(eq, 1.0, 0.0).astype(MXU_DTYPE)
            rank = seen + _dot(tri_ref[...], eq_f)
            idx_scr[j] = jnp.where(eq & tied & (rank > need), NEG_INF, s)
            return seen + jnp.sum(eq_f.astype(F32), axis=0, keepdims=True)

        lax.fori_loop(0, n_tiles, tie_tile, jnp.zeros((1, BLK), F32))

    q = q_ref[0, 0]
    _init_pipeline(s_scr, p_scr, alpha_scr, m_scr, acc_scr)

    def qk_stage(j, slot, near, valid=None):
        sel = idx_scr[j] >= thr
        if near is not None:
            sel = sel & admissible(j)
        if valid is not None:
            sel = sel & valid
        mask = jnp.where(sel, 0.0, NEG_INF)
        mask = jnp.concatenate([mask] * PAIRS, axis=1)
        s = _nt_dot(kc_ref[0, j], q)
        if near is not None:
            s = s + bias_ref[near]
        for e in range(2):
            s_scr[slot, e, 0] = s[e * BLK:(e + 1) * BLK] + mask

    def vts(j):
        vt = vt_ref[0, jnp.maximum(j, 0)]
        return [vt, vt]

    n_far = jnp.maximum(i - 1, 0)
    n_trips = (n_far + TILE_GROUP - 1) // TILE_GROUP

    def far_steps(kk, carry):
        for u in range(TILE_GROUP):
            j = TILE_GROUP * kk + u
            _pv_stage(vts(j - 2), p_scr, alpha_scr, acc_scr, u % 2)
            qk_stage(jnp.minimum(j, i), u % 2, None, j < n_far)
            _softmax_stage(s_scr, p_scr, alpha_scr, m_scr, 1 - u % 2)
        return carry

    lax.fori_loop(0, n_trips, far_steps, 0)
    j0 = TILE_GROUP * n_trips
    tail = [(n_far, 0, i >= 1), (i, 1, None)]
    done = [jnp.minimum(j0 - 2, i), jnp.minimum(j0 - 1, i)] + [t[0] for t in tail]
    for c in range(len(tail) + 2):
        par = c % 2
        _pv_stage(vts(done[c]), p_scr, alpha_scr, acc_scr, par)
        if c < len(tail):
            qk_stage(tail[c][0], par, tail[c][1], tail[c][2])
        if c < len(tail) + 1:
            _softmax_stage(s_scr, p_scr, alpha_scr, m_scr, 1 - par)

    for p in range(PAIRS):
        sl = slice(p * BLK, (p + 1) * BLK)
        o_ref[0, :, sl] = jnp.transpose(_normalized_pair(acc_scr, 0, sl)).astype(o_ref.dtype)


def _dsa_call(q, qi, wi, kc, vt, kic, bias_t, tri, batch):
    nb = q.shape[0] // batch
    l = nb * BLK
    k_sel = min(TOPK_MAX, l // 4)
    assert nb % TILE_GROUP == 0
    per_batch =lambda a: a.reshape(batch, nb, *a.shape[1:])
    q, qi, kc, vt, kic = map(per_batch, (q, qi, kc, vt, kic))
    blk = lambda a: pl.BlockSpec((1, 1) + a.shape[2:], lambda bi, n: (bi, n, 0, 0))
    seq = lambda a: pl.BlockSpec((1,) + a.shape[1:], lambda bi, n: (bi, 0, 0, 0))
    return pl.pallas_call(
        functools.partial(_dsa_kernel, k_sel=k_sel),
        name="dsa_attn",
        out_shape=jax.ShapeDtypeStruct((batch, l, HEADS * HEAD_DIM), MXU_DTYPE),
        grid=(batch, nb),
        in_specs=[
            blk(q),
            blk(qi),
            pl.BlockSpec((1, BLK, LANES), lambda bi, n: (bi, n, 0)),
            seq(kc),
            seq(vt),
            seq(kic),
            pl.BlockSpec(bias_t.shape, lambda bi, n: (0, 0, 0)),
            pl.BlockSpec(tri.shape, lambda bi, n: (0, 0)),
        ],
        out_specs=pl.BlockSpec((1, BLK, HEADS * HEAD_DIM), lambda bi, n: (bi, n, 0)),
        scratch_shapes=[pltpu.VMEM((nb, BLK, BLK), F32), pltpu.VMEM((nb + SEARCH_STEP, BLK, BLK), jnp.int16),
                        pltpu.VMEM((nb + SEARCH_STEP, BLK, BLK), jnp.int16)]
        + _pipeline_scratch(BLK, 1, PAIRS * BLK),
        compiler_params=pltpu.CompilerParams(dimension_semantics=("parallel", "arbitrary"),
                                             vmem_limit_bytes=VMEM_LIMIT),
    )(q, qi, wi.reshape(batch, l, LANES), kc, vt, kic, bias_t, tri)


def _t5_bucket(rel):
    n = jnp.maximum(rel, 0)
    max_exact = N_BUCKETS // 2
    nf = jnp.maximum(n, 1).astype(F32)
    large = max_exact + (jnp.log(nf / max_exact) / math.log(MAX_DISTANCE / max_exact)
                         * (N_BUCKETS - max_exact)).astype(jnp.int32)
    large = jnp.minimum(large, N_BUCKETS - 1)
    return jnp.where(n < max_exact, n, large)


def _band_bias(rel_bias):
    qi = jnp.arange(BLK, dtype=jnp.int32)[:, None] + BLK
    kj = jnp.arange(2 * BLK, dtype=jnp.int32)[None, :]
    return jnp.transpose(rel_bias[_t5_bucket(qi - kj)], (2, 0, 1))


def _pair_gain(g, scale=1.0):
    return (jnp.concatenate([g, g]) * scale).reshape(1, LANES).astype(F32)


def _dup(w, n_heads):
    d = w.shape[0]
    w = w.reshape(d, n_heads, 1, HEAD_DIM)
    return jnp.broadcast_to(w, (d, n_heads, 2, HEAD_DIM)).reshape(d, n_heads * LANES)


def _layer_a(x, g_mix, w_in, q_gain, k_gain):
    nq = HEADS * HEAD_DIM
    sizes = [nq, HEAD_DIM, HEAD_DIM, IDX_HEADS * IDX_DIM, IDX_DIM, IDX_HEADS]
    wq, wk, wv, wqi, wki, wwi = jnp.split(w_in, np.cumsum(sizes)[:-1].tolist(), axis=1)
    pad_to_lanes = lambda a: jnp.pad(a, ((0, 0), (0, LANES - a.shape[1])))
    w = jnp.concatenate([wq, _dup(wk, 1), pad_to_lanes(wv), wqi, _dup(wki, 1), pad_to_lanes(wwi)],
                        axis=1).astype(MXU_DTYPE)
    consts = [g_mix.reshape(1, -1), w, _pair_gain(q_gain, HEAD_DIM ** -0.5 * LOG2_E), _pair_gain(k_gain)]
    outs = [("stacked", PAIRS * BLK, MXU_DTYPE), ("stacked", 2 * BLK, MXU_DTYPE), ("stacked", VT_ROWS, MXU_DTYPE),
            ("stacked", IDX_HEADS // 2 * BLK, MXU_DTYPE), ("stacked", 2 * BLK, MXU_DTYPE), ("rows", LANES, F32)]
    return _proj_call(_proj_a_kernel, x, consts, [], outs)


def _layer_b(x, g_mix, w_in, q_gain, k_gain):
    nq = HEADS * HEAD_DIM
    nkv = B_KV_HEADS * HEAD_DIM
    wq, wk, wv = jnp.split(w_in, [nq, nq + nkv], axis=1)
    wv = jnp.pad(wv.reshape(-1, B_KV_HEADS, HEAD_DIM), ((0, 0), (0, 0), (0, LANES - HEAD_DIM)))
    w = jnp.concatenate([wq, _dup(wk, B_KV_HEADS), wv.reshape(-1, B_KV_HEADS * LANES)], axis=1).astype(MXU_DTYPE)
    consts = [g_mix.reshape(1, -1), w, _pair_gain(q_gain, HEAD_DIM ** -0.5), _pair_gain(k_gain)]
    outs = [("stacked", PAIRS * BLK, MXU_DTYPE), ("stacked", B_KV_HEADS * 2 * BLK, MXU_DTYPE),
            ("stacked", B_KV_HEADS * VT_ROWS, MXU_DTYPE)]
    return _proj_call(_proj_b_kernel, x, consts, [], outs)


def _pad_heads(w, width):
    d = w.shape[0]
    w = w.reshape(d, HEADS, width)
    return jnp.pad(w, ((0, 0), (0, 0), (0, LANES - width))).reshape(d, HEADS * LANES)


def _layer_c(x, g_mix, w_in, q_a_gain, w_q_b, kv_a_gain, w_kv_b, q_gain, k_gain, cos_t, sin_t):
    d = w_in.shape[0]
    w_rope = jnp.zeros((d, LANES), w_in.dtype).at[:, C_NOPE:C_QK].set(w_in[:, C_Q_RANK + C_KV_RANK:])
    w_in_p = jnp.concatenate([w_in[:, :C_Q_RANK + C_KV_RANK], w_rope], axis=1).astype(MXU_DTYPE)
    wq = _pad_heads(w_q_b, C_QK).astype(MXU_DTYPE)
    w_kv = w_kv_b.reshape(C_KV_RANK, HEADS, C_NOPE + HEAD_DIM)
    wk = _pad_heads(w_kv[:, :, :C_NOPE].reshape(C_KV_RANK, -1), C_NOPE).astype(MXU_DTYPE)
    wv = w_kv[:, :, C_NOPE:].reshape(C_KV_RANK, -1).astype(MXU_DTYPE)
    pad_gain = lambda g, s: (jnp.pad(g, (0, LANES - C_QK)) * s).reshape(1, LANES).astype(F32)
    consts = [g_mix.reshape(1, -1), w_in_p, q_a_gain.reshape(1, -1), kv_a_gain.reshape(1, -1), wq, wk, wv,
              pad_gain(q_gain, C_QK ** -0.5 * LOG2_E), pad_gain(k_gain, 1.0)]
    outs = [("rows", HEADS * LANES, MXU_DTYPE), ("rows", HEADS * LANES, MXU_DTYPE),
            ("stacked", HEADS * VT_ROWS, MXU_DTYPE)]
    return _proj_call(_proj_c_kernel, x, consts, [cos_t, sin_t], outs)


def kernel(x, positions, rel_bias, norm_mix, norm_mlp, w_up, w_down, a_w_in, a_q_gain, a_k_gain, a_w_out,
           b_w_in, b_q_gain, b_k_gain, b_sinks, b_w_out, c_w_in, c_q_a_gain, c_w_q_b, c_kv_a_gain, c_w_kv_b,
           c_q_gain, c_k_gain, c_w_out):
    b, l, d = x.shape
    t = b * l
    depth = norm_mix.shape[0]

    band = _band_bias(rel_bias.astype(F32))
    far = rel_bias[N_BUCKETS - 1].astype(F32)[:, None, None]
    bias_t = jnp.transpose(((band - far) * LOG2_E).reshape(PAIRS, 2, BLK, 2, BLK), (3, 1, 4, 0, 2)).reshape(
        2, 2 * BLK, PAIRS * BLK)
    tri = jnp.asarray(np.arange(BLK)[:, None] >= np.arange(BLK)[None, :], MXU_DTYPE)

    inv_freq = ROPE_THETA ** (-jnp.arange(0, C_ROPE, 2, dtype=F32) / C_ROPE)
    ang = positions.astype(F32).reshape(t, 1) * inv_freq
    cos, sin = jnp.cos(ang), jnp.sin(ang)
    cos_t = jnp.concatenate([jnp.ones((t, C_NOPE), F32), cos, cos, jnp.zeros((t, LANES - C_QK), F32)], axis=1)
    sin_t = jnp.concatenate([jnp.zeros((t, C_NOPE), F32), -sin, sin, jnp.zeros((t, LANES - C_QK), F32)], axis=1)

    xf = x.reshape(t, d).astype(F32)
    ia = ib = ic = 0
    for i in range(depth):
        kind = i % 3
        g_mix = norm_mix[i]
        if kind == 0:
            q, kc, vt, qi, kic, wi = _layer_a(xf, g_mix, a_w_in[ia], a_q_gain[ia], a_k_gain[ia])
            attn = _dsa_call(q, qi, wi, kc, vt, kic, bias_t, tri, b)
            w_out = a_w_out[ia]
            ia += 1
        elif kind == 1:
            q, kc, vt = _layer_b(xf, g_mix, b_w_in[ib], b_q_gain[ib], b_k_gain[ib])
            group_pairs = PAIRS // B_KV_HEADS
            swa_bias = jnp.transpose(band.reshape(B_KV_HEADS, group_pairs, 2, BLK, 2 * BLK),
                                     (0, 2, 4, 1, 3)).reshape(B_KV_HEADS, 2, 2 * BLK, group_pairs * BLK)
            sinks = jnp.transpose(b_sinks[ib].astype(F32).reshape(B_KV_HEADS, group_pairs, 2), (0, 2, 1))
            sinks = jnp.broadcast_to(sinks[..., None], (B_KV_HEADS, 2, group_pairs, BLK)).reshape(
                B_KV_HEADS, 2, 1, group_pairs * BLK)
            attn = _swa_call(q, kc, vt, swa_bias, sinks, b)
            w_out = b_w_out[ib]
            ib += 1
        else:
            q, k, vt = _layer_c(xf, g_mix, c_w_in[ic], c_q_a_gain[ic], c_w_q_b[ic], c_kv_a_gain[ic],
                               c_w_kv_b[ic], c_q_gain[ic], c_k_gain[ic], cos_t, sin_t)
            r3 = lambda a: a.reshape(b, l, a.shape[-1])
            attn = _mla_call(r3(q), r3(k), vt)
            w_out = c_w_out[ic]
            ic += 1
        xf = _out_mlp_call(xf, attn.reshape(t, -1), w_out.astype(MXU_DTYPE), norm_mlp[i].reshape(1, -1),
                           w_up[i].astype(MXU_DTYPE), w_down[i].astype(MXU_DTYPE))
    return xf.reshape(b, l, d).astype(x.dtype)
```

```python
import functools
import math

import numpy as np
import jax
import jax.numpy as jnp
from jax import lax
from jax.experimental import pallas as pl
from jax.experimental.pallas import tpu as pltpu

F32 = jnp.float32
MXU_DTYPE = jnp.bfloat16

D_MODEL = 1024
D_FF = 4 * D_MODEL
RMS_EPS = 1e-6
N_BUCKETS = 32
MAX_DISTANCE = 128
HEADS = 16
HEAD_DIM = 64
PAIRS = HEADS // 2
LANES = 128
BLK = 128
IDX_HEADS = 8
IDX_DIM = 64
TOPK_MAX = 256
B_KV_HEADS = 4
WINDOW = 128
C_Q_RANK = 256
C_KV_RANK = 128
C_NOPE = 64
C_ROPE = 32
C_QK = C_NOPE + C_ROPE
ROPE_THETA = 10000.0
NEG_INF = float("-inf")
LOG2_E = math.log2(math.e)
HALF_RANGE = 1 << 15
TILE_GROUP = 4
SEARCH_STEP = 2 * TILE_GROUP
VT_ROWS = HEAD_DIM + 16
VMEM_LIMIT = 56 * 1024 * 1024


def _nt_dot(a, b):
    return lax.dot_general(a, b, (((1,), (1,)), ((), ())), preferred_element_type=F32)


def _dot(a, b):
    return jnp.dot(a, b, preferred_element_type=F32)


def _rms(x, g):
    ms = jnp.mean(x * x, axis=-1, keepdims=True)
    return x * lax.rsqrt(ms + RMS_EPS) * g


def _lane_iota(shape):
    return lax.broadcasted_iota(jnp.int32, shape, len(shape) - 1)


def _pair_rms(y, gain):
    left = _lane_iota(y.shape) < HEAD_DIM
    y2 = y * y
    s_left = jnp.sum(jnp.where(left, y2, 0.0), axis=-1, keepdims=True)
    s_right = jnp.sum(y2, axis=-1, keepdims=True) - s_left
    r = jnp.where(left, lax.rsqrt(s_left / HEAD_DIM + RMS_EPS), lax.rsqrt(s_right / HEAD_DIM + RMS_EPS))
    return y * r * gain


def _split_kv(t, parity, ones_lane=None):
    lane = _lane_iota(t.shape)
    keep = (lane < HEAD_DIM) if parity == 0 else (lane >= HEAD_DIM)
    out = jnp.where(keep, t, jnp.zeros_like(t))
    if ones_lane is not None:
        out = jnp.where(lane == ones_lane, jnp.ones_like(t), out)
    return out


def _put_blocks(out_ref, slot, tile):
    for bk in range(tile.shape[0] // BLK):
        out_ref[bk, slot * BLK:(slot + 1) * BLK, :] = tile[bk * BLK:(bk + 1) * BLK].astype(out_ref.dtype)


def _two_tiles(xn, w_ref, o):
    y = _dot(xn, w_ref[:, o:o + 2 * LANES])
    return y[:, :LANES], y[:, LANES:]


def _proj_a_kernel(x_ref, g_ref, w_ref, gq_ref, gk_ref, q_out, kc_out, vt_out, qi_out, kic_out, wi_out):
    xn = _rms(x_ref[...], g_ref[...]).astype(MXU_DTYPE)
    two = functools.partial(_two_tiles, xn, w_ref)
    for p in range(0, PAIRS, 2):
        qa, qb = two(p * LANES)
        _put_blocks(q_out, p, _pair_rms(qa, gq_ref[...]))
        _put_blocks(q_out, p + 1, _pair_rms(qb, gq_ref[...]))
    o = HEADS * HEAD_DIM
    k, v = two(o)
    k = _pair_rms(k, gk_ref[...])
    _put_blocks(kc_out, 0, _split_kv(k, 0))
    _put_blocks(kc_out, 1, _split_kv(k, 1))
    _put_vt(vt_out, 0, v)
    o += 2 * LANES
    for hp in range(0, IDX_HEADS // 2, 2):
        qia, qib = two(o)
        _put_blocks(qi_out, hp, qia * IDX_DIM ** -0.5)
        _put_blocks(qi_out, hp + 1, qib * IDX_DIM ** -0.5)
        o += 2 * LANES
    ki, wi = two(o)
    _put_blocks(kic_out, 0, _split_kv(ki, 0))
    _put_blocks(kic_out, 1, _split_kv(ki, 1))
    wi_out[...] = wi * IDX_HEADS ** -0.5


def _put_vt(vt_out, slot, v):
    vt = jnp.transpose(v)
    row = lax.broadcasted_iota(jnp.int32, vt.shape, 0)
    vt = jnp.where(row == HEAD_DIM, 1.0, vt)
    for bk in range(vt.shape[1] // BLK):
        vt_out[bk, slot * VT_ROWS:(slot + 1) * VT_ROWS, :] = (
            vt[:VT_ROWS, bk * BLK:(bk + 1) * BLK].astype(vt_out.dtype))


def _proj_b_kernel(x_ref, g_ref, w_ref, gq_ref, gk_ref, q_out, kc_out, vt_out):
    xn = _rms(x_ref[...], g_ref[...]).astype(MXU_DTYPE)
    two = functools.partial(_two_tiles, xn, w_ref)
    for p in range(0, PAIRS, 2):
        qa, qb = two(p * LANES)
        _put_blocks(q_out, p, _pair_rms(qa, gq_ref[...]))
        _put_blocks(q_out, p + 1, _pair_rms(qb, gq_ref[...]))
    o = HEADS * HEAD_DIM
    for c in range(0, B_KV_HEADS, 2):
        for c1, k in zip((c, c + 1), two(o + c * LANES)):
            k = _pair_rms(k, gk_ref[...])
            _put_blocks(kc_out, 2 * c1, _split_kv(k, 0))
            _put_blocks(kc_out, 2 * c1 + 1, _split_kv(k, 1))
    o += B_KV_HEADS * LANES
    for c in range(0, B_KV_HEADS, 2):
        for c1, v in zip((c, c + 1), two(o + c * LANES)):
            _put_vt(vt_out, c1, v)


def _norm_rope(y, own, partner):
    y = y * lax.rsqrt(jnp.sum(y * y, axis=-1, keepdims=True) / C_QK + RMS_EPS)
    return y * own + pltpu.roll(y, LANES // 2, 1) * partner


def _proj_c_kernel(x_ref, g_ref, w_in_ref, gqa_ref, gkva_ref, wq_ref, wk_ref, wv_ref,
                   q_own_ref, q_partner_ref, k_own_ref, k_partner_ref, q_out, k_out, vt_out):
    xn = _rms(x_ref[...], g_ref[...]).astype(MXU_DTYPE)
    lat = _dot(xn, w_in_ref[...])
    qa = _rms(lat[:, :C_Q_RANK], gqa_ref[...]).astype(MXU_DTYPE)
    kva = _rms(lat[:, C_Q_RANK:C_Q_RANK + C_KV_RANK], gkva_ref[...]).astype(MXU_DTYPE)
    k_rope = lat[:, C_Q_RANK + C_KV_RANK:]
    q_own, q_partner = q_own_ref[...], q_partner_ref[...]
    k_own, k_partner = k_own_ref[...], k_partner_ref[...]
    v = _dot(kva, wv_ref[...])
    tm = v.shape[0]
    ones_row = jnp.where(lax.broadcasted_iota(jnp.int32, (VT_ROWS - HEAD_DIM, tm), 0) == 0, 1.0, 0.0)
    for p in range(PAIRS):
        v_t = jnp.transpose(v[:, p * LANES:(p + 1) * LANES])
        for e in range(2):
            h = 2 * p + e
            head_t = jnp.concatenate([v_t[e * HEAD_DIM:(e + 1) * HEAD_DIM], ones_row], axis=0)
            for bk in range(tm // BLK):
                vt_out[bk, h * VT_ROWS:(h + 1) * VT_ROWS, :] = (
                    head_t[:, bk * BLK:(bk + 1) * BLK].astype(vt_out.dtype))
    for h0 in range(0, HEADS, 2):
        q2 = _dot(qa, wq_ref[:, h0 * LANES:(h0 + 2) * LANES])
        k2 = _dot(kva, wk_ref[:, h0 * LANES:(h0 + 2) * LANES])
        for e in range(2):
            sl = slice((h0 + e) * LANES, (h0 + e + 1) * LANES)
            q_out[:, sl] = _norm_rope(q2[:, e * LANES:(e + 1) * LANES], q_own, q_partner).astype(q_out.dtype)
            k_out[:, sl] = _norm_rope(k2[:, e * LANES:(e + 1) * LANES] + k_rope,
                                      k_own, k_partner).astype(k_out.dtype)


def _row_spec(tm, n):
    return pl.BlockSpec((tm, n), lambda i: (i, 0))


def _const_spec(shape):
    return pl.BlockSpec(shape, lambda i: (0,) * len(shape))


def _proj_call(kernel, x, consts, row_inputs, outs, tm=512):
    t = x.shape[0]
    in_specs = [_row_spec(tm, x.shape[1])] + [_const_spec(c.shape) for c in consts]
    in_specs += [_row_spec(tm, r.shape[1]) for r in row_inputs]
    out_shape, out_specs = [], []
    for kind, n, dtype in outs:
        if kind == "rows":
            out_shape.append(jax.ShapeDtypeStruct((t, n), dtype))
            out_specs.append(_row_spec(tm, n))
        else:
            out_shape.append(jax.ShapeDtypeStruct((t // BLK, n, LANES), dtype))
            out_specs.append(pl.BlockSpec((tm // BLK, n, LANES), lambda i: (i, 0, 0)))
    return pl.pallas_call(
        kernel,
        name=kernel.__name__.strip("_"),
        out_shape=out_shape,
        grid=(t // tm,),
        in_specs=in_specs,
        out_specs=out_specs,
        compiler_params=pltpu.CompilerParams(dimension_semantics=("parallel",), vmem_limit_bytes=VMEM_LIMIT),
    )(x, *consts, *row_inputs)


def _out_mlp_kernel(x_ref, a_ref, wo_ref, g_ref, wu_ref, wd_ref, o_ref, *, f_chunk):
    x1 = x_ref[...] + _dot(a_ref[...], wo_ref[...])
    h = _rms(x1, g_ref[...]).astype(MXU_DTYPE)
    acc = x1
    for c in range(D_FF // f_chunk):
        u = jnp.maximum(_dot(h, wu_ref[:, c * f_chunk:(c + 1) * f_chunk]), 0.0)
        acc = acc + _dot((u * u).astype(MXU_DTYPE), wd_ref[c * f_chunk:(c + 1) * f_chunk, :])
    o_ref[...] = acc


def _out_mlp_call(x, attn, w_out, g, w_up, w_down, tm=512, f_chunk=1024):
    t = x.shape[0]
    single = pl.Buffered(1)
    return pl.pallas_call(
        functools.partial(_out_mlp_kernel, f_chunk=f_chunk),
        name="out_mlp",
        out_shape=jax.ShapeDtypeStruct((t, D_MODEL), F32),
        grid=(t // tm,),
        in_specs=[
            _row_spec(tm, D_MODEL),
            _row_spec(tm, attn.shape[1]),
            pl.BlockSpec(w_out.shape, lambda i: (0, 0), pipeline_mode=single),
            _const_spec(g.shape),
            pl.BlockSpec(w_up.shape, lambda i: (0, 0), pipeline_mode=single),
            pl.BlockSpec(w_down.shape, lambda i: (0, 0), pipeline_mode=single),
        ],
        out_specs=_row_spec(tm, D_MODEL),
        compiler_params=pltpu.CompilerParams(dimension_semantics=("parallel",), vmem_limit_bytes=VMEM_LIMIT),
    )(x, attn, w_out, g, w_up, w_down)


def _swa_kernel(q_ref, kp_ref, kc_ref, vp_ref, vc_ref, bias_ref, sink_ref, o_ref):
    first_step = pl.program_id(1) == 0
    group_pairs = PAIRS // B_KV_HEADS
    w = group_pairs * BLK
    key = lax.broadcasted_iota(jnp.int32, (2 * BLK, w), 0)
    query = lax.broadcasted_iota(jnp.int32, (2 * BLK, w), 1) & (BLK - 1)
    rel = query + WINDOW - key
    in_band = (rel >= 0) & (rel < WINDOW)
    for blk in range(q_ref.shape[1]):
        k_prev, v_prev = (kp_ref, vp_ref) if blk == 0 else (kc_ref, vc_ref)
        prev = max(blk - 1, 0)
        ok = in_band & ((key >= WINDOW) | jnp.logical_not(first_step)) if blk == 0 else in_band
        for c in range(B_KV_HEADS):
            kk = jnp.concatenate([k_prev[0, prev, c * 2 * BLK:(c + 1) * 2 * BLK, :],
                                  kc_ref[0, blk, c * 2 * BLK:(c + 1) * 2 * BLK, :]], axis=0)
            s = _nt_dot(kk, q_ref[0, blk, c * w:(c + 1) * w, :])
            vt = jnp.concatenate([v_prev[0, prev, c * VT_ROWS:(c + 1) * VT_ROWS, :],
                                  vc_ref[0, blk, c * VT_ROWS:(c + 1) * VT_ROWS, :]], axis=1)
            halves = []
            for e in range(2):
                se = jnp.concatenate([s[e * BLK:(e + 1) * BLK], s[(2 + e) * BLK:(3 + e) * BLK]], axis=0)
                se = jnp.where(ok, se + bias_ref[c, e], NEG_INF)
                sink = sink_ref[c, e]
                m = jnp.maximum(jnp.max(se, axis=0, keepdims=True), sink)
                acc = _dot(vt, jnp.exp(se - m).astype(MXU_DTYPE))
                halves.append(acc[:HEAD_DIM] / (acc[HEAD_DIM:HEAD_DIM + 1] + jnp.exp(sink - m)))
            for pg in range(group_pairs):
                sl = slice(pg * BLK, (pg + 1) * BLK)
                pair = jnp.concatenate([halves[0][:, sl], halves[1][:, sl]], axis=0)
                p = c * group_pairs + pg
                o_ref[0, blk * BLK:(blk + 1) * BLK, p * LANES:(p + 1) * LANES] = (
                    jnp.transpose(pair).astype(o_ref.dtype))


def _swa_call(q, kc, vt, bias_t, sinks_t, batch, blocks_per_step=2):
    nb = q.shape[0] // batch
    g = blocks_per_step
    assert nb % g == 0
    per_batch = lambda a: a.reshape(batch, nb, *a.shape[1:])
    q, kc, vt = map(per_batch, (q, kc, vt))
    prev = lambda a: pl.BlockSpec((1, 1) + a.shape[2:], lambda bi, n: (bi, jnp.maximum(n * g - 1, 0), 0, 0))
    cur = lambda a: pl.BlockSpec((1, g) + a.shape[2:], lambda bi, n: (bi, n, 0, 0))
    return pl.pallas_call(
        _swa_kernel,
        name="swa_attn",
        out_shape=jax.ShapeDtypeStruct((batch, nb * BLK, HEADS * HEAD_DIM), MXU_DTYPE),
        grid=(batch, nb // g),
        in_specs=[cur(q), prev(kc), cur(kc), prev(vt), cur(vt),
                  pl.BlockSpec(bias_t.shape, lambda bi, n: (0, 0, 0, 0)),
                  pl.BlockSpec(sinks_t.shape, lambda bi, n: (0, 0, 0, 0))],
        out_specs=pl.BlockSpec((1, g * BLK, HEADS * HEAD_DIM), lambda bi, n: (bi, n, 0)),
        compiler_params=pltpu.CompilerParams(dimension_semantics=("parallel", "parallel"),
                                             vmem_limit_bytes=VMEM_LIMIT),
    )(q, kc, kc, vt, vt, bias_t, sinks_t)


def _pipeline_scratch(tk, n_chunks, lanes):
    return [pltpu.VMEM((2, 2, n_chunks, tk, lanes), F32), pltpu.VMEM((2, 2, n_chunks, tk, lanes), MXU_DTYPE),
            pltpu.VMEM((2, 2, n_chunks, 1, lanes), F32), pltpu.VMEM((2, n_chunks, 1, lanes), F32),
            pltpu.VMEM((2, n_chunks, VT_ROWS, lanes), F32)]


def _init_pipeline(s_scr, p_scr, alpha_scr, m_scr, acc_scr):
    m_scr[...] = jnp.full(m_scr.shape, NEG_INF, F32)
    acc_scr[...] = jnp.zeros(acc_scr.shape, F32)
    s_scr[1] = jnp.full(s_scr.shape[1:], NEG_INF, F32)
    p_scr[0] = jnp.zeros(p_scr.shape[1:], p_scr.dtype)
    alpha_scr[0] = jnp.ones(alpha_scr.shape[1:], F32)


def _softmax_stage(s_scr, p_scr, alpha_scr, m_scr, slot, chunks=(0,)):
    for e in range(2):
        for ch in chunks:
            s = s_scr[slot, e, ch]
            m_old = m_scr[e, ch]
            m_new = jnp.maximum(m_old, jnp.max(s, axis=0, keepdims=True))
            m_safe = jnp.where(m_new == NEG_INF, 0.0, m_new)
            p_scr[slot, e, ch] = jnp.exp2(s - m_safe).astype(p_scr.dtype)
            alpha_scr[slot, e, ch] = jnp.exp2(m_old - m_safe)
            m_scr[e, ch] = m_new


def _pv_stage(vts, p_scr, alpha_scr, acc_scr, slot, chunks=(0,)):
    for e in range(2):
        for ch in chunks:
            acc_scr[e, ch] = acc_scr[e, ch] * alpha_scr[slot, e, ch] + _dot(vts[e], p_scr[slot, e, ch])


def _normalized_pair(acc_scr, ch, lanes):
    a0, a1 = acc_scr[0, ch, :, lanes], acc_scr[1, ch, :, lanes]
    return jnp.concatenate([a0[:HEAD_DIM] / a0[HEAD_DIM:HEAD_DIM + 1],
                            a1[:HEAD_DIM] / a1[HEAD_DIM:HEAD_DIM + 1]], axis=0)


def _mla_kernel(q_ref, k_ref, vt_ref, o_ref, s_scr, p_scr, alpha_scr, m_scr, acc_scr, *, tq, tk):
    seq = q_ref.shape[1]
    n_diag = tq // tk
    sub = tk // BLK
    all_chunks = tuple(range(n_diag))
    below_diagonal = (lax.broadcasted_iota(jnp.int32, (tk, tk), 0)
                      <= lax.broadcasted_iota(jnp.int32, (tk, tk), 1))

    def q_tile(qt, _):
        q0 = pl.multiple_of(qt * tq, tq)
        qs = [q_ref[0, pl.ds(q0, tq), e * LANES:(e + 1) * LANES] for e in range(2)]
        n_full = qt * n_diag
        _init_pipeline(s_scr, p_scr, alpha_scr, m_scr, acc_scr)

        def vts(kt):
            kt = jnp.maximum(kt, 0)
            return [jnp.concatenate([vt_ref[0, kt * sub + c, e * VT_ROWS:(e + 1) * VT_ROWS, :]
                                     for c in range(sub)], axis=1) for e in range(2)]

        def qk_stage(kt, slot, diag):
            k0 = pl.multiple_of(kt * tk, tk)
            first = 0 if diag is None else diag
            for e in range(2):
                s = _nt_dot(k_ref[0, pl.ds(k0, tk), e * LANES:(e + 1) * LANES], qs[e][first * tk:])
                for ch in range(first, n_diag):
                    sc = s[:, (ch - first) * tk:(ch - first + 1) * tk]
                    if ch == diag:
                        sc = jnp.where(below_diagonal, sc, NEG_INF)
                    s_scr[slot, e, ch] = sc

        def chunks_of(c):
            return all_chunks[max(c, 0):]

        def full_steps(kk, carry):
            for u in range(n_diag):
                kt = n_diag * kk + u
                _pv_stage(vts(kt - 2), p_scr, alpha_scr, acc_scr, u % 2, all_chunks)
                qk_stage(kt, u % 2, None)
                _softmax_stage(s_scr, p_scr, alpha_scr, m_scr, 1 - u % 2, all_chunks)
            return carry

        lax.fori_loop(0, qt, full_steps, 0)
        for c in range(n_diag + 2):
            par = c % 2
            _pv_stage(vts(n_full + c - 2), p_scr, alpha_scr, acc_scr, par, chunks_of(c - 2))
            if c < n_diag:
                qk_stage(n_full + c, par, c)
            if c < n_diag + 1:
                _softmax_stage(s_scr, p_scr, alpha_scr, m_scr, 1 - par, chunks_of(c - 1))
        for ch in all_chunks:
            o_ref[0, pl.ds(pl.multiple_of(q0 + ch * tk, tk), tk), :] = jnp.transpose(
                _normalized_pair(acc_scr, ch, slice(None))).astype(o_ref.dtype)
        return 0

    lax.fori_loop(0, seq // tq, q_tile, 0)


def _mla_call(q, k, vt, tq=1024, tk=256):
    b, l, _ = q.shape
    tq, tk = min(tq, l), min(tk, l)
    assert (tq // tk) % 2 == 0 and l % tq == 0
    nb = l // BLK
    vt = vt.reshape(b, nb, HEADS * VT_ROWS, LANES)
    return pl.pallas_call(
        functools.partial(_mla_kernel, tq=tq, tk=tk),
        name="mla_attn",
        out_shape=jax.ShapeDtypeStruct((b, l, HEADS * HEAD_DIM), MXU_DTYPE),
        grid=(b, PAIRS),
        in_specs=[
            pl.BlockSpec((1, l, 2 * LANES), lambda bi, p: (bi, 0, p)),
            pl.BlockSpec((1, l, 2 * LANES), lambda bi, p: (bi, 0, p)),
            pl.BlockSpec((1, nb, 2 * VT_ROWS, LANES), lambda bi, p: (bi, 0, p, 0)),
        ],
        out_specs=pl.BlockSpec((1, l, LANES), lambda bi, p: (bi, 0, p)),
        scratch_shapes=_pipeline_scratch(tk, tq // tk, tk),
        compiler_params=pltpu.CompilerParams(dimension_semantics=("parallel", "parallel"),
                                             vmem_limit_bytes=VMEM_LIMIT),
    )(q, k, vt)


def _dsa_kernel(q_ref, qi_ref, wi_ref, kc_ref, vt_ref, kic_ref, bias_ref, tri_ref, o_ref,
                idx_scr, hi_scr, lo_scr, s_scr, p_scr, alpha_scr, m_scr, acc_scr, *, k_sel):
    i = pl.program_id(1)
    n_tiles = i + 1
    n_groups = i // TILE_GROUP + 1
    key_row = lax.broadcasted_iota(jnp.int32, (BLK, BLK), 0)
    query_col = lax.broadcasted_iota(jnp.int32, (BLK, BLK), 1)

    def admissible(j):
        return (j < i) | ((j == i) & (key_row <= query_col))

    wt = jnp.transpose(wi_ref[0])
    qi = qi_ref[0, 0]

    def score_group(g, carry):
        for u in range(TILE_GROUP):
            j = g * TILE_GROUP + u
            d = jnp.maximum(_nt_dot(kic_ref[0, j], qi), 0.0)
            sc = jnp.zeros((BLK, BLK), F32)
            for hp in range(IDX_HEADS // 2):
                for e in range(2):
                    h = 2 * hp + e
                    sc = sc + wt[h:h + 1, :] * d[e * BLK:(e + 1) * BLK, hp * BLK:(hp + 1) * BLK]
            sc = jnp.where(admissible(j), sc, NEG_INF)
            idx_scr[j] = sc
            bits = lax.bitcast_convert_type(sc, jnp.int32)
            key = jnp.where(bits >= 0, bits, bits ^ jnp.int32(0x7FFFFFFF))
            hi_scr[j] = lax.shift_right_arithmetic(key, 16).astype(jnp.int16)
            lo_scr[j] = ((key & 0xFFFF) - HALF_RANGE).astype(jnp.int16)
        return carry

    lax.fori_loop(0, n_groups, score_group, 0)

    def count(pred):
        def body(g, c):
            for u in range(TILE_GROUP):
                c = c + jnp.where(pred(idx_scr[g * TILE_GROUP + u]), 1.0, 0.0)
            return c
        c = lax.fori_loop(0, n_groups, body, jnp.zeros((BLK, BLK), F32))
        return jnp.sum(c, axis=0, keepdims=True)

    n_blocks = idx_scr.shape[0]
    step = min(SEARCH_STEP, n_blocks)
    sizes = tuple(range(step, n_blocks + 1, step))
    lowest = jnp.full((BLK, BLK), -HALF_RANGE, jnp.int16)
    for u in range(step - TILE_GROUP):
        hi_scr[n_groups * TILE_GROUP + u] = lowest
        lo_scr[n_groups * TILE_GROUP + u] = lowest

    def count16(ref, cand, n_static):
        cand16 = cand.astype(jnp.int16)
        c = jnp.zeros((BLK, BLK), jnp.int16)
        for j in range(n_static):
            c = c + jnp.where(ref[j] >= cand16, jnp.int16(1), jnp.int16(0))
        return jnp.sum(c.astype(jnp.int32), axis=0, keepdims=True)

    def search16(ref, need, n_static):
        def bit_step(it, v):
            cand_v = v | lax.shift_left(jnp.int32(1), 15 - it)
            return jnp.where(count16(ref, cand_v - HALF_RANGE, n_static) >= need, cand_v, v)
        return lax.fori_loop(0, 16, bit_step, jnp.zeros((1, BLK), jnp.int32))

    def search(n_static):
        t_hi = search16(hi_scr, k_sel, n_static) - HALF_RANGE
        above = jnp.where(t_hi == HALF_RANGE - 1, 0,
                          count16(hi_scr, jnp.minimum(t_hi + 1, HALF_RANGE - 1), n_static))
        t_hi16 = t_hi.astype(jnp.int16)
        for j in range(n_static):
            lo_scr[j] = jnp.where(hi_scr[j] == t_hi16, lo_scr[j], jnp.int16(-HALF_RANGE))
        t_lo = search16(lo_scr, k_sel - above, n_static)
        key = lax.shift_left(t_hi, 16) | t_lo
        bits = jnp.where(key >= 0, key, key ^ jnp.int32(0x7FFFFFFF))
        t = lax.bitcast_convert_type(bits, F32)
        return jnp.where(t != t, NEG_INF, t)

    def run_search():
        which = (n_groups * TILE_GROUP + step - 1) // step - 1
        return lax.switch(which, [functools.partial(search, n) for n in sizes])

    thr = lax.cond(n_tiles * BLK > k_sel, run_search, lambda: jnp.full((1, BLK), NEG_INF, F32))

    n_gt = count(lambda s: s > thr)
    n_ge = count(lambda s: s >= thr)
    tied = (n_ge > k_sel) & (thr > NEG_INF)

    @pl.when(jnp.max(jnp.where(tied, 1.0, 0.0)) > 0.0)
    def _():
        need = k_sel - n_gt

        def tie_tile(j, seen):
            s = idx_scr[j]
            eq = s == thr
            eq_f = jnp.where(eq, 1.0, 0.0).astype(MXU_DTYPE)
            rank = seen + _dot(tri_ref[...], eq_f)
            idx_scr[j] = jnp.where(eq & tied & (rank > need), NEG_INF, s)
            return seen + jnp.sum(eq_f.astype(F32), axis=0, keepdims=True)

        lax.fori_loop(0, n_tiles, tie_tile, jnp.zeros((1, BLK), F32))

    q = q_ref[0, 0]
    _init_pipeline(s_scr, p_scr, alpha_scr, m_scr, acc_scr)

    def qk_stage(j, slot, near, valid=None):
        sel = idx_scr[j] >= thr
        if near is not None:
            sel = sel & admissible(j)
        if valid is not None:
            sel = sel & valid
        mask = jnp.where(sel, 0.0, NEG_INF)
        mask = jnp.concatenate([mask] * PAIRS, axis=1)
        s = _nt_dot(kc_ref[0, j], q)
        if near is not None:
            s = s + bias_ref[near]
        for e in range(2):
            s_scr[slot, e, 0] = s[e * BLK:(e + 1) * BLK] + mask

    def vts(j):
        vt = vt_ref[0, jnp.maximum(j, 0)]
        return [vt, vt]

    n_far = jnp.maximum(i - 1, 0)
    n_trips = (n_far + TILE_GROUP - 1) // TILE_GROUP

    def far_steps(kk, carry):
        for u in range(TILE_GROUP):
            j = TILE_GROUP * kk + u
            _pv_stage(vts(j - 2), p_scr, alpha_scr, acc_scr, u % 2)
            qk_stage(jnp.minimum(j, i), u % 2, None, j < n_far)
            _softmax_stage(s_scr, p_scr, alpha_scr, m_scr, 1 - u % 2)
        return carry

    lax.fori_loop(0, n_trips, far_steps, 0)
    j0 = TILE_GROUP * n_trips
    tail = [(n_far, 0, i >= 1), (i, 1, None)]
    done = [jnp.minimum(j0 - 2, i), jnp.minimum(j0 - 1, i)] + [t[0] for t in tail]
    for c in range(len(tail) + 2):
        par = c % 2
        _pv_stage(vts(done[c]), p_scr, alpha_scr, acc_scr, par)
        if c < len(tail):
            qk_stage(tail[c][0], par, tail[c][1], tail[c][2])
        if c < len(tail) + 1:
            _softmax_stage(s_scr, p_scr, alpha_scr, m_scr, 1 - par)

    for p in range(PAIRS):
        sl = slice(p * BLK, (p + 1) * BLK)
        o_ref[0, :, sl] = jnp.transpose(_normalized_pair(acc_scr, 0, sl)).astype(o_ref.dtype)


def _dsa_call(q, qi, wi, kc, vt, kic, bias_t, tri, batch):
    nb = q.shape[0] // batch
    l = nb * BLK
    k_sel = min(TOPK_MAX, l // 4)
    assert nb % TILE_GROUP == 0
    per_batch =lambda a: a.reshape(batch, nb, *a.shape[1:])
    q, qi, kc, vt, kic = map(per_batch, (q, qi, kc, vt, kic))
    blk = lambda a: pl.BlockSpec((1, 1) + a.shape[2:], lambda bi, n: (bi, n, 0, 0))
    seq = lambda a: pl.BlockSpec((1,) + a.shape[1:], lambda bi, n: (bi, 0, 0, 0))
    return pl.pallas_call(
        functools.partial(_dsa_kernel, k_sel=k_sel),
        name="dsa_attn",
        out_shape=jax.ShapeDtypeStruct((batch, l, HEADS * HEAD_DIM), MXU_DTYPE),
        grid=(batch, nb),
        in_specs=[
            blk(q),
            blk(qi),
            pl.BlockSpec((1, BLK, LANES), lambda bi, n: (bi, n, 0)),
            seq(kc),
            seq(vt),
            seq(kic),
            pl.BlockSpec(bias_t.shape, lambda bi, n: (0, 0, 0)),
            pl.BlockSpec(tri.shape, lambda bi, n: (0, 0)),
        ],
        out_specs=pl.BlockSpec((1, BLK, HEADS * HEAD_DIM), lambda bi, n: (bi, n, 0)),
        scratch_shapes=[pltpu.VMEM((nb, BLK, BLK), F32), pltpu.VMEM((nb + SEARCH_STEP, BLK, BLK), jnp.int16),
                        pltpu.VMEM((nb + SEARCH_STEP, BLK, BLK), jnp.int16)]
        + _pipeline_scratch(BLK, 1, PAIRS * BLK),
        compiler_params=pltpu.CompilerParams(dimension_semantics=("parallel", "arbitrary"),
                                             vmem_limit_bytes=VMEM_LIMIT),
    )(q, qi, wi.reshape(batch, l, LANES), kc, vt, kic, bias_t, tri)


def _t5_bucket(rel):
    n = jnp.maximum(rel, 0)
    max_exact = N_BUCKETS // 2
    nf = jnp.maximum(n, 1).astype(F32)
    large = max_exact + (jnp.log(nf / max_exact) / math.log(MAX_DISTANCE / max_exact)
                         * (N_BUCKETS - max_exact)).astype(jnp.int32)
    large = jnp.minimum(large, N_BUCKETS - 1)
    return jnp.where(n < max_exact, n, large)


def _band_bias(rel_bias):
    qi = jnp.arange(BLK, dtype=jnp.int32)[:, None] + BLK
    kj = jnp.arange(2 * BLK, dtype=jnp.int32)[None, :]
    one_hot = (_t5_bucket(qi - kj)[None] == jnp.arange(N_BUCKETS, dtype=jnp.int32)[:, None, None]).astype(F32)
    return jnp.einsum("nh,nqk->hqk", rel_bias, one_hot, precision=lax.Precision.HIGHEST)


def _pair_gain(g, scale=1.0):
    return (jnp.concatenate([g, g]) * scale).reshape(1, LANES).astype(F32)


def _dup(w, n_heads):
    d = w.shape[0]
    w = w.reshape(d, n_heads, 1, HEAD_DIM)
    return jnp.broadcast_to(w, (d, n_heads, 2, HEAD_DIM)).reshape(d, n_heads * LANES)


def _layer_a(x, g_mix, w_in, q_gain, k_gain):
    nq = HEADS * HEAD_DIM
    sizes = [nq, HEAD_DIM, HEAD_DIM, IDX_HEADS * IDX_DIM, IDX_DIM, IDX_HEADS]
    wq, wk, wv, wqi, wki, wwi = jnp.split(w_in, np.cumsum(sizes)[:-1].tolist(), axis=1)
    pad_to_lanes = lambda a: jnp.pad(a, ((0, 0), (0, LANES - a.shape[1])))
    w = jnp.concatenate([wq, _dup(wk, 1), pad_to_lanes(wv), wqi, _dup(wki, 1), pad_to_lanes(wwi)],
                        axis=1).astype(MXU_DTYPE)
    consts = [g_mix.reshape(1, -1), w, _pair_gain(q_gain, HEAD_DIM ** -0.5 * LOG2_E), _pair_gain(k_gain)]
    outs = [("stacked", PAIRS * BLK, MXU_DTYPE), ("stacked", 2 * BLK, MXU_DTYPE), ("stacked", VT_ROWS, MXU_DTYPE),
            ("stacked", IDX_HEADS // 2 * BLK, MXU_DTYPE), ("stacked", 2 * BLK, MXU_DTYPE), ("rows", LANES, F32)]
    return _proj_call(_proj_a_kernel, x, consts, [], outs)


def _layer_b(x, g_mix, w_in, q_gain, k_gain):
    nq = HEADS * HEAD_DIM
    nkv = B_KV_HEADS * HEAD_DIM
    wq, wk, wv = jnp.split(w_in, [nq, nq + nkv], axis=1)
    wv = jnp.pad(wv.reshape(-1, B_KV_HEADS, HEAD_DIM), ((0, 0), (0, 0), (0, LANES - HEAD_DIM)))
    w = jnp.concatenate([wq, _dup(wk, B_KV_HEADS), wv.reshape(-1, B_KV_HEADS * LANES)], axis=1).astype(MXU_DTYPE)
    consts = [g_mix.reshape(1, -1), w, _pair_gain(q_gain, HEAD_DIM ** -0.5), _pair_gain(k_gain)]
    outs = [("stacked", PAIRS * BLK, MXU_DTYPE), ("stacked", B_KV_HEADS * 2 * BLK, MXU_DTYPE),
            ("stacked", B_KV_HEADS * VT_ROWS, MXU_DTYPE)]
    return _proj_call(_proj_b_kernel, x, consts, [], outs)


_half = C_ROPE // 2
MLA_LANE = np.concatenate([np.arange(LANES // 2 - _half), LANES // 2 + np.arange(C_NOPE - (LANES // 2 - _half)),
                           LANES // 2 - _half + np.arange(_half), LANES - _half + np.arange(_half)])


def _to_mla_lanes(a):
    n = a.shape[-1]
    lanes = MLA_LANE[:n] if n != C_ROPE else MLA_LANE[C_NOPE:]
    return jnp.zeros(a.shape[:-1] + (LANES,), a.dtype).at[..., lanes].set(a)


def _rope_tables(cos, sin, gain):
    ones = jnp.ones(cos.shape[:-1] + (C_NOPE,), F32)
    own = _to_mla_lanes(jnp.concatenate([ones, cos, cos], axis=-1))
    partner = _to_mla_lanes(jnp.concatenate([-sin, sin], axis=-1))
    g = _to_mla_lanes(gain.astype(F32))
    return own * g, partner * jnp.roll(g, LANES // 2)


def _layer_c(x, g_mix, w_in, q_a_gain, w_q_b, kv_a_gain, w_kv_b, tables):
    d = w_in.shape[0]
    heads = lambda w, n: _to_mla_lanes(w.reshape(w.shape[0], HEADS, n)).reshape(w.shape[0], HEADS * LANES)
    w_in_p = jnp.concatenate([w_in[:, :C_Q_RANK + C_KV_RANK], _to_mla_lanes(w_in[:, C_Q_RANK + C_KV_RANK:])],
                             axis=1).astype(MXU_DTYPE)
    wq = heads(w_q_b, C_QK).astype(MXU_DTYPE)
    w_kv = w_kv_b.reshape(C_KV_RANK, HEADS, C_NOPE + HEAD_DIM)
    wk = heads(w_kv[:, :, :C_NOPE].reshape(C_KV_RANK, -1), C_NOPE).astype(MXU_DTYPE)
    wv = w_kv[:, :, C_NOPE:].reshape(C_KV_RANK, -1).astype(MXU_DTYPE)
    consts = [g_mix.reshape(1, -1), w_in_p, q_a_gain.reshape(1, -1), kv_a_gain.reshape(1, -1), wq, wk, wv]
    outs = [("rows", HEADS * LANES, MXU_DTYPE), ("rows", HEADS * LANES, MXU_DTYPE),
            ("stacked", HEADS * VT_ROWS, MXU_DTYPE)]
    return _proj_call(_proj_c_kernel, x, consts, list(tables), outs)


def kernel(x, positions, rel_bias, norm_mix, norm_mlp, w_up, w_down, a_w_in, a_q_gain, a_k_gain, a_w_out,
           b_w_in, b_q_gain, b_k_gain, b_sinks, b_w_out, c_w_in, c_q_a_gain, c_w_q_b, c_kv_a_gain, c_w_kv_b,
           c_q_gain, c_k_gain, c_w_out):
    b, l, d = x.shape
    t = b * l
    depth = norm_mix.shape[0]

    band = _band_bias(rel_bias.astype(F32))
    far = rel_bias[N_BUCKETS - 1].astype(F32)[:, None, None]
    bias_t = jnp.transpose(((band - far) * LOG2_E).reshape(PAIRS, 2, BLK, 2, BLK), (3, 1, 4, 0, 2)).reshape(
        2, 2 * BLK, PAIRS * BLK)
    tri = jnp.asarray(np.arange(BLK)[:, None] >= np.arange(BLK)[None, :], MXU_DTYPE)

    inv_freq = ROPE_THETA ** (-jnp.arange(0, C_ROPE, 2, dtype=F32) / C_ROPE)
    ang = positions.astype(F32).reshape(t, 1) * inv_freq
    cos, sin = jnp.cos(ang), jnp.sin(ang)

    xf = x.reshape(t, d).astype(F32)
    ia = ib = ic = 0
    for i in range(depth):
        kind = i % 3
        g_mix = norm_mix[i]
        if kind == 0:
            q, kc, vt, qi, kic, wi = _layer_a(xf, g_mix, a_w_in[ia], a_q_gain[ia], a_k_gain[ia])
            attn = _dsa_call(q, qi, wi, kc, vt, kic, bias_t, tri, b)
            w_out = a_w_out[ia]
            ia += 1
        elif kind == 1:
            q, kc, vt = _layer_b(xf, g_mix, b_w_in[ib], b_q_gain[ib], b_k_gain[ib])
            group_pairs = PAIRS // B_KV_HEADS
            swa_bias = jnp.transpose(band.reshape(B_KV_HEADS, group_pairs, 2, BLK, 2 * BLK),
                                     (0, 2, 4, 1, 3)).reshape(B_KV_HEADS, 2, 2 * BLK, group_pairs * BLK)
            sinks = jnp.transpose(b_sinks[ib].astype(F32).reshape(B_KV_HEADS, group_pairs, 2), (0, 2, 1))
            sinks = jnp.broadcast_to(sinks[..., None], (B_KV_HEADS, 2, group_pairs, BLK)).reshape(
                B_KV_HEADS, 2, 1, group_pairs * BLK)
            attn = _swa_call(q, kc, vt, swa_bias, sinks, b)
            w_out = b_w_out[ib]
            ib += 1
        else:
            tables = (_rope_tables(cos, sin, c_q_gain[ic] * (C_QK ** -0.5 * LOG2_E))
                      + _rope_tables(cos, sin, c_k_gain[ic]))
            q, k, vt = _layer_c(xf, g_mix, c_w_in[ic], c_q_a_gain[ic], c_w_q_b[ic], c_kv_a_gain[ic],
                               c_w_kv_b[ic], tables)
            r3 = lambda a: a.reshape(b, l, a.shape[-1])
            attn = _mla_call(r3(q), r3(k), vt)
            w_out = c_w_out[ic]
            ic += 1
        xf = _out_mlp_call(xf, attn.reshape(t, -1), w_out.astype(MXU_DTYPE), norm_mlp[i].reshape(1, -1),
                           w_up[i].astype(MXU_DTYPE), w_down[i].astype(MXU_DTYPE))
    return xf.reshape(b, l, d).astype(x.dtype)
```

```python
import functools
import math

import numpy as np
import jax
import jax.numpy as jnp
from jax import lax
from jax.experimental import pallas as pl
from jax.experimental.pallas import tpu as pltpu

F32 = jnp.float32
MXU_DTYPE = jnp.bfloat16

D_MODEL = 1024
D_FF = 4 * D_MODEL
RMS_EPS = 1e-6
N_BUCKETS = 32
MAX_DISTANCE = 128
HEADS = 16
HEAD_DIM = 64
PAIRS = HEADS // 2
LANES = 128
BLK = 128
IDX_HEADS = 8
IDX_DIM = 64
TOPK_MAX = 256
B_KV_HEADS = 4
WINDOW = 128
C_Q_RANK = 256
C_KV_RANK = 128
C_NOPE = 64
C_ROPE = 32
C_QK = C_NOPE + C_ROPE
ROPE_THETA = 10000.0
NEG_INF = float("-inf")
LOG2_E = math.log2(math.e)
HALF_RANGE = 1 << 15
TILE_GROUP = 4
SEARCH_STEP = 2 * TILE_GROUP
VT_ROWS = HEAD_DIM + 16
VMEM_LIMIT = 56 * 1024 * 1024


def _nt_dot(a, b):
    return lax.dot_general(a, b, (((1,), (1,)), ((), ())), preferred_element_type=F32)


def _dot(a, b):
    return jnp.dot(a, b, preferred_element_type=F32)


def _rms(x, g):
    ms = jnp.mean(x * x, axis=-1, keepdims=True)
    return x * lax.rsqrt(ms + RMS_EPS) * g


def _lane_iota(shape):
    return lax.broadcasted_iota(jnp.int32, shape, len(shape) - 1)


def _pair_rms(y, gain):
    left = _lane_iota(y.shape) < HEAD_DIM
    y2 = y * y
    s_left = jnp.sum(jnp.where(left, y2, 0.0), axis=-1, keepdims=True)
    s_right = jnp.sum(y2, axis=-1, keepdims=True) - s_left
    r = jnp.where(left, lax.rsqrt(s_left / HEAD_DIM + RMS_EPS), lax.rsqrt(s_right / HEAD_DIM + RMS_EPS))
    return y * r * gain


def _split_kv(t, parity, ones_lane=None):
    lane = _lane_iota(t.shape)
    keep = (lane < HEAD_DIM) if parity == 0 else (lane >= HEAD_DIM)
    out = jnp.where(keep, t, jnp.zeros_like(t))
    if ones_lane is not None:
        out = jnp.where(lane == ones_lane, jnp.ones_like(t), out)
    return out


def _put_blocks(out_ref, slot, tile):
    for bk in range(tile.shape[0] // BLK):
        out_ref[bk, slot * BLK:(slot + 1) * BLK, :] = tile[bk * BLK:(bk + 1) * BLK].astype(out_ref.dtype)


def _two_tiles(xn, w_ref, o):
    y = _dot(xn, w_ref[:, o:o + 2 * LANES])
    return y[:, :LANES], y[:, LANES:]


def _proj_a_kernel(x_ref, g_ref, w_ref, gq_ref, gk_ref, q_out, kc_out, vt_out, qi_out, kic_out, wi_out):
    xn = _rms(x_ref[...], g_ref[...]).astype(MXU_DTYPE)
    two = functools.partial(_two_tiles, xn, w_ref)
    for p in range(0, PAIRS, 2):
        qa, qb = two(p * LANES)
        _put_blocks(q_out, p, _pair_rms(qa, gq_ref[...]))
        _put_blocks(q_out, p + 1, _pair_rms(qb, gq_ref[...]))
    o = HEADS * HEAD_DIM
    k, v = two(o)
    k = _pair_rms(k, gk_ref[...])
    _put_blocks(kc_out, 0, _split_kv(k, 0))
    _put_blocks(kc_out, 1, _split_kv(k, 1))
    _put_vt(vt_out, 0, v)
    o += 2 * LANES
    for hp in range(0, IDX_HEADS // 2, 2):
        qia, qib = two(o)
        _put_blocks(qi_out, hp, qia * IDX_DIM ** -0.5)
        _put_blocks(qi_out, hp + 1, qib * IDX_DIM ** -0.5)
        o += 2 * LANES
    ki, wi = two(o)
    _put_blocks(kic_out, 0, _split_kv(ki, 0))
    _put_blocks(kic_out, 1, _split_kv(ki, 1))
    wi_out[...] = wi * IDX_HEADS ** -0.5


def _put_vt(vt_out, slot, v):
    vt = jnp.transpose(v)
    row = lax.broadcasted_iota(jnp.int32, vt.shape, 0)
    vt = jnp.where(row == HEAD_DIM, 1.0, vt)
    for bk in range(vt.shape[1] // BLK):
        vt_out[bk, slot * VT_ROWS:(slot + 1) * VT_ROWS, :] = (
            vt[:VT_ROWS, bk * BLK:(bk + 1) * BLK].astype(vt_out.dtype))


def _proj_b_kernel(x_ref, g_ref, w_ref, gq_ref, gk_ref, q_out, kc_out, vt_out):
    xn = _rms(x_ref[...], g_ref[...]).astype(MXU_DTYPE)
    two = functools.partial(_two_tiles, xn, w_ref)
    for p in range(0, PAIRS, 2):
        qa, qb = two(p * LANES)
        _put_blocks(q_out, p, _pair_rms(qa, gq_ref[...]))
        _put_blocks(q_out, p + 1, _pair_rms(qb, gq_ref[...]))
    o = HEADS * HEAD_DIM
    for c in range(0, B_KV_HEADS, 2):
        for c1, k in zip((c, c + 1), two(o + c * LANES)):
            k = _pair_rms(k, gk_ref[...])
            _put_blocks(kc_out, 2 * c1, _split_kv(k, 0))
            _put_blocks(kc_out, 2 * c1 + 1, _split_kv(k, 1))
    o += B_KV_HEADS * LANES
    for c in range(0, B_KV_HEADS, 2):
        for c1, v in zip((c, c + 1), two(o + c * LANES)):
            _put_vt(vt_out, c1, v)


def _norm_rope(y, own, partner):
    y = y * lax.rsqrt(jnp.sum(y * y, axis=-1, keepdims=True) / C_QK + RMS_EPS)
    return y * own + pltpu.roll(y, LANES // 2, 1) * partner


def _proj_c_kernel(x_ref, g_ref, w_in_ref, gqa_ref, gkva_ref, wq_ref, wk_ref, wv_ref,
                   q_own_ref, q_partner_ref, k_own_ref, k_partner_ref, q_out, k_out, vt_out):
    xn = _rms(x_ref[...], g_ref[...]).astype(MXU_DTYPE)
    lat = _dot(xn, w_in_ref[...])
    qa = _rms(lat[:, :C_Q_RANK], gqa_ref[...]).astype(MXU_DTYPE)
    kva = _rms(lat[:, C_Q_RANK:C_Q_RANK + C_KV_RANK], gkva_ref[...]).astype(MXU_DTYPE)
    k_rope = lat[:, C_Q_RANK + C_KV_RANK:]
    q_own, q_partner = q_own_ref[...], q_partner_ref[...]
    k_own, k_partner = k_own_ref[...], k_partner_ref[...]
    v = _dot(kva, wv_ref[...])
    tm = v.shape[0]
    ones_row = jnp.where(lax.broadcasted_iota(jnp.int32, (VT_ROWS - HEAD_DIM, tm), 0) == 0, 1.0, 0.0)
    for p in range(PAIRS):
        v_t = jnp.transpose(v[:, p * LANES:(p + 1) * LANES])
        for e in range(2):
            h = 2 * p + e
            head_t = jnp.concatenate([v_t[e * HEAD_DIM:(e + 1) * HEAD_DIM], ones_row], axis=0)
            for bk in range(tm // BLK):
                vt_out[bk, h * VT_ROWS:(h + 1) * VT_ROWS, :] = (
                    head_t[:, bk * BLK:(bk + 1) * BLK].astype(vt_out.dtype))
    for h0 in range(0, HEADS, 2):
        q2 = _dot(qa, wq_ref[:, h0 * LANES:(h0 + 2) * LANES])
        k2 = _dot(kva, wk_ref[:, h0 * LANES:(h0 + 2) * LANES])
        for e in range(2):
            sl = slice((h0 + e) * LANES, (h0 + e + 1) * LANES)
            q_out[:, sl] = _norm_rope(q2[:, e * LANES:(e + 1) * LANES], q_own, q_partner).astype(q_out.dtype)
            k_out[:, sl] = _norm_rope(k2[:, e * LANES:(e + 1) * LANES] + k_rope,
                                      k_own, k_partner).astype(k_out.dtype)


def _row_spec(tm, n):
    return pl.BlockSpec((tm, n), lambda i: (i, 0))


def _const_spec(shape):
    return pl.BlockSpec(shape, lambda i: (0,) * len(shape))


def _proj_call(kernel, x, consts, row_inputs, outs, tm=512):
    t = x.shape[0]
    in_specs = [_row_spec(tm, x.shape[1])] + [_const_spec(c.shape) for c in consts]
    in_specs += [_row_spec(tm, r.shape[1]) for r in row_inputs]
    out_shape, out_specs = [], []
    for kind, n, dtype in outs:
        if kind == "rows":
            out_shape.append(jax.ShapeDtypeStruct((t, n), dtype))
            out_specs.append(_row_spec(tm, n))
        else:
            out_shape.append(jax.ShapeDtypeStruct((t // BLK, n, LANES), dtype))
            out_specs.append(pl.BlockSpec((tm // BLK, n, LANES), lambda i: (i, 0, 0)))
    return pl.pallas_call(
        kernel,
        name=kernel.__name__.strip("_"),
        out_shape=out_shape,
        grid=(t // tm,),
        in_specs=in_specs,
        out_specs=out_specs,
        compiler_params=pltpu.CompilerParams(dimension_semantics=("parallel",), vmem_limit_bytes=VMEM_LIMIT),
    )(x, *consts, *row_inputs)


def _out_mlp_kernel(x_ref, a_ref, wo_ref, g_ref, wu_ref, wd_ref, o_ref, *, f_chunk):
    x1 = x_ref[...] + _dot(a_ref[...], wo_ref[...])
    h = _rms(x1, g_ref[...]).astype(MXU_DTYPE)
    acc = x1
    for c in range(D_FF // f_chunk):
        u = jnp.maximum(_dot(h, wu_ref[:, c * f_chunk:(c + 1) * f_chunk]), 0.0)
        acc = acc + _dot((u * u).astype(MXU_DTYPE), wd_ref[c * f_chunk:(c + 1) * f_chunk, :])
    o_ref[...] = acc


def _out_mlp_call(x, attn, w_out, g, w_up, w_down, tm=512, f_chunk=1024):
    t = x.shape[0]
    single = pl.Buffered(1)
    return pl.pallas_call(
        functools.partial(_out_mlp_kernel, f_chunk=f_chunk),
        name="out_mlp",
        out_shape=jax.ShapeDtypeStruct((t, D_MODEL), F32),
        grid=(t // tm,),
        in_specs=[
            _row_spec(tm, D_MODEL),
            _row_spec(tm, attn.shape[1]),
            pl.BlockSpec(w_out.shape, lambda i: (0, 0), pipeline_mode=single),
            _const_spec(g.shape),
            pl.BlockSpec(w_up.shape, lambda i: (0, 0), pipeline_mode=single),
            pl.BlockSpec(w_down.shape, lambda i: (0, 0), pipeline_mode=single),
        ],
        out_specs=_row_spec(tm, D_MODEL),
        compiler_params=pltpu.CompilerParams(dimension_semantics=("parallel",), vmem_limit_bytes=VMEM_LIMIT),
    )(x, attn, w_out, g, w_up, w_down)


def _swa_kernel(q_ref, kp_ref, kc_ref, vp_ref, vc_ref, bias_ref, sink_ref, o_ref):
    first_step = pl.program_id(1) == 0
    group_pairs = PAIRS // B_KV_HEADS
    w = group_pairs * BLK
    key = lax.broadcasted_iota(jnp.int32, (2 * BLK, w), 0)
    query = lax.broadcasted_iota(jnp.int32, (2 * BLK, w), 1) & (BLK - 1)
    rel = query + WINDOW - key
    in_band = (rel >= 0) & (rel < WINDOW)
    for blk in range(q_ref.shape[1]):
        k_prev, v_prev = (kp_ref, vp_ref) if blk == 0 else (kc_ref, vc_ref)
        prev = max(blk - 1, 0)
        ok = in_band & ((key >= WINDOW) | jnp.logical_not(first_step)) if blk == 0 else in_band
        for c in range(B_KV_HEADS):
            kk = jnp.concatenate([k_prev[0, prev, c * 2 * BLK:(c + 1) * 2 * BLK, :],
                                  kc_ref[0, blk, c * 2 * BLK:(c + 1) * 2 * BLK, :]], axis=0)
            s = _nt_dot(kk, q_ref[0, blk, c * w:(c + 1) * w, :])
            vt = jnp.concatenate([v_prev[0, prev, c * VT_ROWS:(c + 1) * VT_ROWS, :],
                                  vc_ref[0, blk, c * VT_ROWS:(c + 1) * VT_ROWS, :]], axis=1)
            halves = []
            for e in range(2):
                se = jnp.concatenate([s[e * BLK:(e + 1) * BLK], s[(2 + e) * BLK:(3 + e) * BLK]], axis=0)
                se = jnp.where(ok, se + bias_ref[c, e], NEG_INF)
                sink = sink_ref[c, e]
                m = jnp.maximum(jnp.max(se, axis=0, keepdims=True), sink)
                acc = _dot(vt, jnp.exp(se - m).astype(MXU_DTYPE))
                halves.append(acc[:HEAD_DIM] / (acc[HEAD_DIM:HEAD_DIM + 1] + jnp.exp(sink - m)))
            for pg in range(group_pairs):
                sl = slice(pg * BLK, (pg + 1) * BLK)
                pair = jnp.concatenate([halves[0][:, sl], halves[1][:, sl]], axis=0)
                p = c * group_pairs + pg
                o_ref[0, blk * BLK:(blk + 1) * BLK, p * LANES:(p + 1) * LANES] = (
                    jnp.transpose(pair).astype(o_ref.dtype))


def _swa_call(q, kc, vt, bias_t, sinks_t, batch, blocks_per_step=2):
    nb = q.shape[0] // batch
    g = blocks_per_step
    assert nb % g == 0
    per_batch = lambda a: a.reshape(batch, nb, *a.shape[1:])
    q, kc, vt = map(per_batch, (q, kc, vt))
    prev = lambda a: pl.BlockSpec((1, 1) + a.shape[2:], lambda bi, n: (bi, jnp.maximum(n * g - 1, 0), 0, 0))
    cur = lambda a: pl.BlockSpec((1, g) + a.shape[2:], lambda bi, n: (bi, n, 0, 0))
    return pl.pallas_call(
        _swa_kernel,
        name="swa_attn",
        out_shape=jax.ShapeDtypeStruct((batch, nb * BLK, HEADS * HEAD_DIM), MXU_DTYPE),
        grid=(batch, nb // g),
        in_specs=[cur(q), prev(kc), cur(kc), prev(vt), cur(vt),
                  pl.BlockSpec(bias_t.shape, lambda bi, n: (0, 0, 0, 0)),
                  pl.BlockSpec(sinks_t.shape, lambda bi, n: (0, 0, 0, 0))],
        out_specs=pl.BlockSpec((1, g * BLK, HEADS * HEAD_DIM), lambda bi, n: (bi, n, 0)),
        compiler_params=pltpu.CompilerParams(dimension_semantics=("parallel", "parallel"),
                                             vmem_limit_bytes=VMEM_LIMIT),
    )(q, kc, kc, vt, vt, bias_t, sinks_t)


def _pipeline_scratch(tk, n_chunks, lanes):
    return [pltpu.VMEM((2, 2, n_chunks, tk, lanes), F32), pltpu.VMEM((2, 2, n_chunks, tk, lanes), MXU_DTYPE),
            pltpu.VMEM((2, 2, n_chunks, 1, lanes), F32), pltpu.VMEM((2, n_chunks, 1, lanes), F32),
            pltpu.VMEM((2, n_chunks, VT_ROWS, lanes), F32)]


def _init_pipeline(s_scr, p_scr, alpha_scr, m_scr, acc_scr):
    m_scr[...] = jnp.full(m_scr.shape, NEG_INF, F32)
    acc_scr[...] = jnp.zeros(acc_scr.shape, F32)
    s_scr[1] = jnp.full(s_scr.shape[1:], NEG_INF, F32)
    p_scr[0] = jnp.zeros(p_scr.shape[1:], p_scr.dtype)
    alpha_scr[0] = jnp.ones(alpha_scr.shape[1:], F32)


def _softmax_stage(s_scr, p_scr, alpha_scr, m_scr, slot, chunks=(0,)):
    for e in range(2):
        for ch in chunks:
            s = s_scr[slot, e, ch]
            m_old = m_scr[e, ch]
            m_new = jnp.maximum(m_old, jnp.max(s, axis=0, keepdims=True))
            m_safe = jnp.where(m_new == NEG_INF, 0.0, m_new)
            p_scr[slot, e, ch] = jnp.exp2(s - m_safe).astype(p_scr.dtype)
            alpha_scr[slot, e, ch] = jnp.exp2(m_old - m_safe)
            m_scr[e, ch] = m_new


def _pv_stage(vts, p_scr, alpha_scr, acc_scr, slot, chunks=(0,)):
    for e in range(2):
        for ch in chunks:
            acc_scr[e, ch] = acc_scr[e, ch] * alpha_scr[slot, e, ch] + _dot(vts[e], p_scr[slot, e, ch])


def _normalized_pair(acc_scr, ch, lanes):
    a0, a1 = acc_scr[0, ch, :, lanes], acc_scr[1, ch, :, lanes]
    return jnp.concatenate([a0[:HEAD_DIM] / a0[HEAD_DIM:HEAD_DIM + 1],
                            a1[:HEAD_DIM] / a1[HEAD_DIM:HEAD_DIM + 1]], axis=0)


def _mla_kernel(q_ref, k_ref, vt_ref, o_ref, s_scr, p_scr, alpha_scr, m_scr, acc_scr, *, tq, tk):
    seq = q_ref.shape[1]
    n_diag = tq // tk
    sub = tk // BLK
    all_chunks = tuple(range(n_diag))
    below_diagonal = (lax.broadcasted_iota(jnp.int32, (tk, tk), 0)
                      <= lax.broadcasted_iota(jnp.int32, (tk, tk), 1))

    def q_tile(qt, _):
        q0 = pl.multiple_of(qt * tq, tq)
        qs = [q_ref[0, pl.ds(q0, tq), e * LANES:(e + 1) * LANES] for e in range(2)]
        n_full = qt * n_diag
        _init_pipeline(s_scr, p_scr, alpha_scr, m_scr, acc_scr)

        def vts(kt):
            kt = jnp.maximum(kt, 0)
            return [jnp.concatenate([vt_ref[0, kt * sub + c, e * VT_ROWS:(e + 1) * VT_ROWS, :]
                                     for c in range(sub)], axis=1) for e in range(2)]

        def qk_stage(kt, slot, diag):
            k0 = pl.multiple_of(kt * tk, tk)
            first = 0 if diag is None else diag
            for e in range(2):
                s = _nt_dot(k_ref[0, pl.ds(k0, tk), e * LANES:(e + 1) * LANES], qs[e][first * tk:])
                for ch in range(first, n_diag):
                    sc = s[:, (ch - first) * tk:(ch - first + 1) * tk]
                    if ch == diag:
                        sc = jnp.where(below_diagonal, sc, NEG_INF)
                    s_scr[slot, e, ch] = sc

        def chunks_of(c):
            return all_chunks[max(c, 0):]

        def full_steps(kk, carry):
            for u in range(n_diag):
                kt = n_diag * kk + u
                _pv_stage(vts(kt - 2), p_scr, alpha_scr, acc_scr, u % 2, all_chunks)
                qk_stage(kt, u % 2, None)
                _softmax_stage(s_scr, p_scr, alpha_scr, m_scr, 1 - u % 2, all_chunks)
            return carry

        lax.fori_loop(0, qt, full_steps, 0)
        for c in range(n_diag + 2):
            par = c % 2
            _pv_stage(vts(n_full + c - 2), p_scr, alpha_scr, acc_scr, par, chunks_of(c - 2))
            if c < n_diag:
                qk_stage(n_full + c, par, c)
            if c < n_diag + 1:
                _softmax_stage(s_scr, p_scr, alpha_scr, m_scr, 1 - par, chunks_of(c - 1))
        for ch in all_chunks:
            o_ref[0, pl.ds(pl.multiple_of(q0 + ch * tk, tk), tk), :] = jnp.transpose(
                _normalized_pair(acc_scr, ch, slice(None))).astype(o_ref.dtype)
        return 0

    lax.fori_loop(0, seq // tq, q_tile, 0)


def _mla_call(q, k, vt, tq=1024, tk=256):
    b, l, _ = q.shape
    tq, tk = min(tq, l), min(tk, l)
    assert (tq // tk) % 2 == 0 and l % tq == 0
    nb = l // BLK
    vt = vt.reshape(b, nb, HEADS * VT_ROWS, LANES)
    return pl.pallas_call(
        functools.partial(_mla_kernel, tq=tq, tk=tk),
        name="mla_attn",
        out_shape=jax.ShapeDtypeStruct((b, l, HEADS * HEAD_DIM), MXU_DTYPE),
        grid=(b, PAIRS),
        in_specs=[
            pl.BlockSpec((1, l, 2 * LANES), lambda bi, p: (bi, 0, p)),
            pl.BlockSpec((1, l, 2 * LANES), lambda bi, p: (bi, 0, p)),
            pl.BlockSpec((1, nb, 2 * VT_ROWS, LANES), lambda bi, p: (bi, 0, p, 0)),
        ],
        out_specs=pl.BlockSpec((1, l, LANES), lambda bi, p: (bi, 0, p)),
        scratch_shapes=_pipeline_scratch(tk, tq // tk, tk),
        compiler_params=pltpu.CompilerParams(dimension_semantics=("parallel", "parallel"),
                                             vmem_limit_bytes=VMEM_LIMIT),
    )(q, k, vt)


def _dsa_kernel(q_ref, qi_ref, wi_ref, kc_ref, vt_ref, kic_ref, bias_ref, tri_ref, o_ref,
                idx_scr, hi_scr, lo_scr, s_scr, p_scr, alpha_scr, m_scr, acc_scr, *, k_sel):
    i = pl.program_id(1)
    n_tiles = i + 1
    n_groups = i // TILE_GROUP + 1
    key_row = lax.broadcasted_iota(jnp.int32, (BLK, BLK), 0)
    query_col = lax.broadcasted_iota(jnp.int32, (BLK, BLK), 1)

    def admissible(j):
        return (j < i) | ((j == i) & (key_row <= query_col))

    wt = jnp.transpose(wi_ref[0])
    qi = qi_ref[0, 0]

    def score_group(g, carry):
        for u in range(TILE_GROUP):
            j = g * TILE_GROUP + u
            d = jnp.maximum(_nt_dot(kic_ref[0, j], qi), 0.0)
            sc = jnp.zeros((BLK, BLK), F32)
            for hp in range(IDX_HEADS // 2):
                for e in range(2):
                    h = 2 * hp + e
                    sc = sc + wt[h:h + 1, :] * d[e * BLK:(e + 1) * BLK, hp * BLK:(hp + 1) * BLK]
            sc = jnp.where(admissible(j), sc, NEG_INF)
            idx_scr[j] = sc
            bits = lax.bitcast_convert_type(sc, jnp.int32)
            key = jnp.where(bits >= 0, bits, bits ^ jnp.int32(0x7FFFFFFF))
            hi_scr[j] = lax.shift_right_arithmetic(key, 16).astype(jnp.int16)
            lo_scr[j] = ((key & 0xFFFF) - HALF_RANGE).astype(jnp.int16)
        return carry

    lax.fori_loop(0, n_groups, score_group, 0)

    def count(pred):
        def body(g, c):
            for u in range(TILE_GROUP):
                c = c + jnp.where(pred(idx_scr[g * TILE_GROUP + u]), 1.0, 0.0)
            return c
        c = lax.fori_loop(0, n_groups, body, jnp.zeros((BLK, BLK), F32))
        return jnp.sum(c, axis=0, keepdims=True)

    n_blocks = idx_scr.shape[0]
    step = min(SEARCH_STEP, n_blocks)
    sizes = tuple(range(step, n_blocks + 1, step))
    lowest = jnp.full((BLK, BLK), -HALF_RANGE, jnp.int16)
    for u in range(step - TILE_GROUP):
        hi_scr[n_groups * TILE_GROUP + u] = lowest
        lo_scr[n_groups * TILE_GROUP + u] = lowest

    def count16(ref, cand, n_static):
        cand16 = cand.astype(jnp.int16)
        c = jnp.zeros((BLK, BLK), jnp.int16)
        for j in range(n_static):
            c = c + jnp.where(ref[j] >= cand16, jnp.int16(1), jnp.int16(0))
        return jnp.sum(c.astype(jnp.int32), axis=0, keepdims=True)

    def search16(ref, need, n_static):
        def bit_step(it, v):
            cand_v = v | lax.shift_left(jnp.int32(1), 15 - it)
            return jnp.where(count16(ref, cand_v - HALF_RANGE, n_static) >= need, cand_v, v)
        return lax.fori_loop(0, 16, bit_step, jnp.zeros((1, BLK), jnp.int32))

    def search(n_static):
        t_hi = search16(hi_scr, k_sel, n_static) - HALF_RANGE
        above = jnp.where(t_hi == HALF_RANGE - 1, 0,
                          count16(hi_scr, jnp.minimum(t_hi + 1, HALF_RANGE - 1), n_static))
        t_hi16 = t_hi.astype(jnp.int16)
        for j in range(n_static):
            lo_scr[j] = jnp.where(hi_scr[j] == t_hi16, lo_scr[j], jnp.int16(-HALF_RANGE))
        t_lo = search16(lo_scr, k_sel - above, n_static)
        key = lax.shift_left(t_hi, 16) | t_lo
        bits = jnp.where(key >= 0, key, key ^ jnp.int32(0x7FFFFFFF))
        t = lax.bitcast_convert_type(bits, F32)
        return jnp.where(t != t, NEG_INF, t)

    def run_search():
        which = (n_groups * TILE_GROUP + step - 1) // step - 1
        return lax.switch(which, [functools.partial(search, n) for n in sizes])

    thr = lax.cond(n_tiles * BLK > k_sel, run_search, lambda: jnp.full((1, BLK), NEG_INF, F32))

    n_gt = count(lambda s: s > thr)
    n_ge = count(lambda s: s >= thr)
    tied = (n_ge > k_sel) & (thr > NEG_INF)

    @pl.when(jnp.max(jnp.where(tied, 1.0, 0.0)) > 0.0)
    def _():
        need = k_sel - n_gt

        def tie_tile(j, seen):
            s = idx_scr[j]
            eq = s == thr
            eq_f = jnp.where(eq, 1.0, 0.0).astype(MXU_DTYPE)
            rank = seen + _dot(tri_ref[...], eq_f)
            idx_scr[j] = jnp.where(eq & tied & (rank > need), NEG_INF, s)
            return seen + jnp.sum(eq_f.astype(F32), axis=0, keepdims=True)

        lax.fori_loop(0, n_tiles, tie_tile, jnp.zeros((1, BLK), F32))

    q = q_ref[0, 0]
    _init_pipeline(s_scr, p_scr, alpha_scr, m_scr, acc_scr)

    def qk_stage(j, slot, near, valid=None):
        sel = idx_scr[j] >= thr
        if near is not None:
            sel = sel & admissible(j)
        if valid is not None:
            sel = sel & valid
        mask = jnp.where(sel, 0.0, NEG_INF)
        mask = jnp.concatenate([mask] * PAIRS, axis=1)
        s = _nt_dot(kc_ref[0, j], q)
        if near is not None:
            s = s + bias_ref[near]
        for e in range(2):
            s_scr[slot, e, 0] = s[e * BLK:(e + 1) * BLK] + mask

    def vts(j):
        vt = vt_ref[0, jnp.maximum(j, 0)]
        return [vt, vt]

    n_far = jnp.maximum(i - 1, 0)
    n_trips = (n_far + TILE_GROUP - 1) // TILE_GROUP

    def far_steps(kk, carry):
        for u in range(TILE_GROUP):
            j = TILE_GROUP * kk + u
            _pv_stage(vts(j - 2), p_scr, alpha_scr, acc_scr, u % 2)
            qk_stage(jnp.minimum(j, i), u % 2, None, j < n_far)
            _softmax_stage(s_scr, p_scr, alpha_scr, m_scr, 1 - u % 2)
        return carry

    lax.fori_loop(0, n_trips, far_steps, 0)
    j0 = TILE_GROUP * n_trips
    tail = [(n_far, 0, i >= 1), (i, 1, None)]
    done = [jnp.minimum(j0 - 2, i), jnp.minimum(j0 - 1, i)] + [t[0] for t in tail]
    for c in range(len(tail) + 2):
        par = c % 2
        _pv_stage(vts(done[c]), p_scr, alpha_scr, acc_scr, par)
        if c < len(tail):
            qk_stage(tail[c][0], par, tail[c][1], tail[c][2])
        if c < len(tail) + 1:
            _softmax_stage(s_scr, p_scr, alpha_scr, m_scr, 1 - par)

    for p in range(PAIRS):
        sl = slice(p * BLK, (p + 1) * BLK)
        o_ref[0, :, sl] = jnp.transpose(_normalized_pair(acc_scr, 0, sl)).astype(o_ref.dtype)


def _dsa_call(q, qi, wi, kc, vt, kic, bias_t, tri, batch):
    nb = q.shape[0] // batch
    l = nb * BLK
    k_sel = min(TOPK_MAX, l // 4)
    assert nb % TILE_GROUP == 0
    per_batch =lambda a: a.reshape(batch, nb, *a.shape[1:])
    q, qi, kc, vt, kic = map(per_batch, (q, qi, kc, vt, kic))
    blk = lambda a: pl.BlockSpec((1, 1) + a.shape[2:], lambda bi, n: (bi, n, 0, 0))
    seq = lambda a: pl.BlockSpec((1,) + a.shape[1:], lambda bi, n: (bi, 0, 0, 0))
    return pl.pallas_call(
        functools.partial(_dsa_kernel, k_sel=k_sel),
        name="dsa_attn",
        out_shape=jax.ShapeDtypeStruct((batch, l, HEADS * HEAD_DIM), MXU_DTYPE),
        grid=(batch, nb),
        in_specs=[
            blk(q),
            blk(qi),
            pl.BlockSpec((1, BLK, LANES), lambda bi, n: (bi, n, 0)),
            seq(kc),
            seq(vt),
            seq(kic),
            pl.BlockSpec(bias_t.shape, lambda bi, n: (0, 0, 0)),
            pl.BlockSpec(tri.shape, lambda bi, n: (0, 0)),
        ],
        out_specs=pl.BlockSpec((1, BLK, HEADS * HEAD_DIM), lambda bi, n: (bi, n, 0)),
        scratch_shapes=[pltpu.VMEM((nb, BLK, BLK), F32), pltpu.VMEM((nb + SEARCH_STEP, BLK, BLK), jnp.int16),
                        pltpu.VMEM((nb + SEARCH_STEP, BLK, BLK), jnp.int16)]
        + _pipeline_scratch(BLK, 1, PAIRS * BLK),
        compiler_params=pltpu.CompilerParams(dimension_semantics=("parallel", "arbitrary"),
                                             vmem_limit_bytes=VMEM_LIMIT),
    )(q, qi, wi.reshape(batch, l, LANES), kc, vt, kic, bias_t, tri)


def _t5_bucket(rel):
    n = jnp.maximum(rel, 0)
    max_exact = N_BUCKETS // 2
    nf = jnp.maximum(n, 1).astype(F32)
    large = max_exact + (jnp.log(nf / max_exact) / math.log(MAX_DISTANCE / max_exact)
                         * (N_BUCKETS - max_exact)).astype(jnp.int32)
    large = jnp.minimum(large, N_BUCKETS - 1)
    return jnp.where(n < max_exact, n, large)


def _band_bias(rel_bias):
    qi = jnp.arange(BLK, dtype=jnp.int32)[:, None] + BLK
    kj = jnp.arange(2 * BLK, dtype=jnp.int32)[None, :]
    one_hot = (_t5_bucket(qi - kj)[None] == jnp.arange(N_BUCKETS, dtype=jnp.int32)[:, None, None]).astype(F32)
    return jnp.einsum("nh,nqk->hqk", rel_bias, one_hot, precision=lax.Precision.HIGHEST)


def _pair_gain(g, scale=1.0):
    return (jnp.concatenate([g, g]) * scale).reshape(1, LANES).astype(F32)


def _dup(w, n_heads):
    d = w.shape[0]
    w = w.reshape(d, n_heads, 1, HEAD_DIM)
    return jnp.broadcast_to(w, (d, n_heads, 2, HEAD_DIM)).reshape(d, n_heads * LANES)


def _layer_a(x, g_mix, w_in, q_gain, k_gain):
    nq = HEADS * HEAD_DIM
    sizes = [nq, HEAD_DIM, HEAD_DIM, IDX_HEADS * IDX_DIM, IDX_DIM, IDX_HEADS]
    wq, wk, wv, wqi, wki, wwi = jnp.split(w_in, np.cumsum(sizes)[:-1].tolist(), axis=1)
    pad_to_lanes = lambda a: jnp.pad(a, ((0, 0), (0, LANES - a.shape[1])))
    w = jnp.concatenate([wq, _dup(wk, 1), pad_to_lanes(wv), wqi, _dup(wki, 1), pad_to_lanes(wwi)],
                        axis=1).astype(MXU_DTYPE)
    consts = [g_mix.reshape(1, -1), w, _pair_gain(q_gain, HEAD_DIM ** -0.5 * LOG2_E), _pair_gain(k_gain)]
    outs = [("stacked", PAIRS * BLK, MXU_DTYPE), ("stacked", 2 * BLK, MXU_DTYPE), ("stacked", VT_ROWS, MXU_DTYPE),
            ("stacked", IDX_HEADS // 2 * BLK, MXU_DTYPE), ("stacked", 2 * BLK, MXU_DTYPE), ("rows", LANES, F32)]
    return _proj_call(_proj_a_kernel, x, consts, [], outs)


def _layer_b(x, g_mix, w_in, q_gain, k_gain):
    nq = HEADS * HEAD_DIM
    nkv = B_KV_HEADS * HEAD_DIM
    wq, wk, wv = jnp.split(w_in, [nq, nq + nkv], axis=1)
    wv = jnp.pad(wv.reshape(-1, B_KV_HEADS, HEAD_DIM), ((0, 0), (0, 0), (0, LANES - HEAD_DIM)))
    w = jnp.concatenate([wq, _dup(wk, B_KV_HEADS), wv.reshape(-1, B_KV_HEADS * LANES)], axis=1).astype(MXU_DTYPE)
    consts = [g_mix.reshape(1, -1), w, _pair_gain(q_gain, HEAD_DIM ** -0.5), _pair_gain(k_gain)]
    outs = [("stacked", PAIRS * BLK, MXU_DTYPE), ("stacked", B_KV_HEADS * 2 * BLK, MXU_DTYPE),
            ("stacked", B_KV_HEADS * VT_ROWS, MXU_DTYPE)]
    return _proj_call(_proj_b_kernel, x, consts, [], outs)


_half = C_ROPE // 2


def _to_mla_lanes(a):
    n = a.shape[-1]
    pad = [(0, 0)] * (a.ndim - 1)
    a = jnp.pad(a, pad + [(C_NOPE, 0) if n == C_ROPE else (0, C_QK - n)])
    split = LANES // 2 - _half
    return jnp.concatenate([a[..., :split], a[..., C_NOPE:C_NOPE + _half], a[..., split:C_NOPE],
                            jnp.zeros(a.shape[:-1] + (LANES - C_QK,), a.dtype), a[..., C_NOPE + _half:]], axis=-1)


def _rope_tables(cos, sin, gain):
    ones = jnp.ones(cos.shape[:-1] + (C_NOPE,), F32)
    own = _to_mla_lanes(jnp.concatenate([ones, cos, cos], axis=-1))
    partner = _to_mla_lanes(jnp.concatenate([-sin, sin], axis=-1))
    g = _to_mla_lanes(gain.astype(F32))
    return own * g, partner * jnp.roll(g, LANES // 2)


def _layer_c(x, g_mix, w_in, q_a_gain, w_q_b, kv_a_gain, w_kv_b, tables):
    d = w_in.shape[0]
    heads = lambda w, n: _to_mla_lanes(w.reshape(w.shape[0], HEADS, n)).reshape(w.shape[0], HEADS * LANES)
    w_in_p = jnp.concatenate([w_in[:, :C_Q_RANK + C_KV_RANK], _to_mla_lanes(w_in[:, C_Q_RANK + C_KV_RANK:])],
                             axis=1).astype(MXU_DTYPE)
    wq = heads(w_q_b, C_QK).astype(MXU_DTYPE)
    w_kv = w_kv_b.reshape(C_KV_RANK, HEADS, C_NOPE + HEAD_DIM)
    wk = heads(w_kv[:, :, :C_NOPE].reshape(C_KV_RANK, -1), C_NOPE).astype(MXU_DTYPE)
    wv = w_kv[:, :, C_NOPE:].reshape(C_KV_RANK, -1).astype(MXU_DTYPE)
    consts = [g_mix.reshape(1, -1), w_in_p, q_a_gain.reshape(1, -1), kv_a_gain.reshape(1, -1), wq, wk, wv]
    outs = [("rows", HEADS * LANES, MXU_DTYPE), ("rows", HEADS * LANES, MXU_DTYPE),
            ("stacked", HEADS * VT_ROWS, MXU_DTYPE)]
    return _proj_call(_proj_c_kernel, x, consts, list(tables), outs)


def kernel(x, positions, rel_bias, norm_mix, norm_mlp, w_up, w_down, a_w_in, a_q_gain, a_k_gain, a_w_out,
           b_w_in, b_q_gain, b_k_gain, b_sinks, b_w_out, c_w_in, c_q_a_gain, c_w_q_b, c_kv_a_gain, c_w_kv_b,
           c_q_gain, c_k_gain, c_w_out):
    b, l, d = x.shape
    t = b * l
    depth = norm_mix.shape[0]

    band = _band_bias(rel_bias.astype(F32))
    far = rel_bias[N_BUCKETS - 1].astype(F32)[:, None, None]
    bias_t = jnp.transpose(((band - far) * LOG2_E).reshape(PAIRS, 2, BLK, 2, BLK), (3, 1, 4, 0, 2)).reshape(
        2, 2 * BLK, PAIRS * BLK)
    tri = jnp.asarray(np.arange(BLK)[:, None] >= np.arange(BLK)[None, :], MXU_DTYPE)

    inv_freq = ROPE_THETA ** (-jnp.arange(0, C_ROPE, 2, dtype=F32) / C_ROPE)
    ang = positions.astype(F32).reshape(t, 1) * inv_freq
    cos, sin = jnp.cos(ang), jnp.sin(ang)

    xf = x.reshape(t, d).astype(F32)
    ia = ib = ic = 0
    for i in range(depth):
        kind = i % 3
        g_mix = norm_mix[i]
        if kind == 0:
            q, kc, vt, qi, kic, wi = _layer_a(xf, g_mix, a_w_in[ia], a_q_gain[ia], a_k_gain[ia])
            attn = _dsa_call(q, qi, wi, kc, vt, kic, bias_t, tri, b)
            w_out = a_w_out[ia]
            ia += 1
        elif kind == 1:
            q, kc, vt = _layer_b(xf, g_mix, b_w_in[ib], b_q_gain[ib], b_k_gain[ib])
            group_pairs = PAIRS // B_KV_HEADS
            swa_bias = jnp.transpose(band.reshape(B_KV_HEADS, group_pairs, 2, BLK, 2 * BLK),
                                     (0, 2, 4, 1, 3)).reshape(B_KV_HEADS, 2, 2 * BLK, group_pairs * BLK)
            sinks = jnp.transpose(b_sinks[ib].astype(F32).reshape(B_KV_HEADS, group_pairs, 2), (0, 2, 1))
            sinks = jnp.broadcast_to(sinks[..., None], (B_KV_HEADS, 2, group_pairs, BLK)).reshape(
                B_KV_HEADS, 2, 1, group_pairs * BLK)
            attn = _swa_call(q, kc, vt, swa_bias, sinks, b)
            w_out = b_w_out[ib]
            ib += 1
        else:
            tables = (_rope_tables(cos, sin, c_q_gain[ic] * (C_QK ** -0.5 * LOG2_E))
                      + _rope_tables(cos, sin, c_k_gain[ic]))
            q, k, vt = _layer_c(xf, g_mix, c_w_in[ic], c_q_a_gain[ic], c_w_q_b[ic], c_kv_a_gain[ic],
                               c_w_kv_b[ic], tables)
            r3 = lambda a: a.reshape(b, l, a.shape[-1])
            attn = _mla_call(r3(q), r3(k), vt)
            w_out = c_w_out[ic]
            ic += 1
        xf = _out_mlp_call(xf, attn.reshape(t, -1), w_out.astype(MXU_DTYPE), norm_mlp[i].reshape(1, -1),
                           w_up[i].astype(MXU_DTYPE), w_down[i].astype(MXU_DTYPE))
    return xf.reshape(b, l, d).astype(x.dtype)
```

```python
import functools
import math

import numpy as np
import jax
import jax.numpy as jnp
from jax import lax
from jax.experimental import pallas as pl
from jax.experimental.pallas import tpu as pltpu

F32 = jnp.float32
MXU_DTYPE = jnp.bfloat16

D_MODEL = 1024
D_FF = 4 * D_MODEL
RMS_EPS = 1e-6
N_BUCKETS = 32
MAX_DISTANCE = 128
HEADS = 16
HEAD_DIM = 64
PAIRS = HEADS // 2
LANES = 128
BLK = 128
IDX_HEADS = 8
IDX_DIM = 64
TOPK_MAX = 256
B_KV_HEADS = 4
WINDOW = 128
C_Q_RANK = 256
C_KV_RANK = 128
C_NOPE = 64
C_ROPE = 32
C_QK = C_NOPE + C_ROPE
ROPE_THETA = 10000.0
NEG_INF = float("-inf")
LOG2_E = math.log2(math.e)
HALF_RANGE = 1 << 15
TILE_GROUP = 4
SEARCH_STEP = 2 * TILE_GROUP
VT_ROWS = HEAD_DIM + 16
VMEM_LIMIT = 56 * 1024 * 1024


def _nt_dot(a, b):
    return lax.dot_general(a, b, (((1,), (1,)), ((), ())), preferred_element_type=F32)


def _dot(a, b):
    return jnp.dot(a, b, preferred_element_type=F32)


def _rms(x, g):
    ms = jnp.mean(x * x, axis=-1, keepdims=True)
    return x * lax.rsqrt(ms + RMS_EPS) * g


def _lane_iota(shape):
    return lax.broadcasted_iota(jnp.int32, shape, len(shape) - 1)


def _pair_rms(y, gain):
    left = _lane_iota(y.shape) < HEAD_DIM
    y2 = y * y
    s_left = jnp.sum(jnp.where(left, y2, 0.0), axis=-1, keepdims=True)
    s_right = jnp.sum(y2, axis=-1, keepdims=True) - s_left
    r = jnp.where(left, lax.rsqrt(s_left / HEAD_DIM + RMS_EPS), lax.rsqrt(s_right / HEAD_DIM + RMS_EPS))
    return y * r * gain


def _split_kv(t, parity, ones_lane=None):
    lane = _lane_iota(t.shape)
    keep = (lane < HEAD_DIM) if parity == 0 else (lane >= HEAD_DIM)
    out = jnp.where(keep, t, jnp.zeros_like(t))
    if ones_lane is not None:
        out = jnp.where(lane == ones_lane, jnp.ones_like(t), out)
    return out


def _put_blocks(out_ref, slot, tile):
    for bk in range(tile.shape[0] // BLK):
        out_ref[bk, slot * BLK:(slot + 1) * BLK, :] = tile[bk * BLK:(bk + 1) * BLK].astype(out_ref.dtype)


def _two_tiles(xn, w_ref, o):
    y = _dot(xn, w_ref[:, o:o + 2 * LANES])
    return y[:, :LANES], y[:, LANES:]


def _proj_a_kernel(x_ref, g_ref, w_ref, gq_ref, gk_ref, q_out, kc_out, vt_out, qi_out, kic_out, wi_out):
    xn = _rms(x_ref[...], g_ref[...]).astype(MXU_DTYPE)
    two = functools.partial(_two_tiles, xn, w_ref)
    for p in range(0, PAIRS, 2):
        qa, qb = two(p * LANES)
        _put_blocks(q_out, p, _pair_rms(qa, gq_ref[...]))
        _put_blocks(q_out, p + 1, _pair_rms(qb, gq_ref[...]))
    o = HEADS * HEAD_DIM
    k, v = two(o)
    k = _pair_rms(k, gk_ref[...])
    _put_blocks(kc_out, 0, _split_kv(k, 0))
    _put_blocks(kc_out, 1, _split_kv(k, 1))
    _put_vt(vt_out, 0, v)
    o += 2 * LANES
    for hp in range(0, IDX_HEADS // 2, 2):
        qia, qib = two(o)
        _put_blocks(qi_out, hp, qia * IDX_DIM ** -0.5)
        _put_blocks(qi_out, hp + 1, qib * IDX_DIM ** -0.5)
        o += 2 * LANES
    ki, wi = two(o)
    _put_blocks(kic_out, 0, _split_kv(ki, 0))
    _put_blocks(kic_out, 1, _split_kv(ki, 1))
    wi_out[...] = wi * IDX_HEADS ** -0.5


def _put_vt(vt_out, slot, v):
    vt = jnp.transpose(v)
    row = lax.broadcasted_iota(jnp.int32, vt.shape, 0)
    vt = jnp.where(row == HEAD_DIM, 1.0, vt)
    for bk in range(vt.shape[1] // BLK):
        vt_out[bk, slot * VT_ROWS:(slot + 1) * VT_ROWS, :] = (
            vt[:VT_ROWS, bk * BLK:(bk + 1) * BLK].astype(vt_out.dtype))


def _proj_b_kernel(x_ref, g_ref, w_ref, gq_ref, gk_ref, q_out, kc_out, vt_out):
    xn = _rms(x_ref[...], g_ref[...]).astype(MXU_DTYPE)
    two = functools.partial(_two_tiles, xn, w_ref)
    for p in range(0, PAIRS, 2):
        qa, qb = two(p * LANES)
        _put_blocks(q_out, p, _pair_rms(qa, gq_ref[...]))
        _put_blocks(q_out, p + 1, _pair_rms(qb, gq_ref[...]))
    o = HEADS * HEAD_DIM
    for c in range(0, B_KV_HEADS, 2):
        for c1, k in zip((c, c + 1), two(o + c * LANES)):
            k = _pair_rms(k, gk_ref[...])
            _put_blocks(kc_out, 2 * c1, _split_kv(k, 0))
            _put_blocks(kc_out, 2 * c1 + 1, _split_kv(k, 1))
    o += B_KV_HEADS * LANES
    for c in range(0, B_KV_HEADS, 2):
        for c1, v in zip((c, c + 1), two(o + c * LANES)):
            _put_vt(vt_out, c1, v)


def _norm_rope(y, own, partner):
    y = y * lax.rsqrt(jnp.sum(y * y, axis=-1, keepdims=True) / C_QK + RMS_EPS)
    return y * own + pltpu.roll(y, LANES // 2, 1) * partner


def _proj_c_kernel(x_ref, g_ref, w_in_ref, gqa_ref, gkva_ref, wq_ref, wk_ref, wv_ref,
                   q_own_ref, q_partner_ref, k_own_ref, k_partner_ref, q_out, k_out, vt_out):
    xn = _rms(x_ref[...], g_ref[...]).astype(MXU_DTYPE)
    lat = _dot(xn, w_in_ref[...])
    qa = _rms(lat[:, :C_Q_RANK], gqa_ref[...]).astype(MXU_DTYPE)
    kva = _rms(lat[:, C_Q_RANK:C_Q_RANK + C_KV_RANK], gkva_ref[...]).astype(MXU_DTYPE)
    k_rope = lat[:, C_Q_RANK + C_KV_RANK:]
    q_own, q_partner = q_own_ref[...], q_partner_ref[...]
    k_own, k_partner = k_own_ref[...], k_partner_ref[...]
    v = _dot(kva, wv_ref[...])
    tm = v.shape[0]
    ones_row = jnp.where(lax.broadcasted_iota(jnp.int32, (VT_ROWS - HEAD_DIM, tm), 0) == 0, 1.0, 0.0)
    for p in range(PAIRS):
        v_t = jnp.transpose(v[:, p * LANES:(p + 1) * LANES])
        for e in range(2):
            h = 2 * p + e
            head_t = jnp.concatenate([v_t[e * HEAD_DIM:(e + 1) * HEAD_DIM], ones_row], axis=0)
            for bk in range(tm // BLK):
                vt_out[bk, h * VT_ROWS:(h + 1) * VT_ROWS, :] = (
                    head_t[:, bk * BLK:(bk + 1) * BLK].astype(vt_out.dtype))
    for h0 in range(0, HEADS, 2):
        q2 = _dot(qa, wq_ref[:, h0 * LANES:(h0 + 2) * LANES])
        k2 = _dot(kva, wk_ref[:, h0 * LANES:(h0 + 2) * LANES])
        for e in range(2):
            sl = slice((h0 + e) * LANES, (h0 + e + 1) * LANES)
            q_out[:, sl] = _norm_rope(q2[:, e * LANES:(e + 1) * LANES], q_own, q_partner).astype(q_out.dtype)
            k_out[:, sl] = _norm_rope(k2[:, e * LANES:(e + 1) * LANES] + k_rope,
                                      k_own, k_partner).astype(k_out.dtype)


def _row_spec(tm, n):
    return pl.BlockSpec((tm, n), lambda i: (i, 0))


def _const_spec(shape):
    return pl.BlockSpec(shape, lambda i: (0,) * len(shape))


def _proj_call(kernel, x, consts, row_inputs, outs, tm=512):
    t = x.shape[0]
    in_specs = [_row_spec(tm, x.shape[1])] + [_const_spec(c.shape) for c in consts]
    in_specs += [_row_spec(tm, r.shape[1]) for r in row_inputs]
    out_shape, out_specs = [], []
    for kind, n, dtype in outs:
        if kind == "rows":
            out_shape.append(jax.ShapeDtypeStruct((t, n), dtype))
            out_specs.append(_row_spec(tm, n))
        else:
            out_shape.append(jax.ShapeDtypeStruct((t // BLK, n, LANES), dtype))
            out_specs.append(pl.BlockSpec((tm // BLK, n, LANES), lambda i: (i, 0, 0)))
    return pl.pallas_call(
        kernel,
        name=kernel.__name__.strip("_"),
        out_shape=out_shape,
        grid=(t // tm,),
        in_specs=in_specs,
        out_specs=out_specs,
        compiler_params=pltpu.CompilerParams(dimension_semantics=("parallel",), vmem_limit_bytes=VMEM_LIMIT),
    )(x, *consts, *row_inputs)


def _out_mlp_kernel(x_ref, a_ref, wo_ref, g_ref, wu_ref, wd_ref, o_ref, *, f_chunk):
    x1 = x_ref[...] + _dot(a_ref[...], wo_ref[...])
    h = _rms(x1, g_ref[...]).astype(MXU_DTYPE)
    acc = x1
    for c in range(D_FF // f_chunk):
        u = jnp.maximum(_dot(h, wu_ref[:, c * f_chunk:(c + 1) * f_chunk]), 0.0)
        acc = acc + _dot((u * u).astype(MXU_DTYPE), wd_ref[c * f_chunk:(c + 1) * f_chunk, :])
    o_ref[...] = acc


def _out_mlp_call(x, attn, w_out, g, w_up, w_down, tm=512, f_chunk=1024):
    t = x.shape[0]
    single = pl.Buffered(1)
    return pl.pallas_call(
        functools.partial(_out_mlp_kernel, f_chunk=f_chunk),
        name="out_mlp",
        out_shape=jax.ShapeDtypeStruct((t, D_MODEL), F32),
        grid=(t // tm,),
        in_specs=[
            _row_spec(tm, D_MODEL),
            _row_spec(tm, attn.shape[1]),
            pl.BlockSpec(w_out.shape, lambda i: (0, 0), pipeline_mode=single),
            _const_spec(g.shape),
            pl.BlockSpec(w_up.shape, lambda i: (0, 0), pipeline_mode=single),
            pl.BlockSpec(w_down.shape, lambda i: (0, 0), pipeline_mode=single),
        ],
        out_specs=_row_spec(tm, D_MODEL),
        compiler_params=pltpu.CompilerParams(dimension_semantics=("parallel",), vmem_limit_bytes=VMEM_LIMIT),
    )(x, attn, w_out, g, w_up, w_down)


def _swa_kernel(q_ref, kp_ref, kc_ref, vp_ref, vc_ref, bias_ref, sink_ref, o_ref):
    first_step = pl.program_id(1) == 0
    group_pairs = PAIRS // B_KV_HEADS
    w = group_pairs * BLK
    key = lax.broadcasted_iota(jnp.int32, (2 * BLK, w), 0)
    query = lax.broadcasted_iota(jnp.int32, (2 * BLK, w), 1) & (BLK - 1)
    rel = query + WINDOW - key
    in_band = (rel >= 0) & (rel < WINDOW)
    for blk in range(q_ref.shape[1]):
        k_prev, v_prev = (kp_ref, vp_ref) if blk == 0 else (kc_ref, vc_ref)
        prev = max(blk - 1, 0)
        ok = in_band & ((key >= WINDOW) | jnp.logical_not(first_step)) if blk == 0 else in_band
        for c in range(B_KV_HEADS):
            kk = jnp.concatenate([k_prev[0, prev, c * 2 * BLK:(c + 1) * 2 * BLK, :],
                                  kc_ref[0, blk, c * 2 * BLK:(c + 1) * 2 * BLK, :]], axis=0)
            s = _nt_dot(kk, q_ref[0, blk, c * w:(c + 1) * w, :])
            vt = jnp.concatenate([v_prev[0, prev, c * VT_ROWS:(c + 1) * VT_ROWS, :],
                                  vc_ref[0, blk, c * VT_ROWS:(c + 1) * VT_ROWS, :]], axis=1)
            halves = []
            for e in range(2):
                se = jnp.concatenate([s[e * BLK:(e + 1) * BLK], s[(2 + e) * BLK:(3 + e) * BLK]], axis=0)
                se = jnp.where(ok, se + bias_ref[c, e], NEG_INF)
                sink = sink_ref[c, e]
                m = jnp.maximum(jnp.max(se, axis=0, keepdims=True), sink)
                acc = _dot(vt, jnp.exp(se - m).astype(MXU_DTYPE))
                halves.append(acc[:HEAD_DIM] / (acc[HEAD_DIM:HEAD_DIM + 1] + jnp.exp(sink - m)))
            for pg in range(group_pairs):
                sl = slice(pg * BLK, (pg + 1) * BLK)
                pair = jnp.concatenate([halves[0][:, sl], halves[1][:, sl]], axis=0)
                p = c * group_pairs + pg
                o_ref[0, blk * BLK:(blk + 1) * BLK, p * LANES:(p + 1) * LANES] = (
                    jnp.transpose(pair).astype(o_ref.dtype))


def _swa_call(q, kc, vt, bias_t, sinks_t, batch, blocks_per_step=2):
    nb = q.shape[0] // batch
    g = blocks_per_step
    assert nb % g == 0
    per_batch = lambda a: a.reshape(batch, nb, *a.shape[1:])
    q, kc, vt = map(per_batch, (q, kc, vt))
    prev = lambda a: pl.BlockSpec((1, 1) + a.shape[2:], lambda bi, n: (bi, jnp.maximum(n * g - 1, 0), 0, 0))
    cur = lambda a: pl.BlockSpec((1, g) + a.shape[2:], lambda bi, n: (bi, n, 0, 0))
    return pl.pallas_call(
        _swa_kernel,
        name="swa_attn",
        out_shape=jax.ShapeDtypeStruct((batch, nb * BLK, HEADS * HEAD_DIM), MXU_DTYPE),
        grid=(batch, nb // g),
        in_specs=[cur(q), prev(kc), cur(kc), prev(vt), cur(vt),
                  pl.BlockSpec(bias_t.shape, lambda bi, n: (0, 0, 0, 0)),
                  pl.BlockSpec(sinks_t.shape, lambda bi, n: (0, 0, 0, 0))],
        out_specs=pl.BlockSpec((1, g * BLK, HEADS * HEAD_DIM), lambda bi, n: (bi, n, 0)),
        compiler_params=pltpu.CompilerParams(dimension_semantics=("parallel", "parallel"),
                                             vmem_limit_bytes=VMEM_LIMIT),
    )(q, kc, kc, vt, vt, bias_t, sinks_t)


def _pipeline_scratch(tk, n_chunks, lanes):
    return [pltpu.VMEM((2, 2, n_chunks, tk, lanes), F32), pltpu.VMEM((2, 2, n_chunks, tk, lanes), MXU_DTYPE),
            pltpu.VMEM((2, 2, n_chunks, 1, lanes), F32), pltpu.VMEM((2, n_chunks, 1, lanes), F32),
            pltpu.VMEM((2, n_chunks, VT_ROWS, lanes), F32)]


def _init_pipeline(s_scr, p_scr, alpha_scr, m_scr, acc_scr):
    m_scr[...] = jnp.full(m_scr.shape, NEG_INF, F32)
    acc_scr[...] = jnp.zeros(acc_scr.shape, F32)
    s_scr[1] = jnp.full(s_scr.shape[1:], NEG_INF, F32)
    p_scr[0] = jnp.zeros(p_scr.shape[1:], p_scr.dtype)
    alpha_scr[0] = jnp.ones(alpha_scr.shape[1:], F32)


def _softmax_stage(s_scr, p_scr, alpha_scr, m_scr, slot, chunks=(0,)):
    for e in range(2):
        for ch in chunks:
            s = s_scr[slot, e, ch]
            m_old = m_scr[e, ch]
            m_new = jnp.maximum(m_old, jnp.max(s, axis=0, keepdims=True))
            m_safe = jnp.where(m_new == NEG_INF, 0.0, m_new)
            p_scr[slot, e, ch] = jnp.exp2(s - m_safe).astype(p_scr.dtype)
            alpha_scr[slot, e, ch] = jnp.exp2(m_old - m_safe)
            m_scr[e, ch] = m_new


def _pv_stage(vts, p_scr, alpha_scr, acc_scr, slot, chunks=(0,)):
    for e in range(2):
        for ch in chunks:
            acc_scr[e, ch] = acc_scr[e, ch] * alpha_scr[slot, e, ch] + _dot(vts[e], p_scr[slot, e, ch])


def _normalized_pair(acc_scr, ch, lanes):
    a0, a1 = acc_scr[0, ch, :, lanes], acc_scr[1, ch, :, lanes]
    return jnp.concatenate([a0[:HEAD_DIM] / a0[HEAD_DIM:HEAD_DIM + 1],
                            a1[:HEAD_DIM] / a1[HEAD_DIM:HEAD_DIM + 1]], axis=0)


def _mla_kernel(q_ref, k_ref, vt_ref, o_ref, s_scr, p_scr, alpha_scr, m_scr, acc_scr, *, tq, tk):
    seq = q_ref.shape[1]
    n_diag = tq // tk
    sub = tk // BLK
    all_chunks = tuple(range(n_diag))
    below_diagonal = (lax.broadcasted_iota(jnp.int32, (tk, tk), 0)
                      <= lax.broadcasted_iota(jnp.int32, (tk, tk), 1))

    def q_tile(qt, _):
        q0 = pl.multiple_of(qt * tq, tq)
        qs = [q_ref[0, pl.ds(q0, tq), e * LANES:(e + 1) * LANES] for e in range(2)]
        n_full = qt * n_diag
        _init_pipeline(s_scr, p_scr, alpha_scr, m_scr, acc_scr)

        def vts(kt):
            kt = jnp.maximum(kt, 0)
            return [jnp.concatenate([vt_ref[0, kt * sub + c, e * VT_ROWS:(e + 1) * VT_ROWS, :]
                                     for c in range(sub)], axis=1) for e in range(2)]

        def qk_stage(kt, slot, diag):
            k0 = pl.multiple_of(kt * tk, tk)
            first = 0 if diag is None else diag
            for e in range(2):
                s = _nt_dot(k_ref[0, pl.ds(k0, tk), e * LANES:(e + 1) * LANES], qs[e][first * tk:])
                for ch in range(first, n_diag):
                    sc = s[:, (ch - first) * tk:(ch - first + 1) * tk]
                    if ch == diag:
                        sc = jnp.where(below_diagonal, sc, NEG_INF)
                    s_scr[slot, e, ch] = sc

        def chunks_of(c):
            return all_chunks[max(c, 0):]

        def full_steps(kk, carry):
            for u in range(n_diag):
                kt = n_diag * kk + u
                _pv_stage(vts(kt - 2), p_scr, alpha_scr, acc_scr, u % 2, all_chunks)
                qk_stage(kt, u % 2, None)
                _softmax_stage(s_scr, p_scr, alpha_scr, m_scr, 1 - u % 2, all_chunks)
            return carry

        lax.fori_loop(0, qt, full_steps, 0)
        for c in range(n_diag + 2):
            par = c % 2
            _pv_stage(vts(n_full + c - 2), p_scr, alpha_scr, acc_scr, par, chunks_of(c - 2))
            if c < n_diag:
                qk_stage(n_full + c, par, c)
            if c < n_diag + 1:
                _softmax_stage(s_scr, p_scr, alpha_scr, m_scr, 1 - par, chunks_of(c - 1))
        for ch in all_chunks:
            o_ref[0, pl.ds(pl.multiple_of(q0 + ch * tk, tk), tk), :] = jnp.transpose(
                _normalized_pair(acc_scr, ch, slice(None))).astype(o_ref.dtype)
        return 0

    lax.fori_loop(0, seq // tq, q_tile, 0)


def _mla_call(q, k, vt, tq=1024, tk=256):
    b, l, _ = q.shape
    tq, tk = min(tq, l), min(tk, l)
    assert (tq // tk) % 2 == 0 and l % tq == 0
    nb = l // BLK
    vt = vt.reshape(b, nb, HEADS * VT_ROWS, LANES)
    return pl.pallas_call(
        functools.partial(_mla_kernel, tq=tq, tk=tk),
        name="mla_attn",
        out_shape=jax.ShapeDtypeStruct((b, l, HEADS * HEAD_DIM), MXU_DTYPE),
        grid=(b, PAIRS),
        in_specs=[
            pl.BlockSpec((1, l, 2 * LANES), lambda bi, p: (bi, 0, p)),
            pl.BlockSpec((1, l, 2 * LANES), lambda bi, p: (bi, 0, p)),
            pl.BlockSpec((1, nb, 2 * VT_ROWS, LANES), lambda bi, p: (bi, 0, p, 0)),
        ],
        out_specs=pl.BlockSpec((1, l, LANES), lambda bi, p: (bi, 0, p)),
        scratch_shapes=_pipeline_scratch(tk, tq // tk, tk),
        compiler_params=pltpu.CompilerParams(dimension_semantics=("parallel", "parallel"),
                                             vmem_limit_bytes=VMEM_LIMIT),
    )(q, k, vt)


def _dsa_kernel(q_ref, qi_ref, wi_ref, kc_ref, vt_ref, kic_ref, bias_ref, tri_ref, o_ref,
                idx_scr, hi_scr, lo_scr, s_scr, p_scr, alpha_scr, m_scr, acc_scr, *, k_sel):
    i = pl.program_id(1)
    n_tiles = i + 1
    n_groups = i // TILE_GROUP + 1
    key_row = lax.broadcasted_iota(jnp.int32, (BLK, BLK), 0)
    query_col = lax.broadcasted_iota(jnp.int32, (BLK, BLK), 1)

    def admissible(j):
        return (j < i) | ((j == i) & (key_row <= query_col))

    wt = jnp.transpose(wi_ref[0])
    qi = qi_ref[0, 0]

    def score_group(g, carry):
        for u in range(TILE_GROUP):
            j = g * TILE_GROUP + u
            d = jnp.maximum(_nt_dot(kic_ref[0, j], qi), 0.0)
            sc = jnp.zeros((BLK, BLK), F32)
            for hp in range(IDX_HEADS // 2):
                for e in range(2):
                    h = 2 * hp + e
                    sc = sc + wt[h:h + 1, :] * d[e * BLK:(e + 1) * BLK, hp * BLK:(hp + 1) * BLK]
            sc = jnp.where(admissible(j), sc, NEG_INF)
            idx_scr[j] = sc
            bits = lax.bitcast_convert_type(sc, jnp.int32)
            key = jnp.where(bits >= 0, bits, bits ^ jnp.int32(0x7FFFFFFF))
            hi_scr[j] = lax.shift_right_arithmetic(key, 16).astype(jnp.int16)
            lo_scr[j] = ((key & 0xFFFF) - HALF_RANGE).astype(jnp.int16)
        return carry

    lax.fori_loop(0, n_groups, score_group, 0)

    def count(pred):
        def body(g, c):
            for u in range(TILE_GROUP):
                c = c + jnp.where(pred(idx_scr[g * TILE_GROUP + u]), 1.0, 0.0)
            return c
        c = lax.fori_loop(0, n_groups, body, jnp.zeros((BLK, BLK), F32))
        return jnp.sum(c, axis=0, keepdims=True)

    n_blocks = idx_scr.shape[0]
    step = min(SEARCH_STEP, n_blocks)
    sizes = tuple(range(step, n_blocks + 1, step))
    lowest = jnp.full((BLK, BLK), -HALF_RANGE, jnp.int16)
    for u in range(step - TILE_GROUP):
        hi_scr[n_groups * TILE_GROUP + u] = lowest
        lo_scr[n_groups * TILE_GROUP + u] = lowest

    def count16(ref, cand, n_static):
        cand16 = cand.astype(jnp.int16)
        c = jnp.zeros((BLK, BLK), jnp.int16)
        for j in range(n_static):
            c = c + jnp.where(ref[j] >= cand16, jnp.int16(1), jnp.int16(0))
        return jnp.sum(c.astype(jnp.int32), axis=0, keepdims=True)

    def search16(ref, need, n_static):
        def bit_step(it, v):
            cand_v = v | lax.shift_left(jnp.int32(1), 15 - it)
            return jnp.where(count16(ref, cand_v - HALF_RANGE, n_static) >= need, cand_v, v)
        return lax.fori_loop(0, 16, bit_step, jnp.zeros((1, BLK), jnp.int32))

    def search(n_static):
        t_hi = search16(hi_scr, k_sel, n_static) - HALF_RANGE
        above = jnp.where(t_hi == HALF_RANGE - 1, 0,
                          count16(hi_scr, jnp.minimum(t_hi + 1, HALF_RANGE - 1), n_static))
        t_hi16 = t_hi.astype(jnp.int16)
        for j in range(n_static):
            lo_scr[j] = jnp.where(hi_scr[j] == t_hi16, lo_scr[j], jnp.int16(-HALF_RANGE))
        t_lo = search16(lo_scr, k_sel - above, n_static)
        key = lax.shift_left(t_hi, 16) | t_lo
        bits = jnp.where(key >= 0, key, key ^ jnp.int32(0x7FFFFFFF))
        t = lax.bitcast_convert_type(bits, F32)
        return jnp.where(t != t, NEG_INF, t)

    def run_search():
        which = (n_groups * TILE_GROUP + step - 1) // step - 1
        return lax.switch(which, [functools.partial(search, n) for n in sizes])

    thr = lax.cond(n_tiles * BLK > k_sel, run_search, lambda: jnp.full((1, BLK), NEG_INF, F32))

    n_gt = count(lambda s: s > thr)
    n_ge = count(lambda s: s >= thr)
    tied = (n_ge > k_sel) & (thr > NEG_INF)

    @pl.when(jnp.max(jnp.where(tied, 1.0, 0.0)) > 0.0)
    def _():
        need = k_sel - n_gt

        def tie_tile(j, seen):
            s = idx_scr[j]
            eq = s == thr
            eq_f = jnp.where(eq, 1.0, 0.0).astype(MXU_DTYPE)
            rank = seen + _dot(tri_ref[...], eq_f)
            idx_scr[j] = jnp.where(eq & tied & (rank > need), NEG_INF, s)
            return seen + jnp.sum(eq_f.astype(F32), axis=0, keepdims=True)

        lax.fori_loop(0, n_tiles, tie_tile, jnp.zeros((1, BLK), F32))

    q = q_ref[0, 0]
    _init_pipeline(s_scr, p_scr, alpha_scr, m_scr, acc_scr)

    def qk_stage(j, slot, near, valid=None):
        sel = idx_scr[j] >= thr
        if near is not None:
            sel = sel & admissible(j)
        if valid is not None:
            sel = sel & valid
        mask = jnp.where(sel, 0.0, NEG_INF)
        mask = jnp.concatenate([mask] * PAIRS, axis=1)
        s = _nt_dot(kc_ref[0, j], q)
        if near is not None:
            s = s + bias_ref[near]
        for e in range(2):
            s_scr[slot, e, 0] = s[e * BLK:(e + 1) * BLK] + mask

    def vts(j):
        vt = vt_ref[0, jnp.maximum(j, 0)]
        return [vt, vt]

    n_far = jnp.maximum(i - 1, 0)
    n_trips = (n_far + TILE_GROUP - 1) // TILE_GROUP

    def far_steps(kk, carry):
        for u in range(TILE_GROUP):
            j = TILE_GROUP * kk + u
            _pv_stage(vts(j - 2), p_scr, alpha_scr, acc_scr, u % 2)
            qk_stage(jnp.minimum(j, i), u % 2, None, j < n_far)
            _softmax_stage(s_scr, p_scr, alpha_scr, m_scr, 1 - u % 2)
        return carry

    lax.fori_loop(0, n_trips, far_steps, 0)
    j0 = TILE_GROUP * n_trips
    tail = [(n_far, 0, i >= 1), (i, 1, None)]
    done = [jnp.minimum(j0 - 2, i), jnp.minimum(j0 - 1, i)] + [t[0] for t in tail]
    for c in range(len(tail) + 2):
        par = c % 2
        _pv_stage(vts(done[c]), p_scr, alpha_scr, acc_scr, par)
        if c < len(tail):
            qk_stage(tail[c][0], par, tail[c][1], tail[c][2])
        if c < len(tail) + 1:
            _softmax_stage(s_scr, p_scr, alpha_scr, m_scr, 1 - par)

    for p in range(PAIRS):
        sl = slice(p * BLK, (p + 1) * BLK)
        o_ref[0, :, sl] = jnp.transpose(_normalized_pair(acc_scr, 0, sl)).astype(o_ref.dtype)


def _dsa_call(q, qi, wi, kc, vt, kic, bias_t, tri, batch):
    nb = q.shape[0] // batch
    l = nb * BLK
    k_sel = min(TOPK_MAX, l // 4)
    assert nb % TILE_GROUP == 0
    per_batch =lambda a: a.reshape(batch, nb, *a.shape[1:])
    q, qi, kc, vt, kic = map(per_batch, (q, qi, kc, vt, kic))
    blk = lambda a: pl.BlockSpec((1, 1) + a.shape[2:], lambda bi, n: (bi, n, 0, 0))
    seq = lambda a: pl.BlockSpec((1,) + a.shape[1:], lambda bi, n: (bi, 0, 0, 0))
    return pl.pallas_call(
        functools.partial(_dsa_kernel, k_sel=k_sel),
        name="dsa_attn",
        out_shape=jax.ShapeDtypeStruct((batch, l, HEADS * HEAD_DIM), MXU_DTYPE),
        grid=(batch, nb),
        in_specs=[
            blk(q),
            blk(qi),
            pl.BlockSpec((1, BLK, LANES), lambda bi, n: (bi, n, 0)),
            seq(kc),
            seq(vt),
            seq(kic),
            pl.BlockSpec(bias_t.shape, lambda bi, n: (0, 0, 0)),
            pl.BlockSpec(tri.shape, lambda bi, n: (0, 0)),
        ],
        out_specs=pl.BlockSpec((1, BLK, HEADS * HEAD_DIM), lambda bi, n: (bi, n, 0)),
        scratch_shapes=[pltpu.VMEM((nb, BLK, BLK), F32), pltpu.VMEM((nb + SEARCH_STEP, BLK, BLK), jnp.int16),
                        pltpu.VMEM((nb + SEARCH_STEP, BLK, BLK), jnp.int16)]
        + _pipeline_scratch(BLK, 1, PAIRS * BLK),
        compiler_params=pltpu.CompilerParams(dimension_semantics=("parallel", "arbitrary"),
                                             vmem_limit_bytes=VMEM_LIMIT),
    )(q, qi, wi.reshape(batch, l, LANES), kc, vt, kic, bias_t, tri)


def _t5_bucket(rel):
    n = jnp.maximum(rel, 0)
    max_exact = N_BUCKETS // 2
    nf = jnp.maximum(n, 1).astype(F32)
    large = max_exact + (jnp.log(nf / max_exact) / math.log(MAX_DISTANCE / max_exact)
                         * (N_BUCKETS - max_exact)).astype(jnp.int32)
    large = jnp.minimum(large, N_BUCKETS - 1)
    return jnp.where(n < max_exact, n, large)


def _band_bias(rel_bias):
    qi = jnp.arange(BLK, dtype=jnp.int32)[:, None] + BLK
    kj = jnp.arange(2 * BLK, dtype=jnp.int32)[None, :]
    one_hot = (_t5_bucket(qi - kj)[None] == jnp.arange(N_BUCKETS, dtype=jnp.int32)[:, None, None]).astype(F32)
    return jnp.einsum("nh,nqk->hqk", rel_bias, one_hot, precision=lax.Precision.HIGHEST)


def _pair_gain(g, scale=1.0):
    return (jnp.concatenate([g, g]) * scale).reshape(1, LANES).astype(F32)


def _dup(w, n_heads):
    d = w.shape[0]
    w = w.reshape(d, n_heads, 1, HEAD_DIM)
    return jnp.broadcast_to(w, (d, n_heads, 2, HEAD_DIM)).reshape(d, n_heads * LANES)


def _layer_a(x, g_mix, w_in, q_gain, k_gain):
    nq = HEADS * HEAD_DIM
    sizes = [nq, HEAD_DIM, HEAD_DIM, IDX_HEADS * IDX_DIM, IDX_DIM, IDX_HEADS]
    wq, wk, wv, wqi, wki, wwi = jnp.split(w_in, np.cumsum(sizes)[:-1].tolist(), axis=1)
    pad_to_lanes = lambda a: jnp.pad(a, ((0, 0), (0, LANES - a.shape[1])))
    w = jnp.concatenate([wq, _dup(wk, 1), pad_to_lanes(wv), wqi, _dup(wki, 1), pad_to_lanes(wwi)],
                        axis=1).astype(MXU_DTYPE)
    consts = [g_mix.reshape(1, -1), w, _pair_gain(q_gain, HEAD_DIM ** -0.5 * LOG2_E), _pair_gain(k_gain)]
    outs = [("stacked", PAIRS * BLK, MXU_DTYPE), ("stacked", 2 * BLK, MXU_DTYPE), ("stacked", VT_ROWS, MXU_DTYPE),
            ("stacked", IDX_HEADS // 2 * BLK, MXU_DTYPE), ("stacked", 2 * BLK, MXU_DTYPE), ("rows", LANES, F32)]
    return _proj_call(_proj_a_kernel, x, consts, [], outs)


def _layer_b(x, g_mix, w_in, q_gain, k_gain):
    nq = HEADS * HEAD_DIM
    nkv = B_KV_HEADS * HEAD_DIM
    wq, wk, wv = jnp.split(w_in, [nq, nq + nkv], axis=1)
    wv = jnp.pad(wv.reshape(-1, B_KV_HEADS, HEAD_DIM), ((0, 0), (0, 0), (0, LANES - HEAD_DIM)))
    w = jnp.concatenate([wq, _dup(wk, B_KV_HEADS), wv.reshape(-1, B_KV_HEADS * LANES)], axis=1).astype(MXU_DTYPE)
    consts = [g_mix.reshape(1, -1), w, _pair_gain(q_gain, HEAD_DIM ** -0.5), _pair_gain(k_gain)]
    outs = [("stacked", PAIRS * BLK, MXU_DTYPE), ("stacked", B_KV_HEADS * 2 * BLK, MXU_DTYPE),
            ("stacked", B_KV_HEADS * VT_ROWS, MXU_DTYPE)]
    return _proj_call(_proj_b_kernel, x, consts, [], outs)


_half = C_ROPE // 2
MLA_LANE = np.concatenate([np.arange(LANES // 2 - _half), LANES // 2 + np.arange(C_NOPE - (LANES // 2 - _half)),
                           LANES // 2 - _half + np.arange(_half), LANES - _half + np.arange(_half)])


def _to_mla_lanes(a):
    n = a.shape[-1]
    lanes = MLA_LANE[:n] if n != C_ROPE else MLA_LANE[C_NOPE:]
    place = np.zeros((n, LANES), np.float32)
    place[np.arange(n), lanes] = 1.0
    return jnp.einsum("...n,nl->...l", a, jnp.asarray(place, a.dtype), precision=lax.Precision.HIGHEST)


def _rope_tables(cos, sin, gain):
    ones = jnp.ones(cos.shape[:-1] + (C_NOPE,), F32)
    own = _to_mla_lanes(jnp.concatenate([ones, cos, cos], axis=-1))
    partner = _to_mla_lanes(jnp.concatenate([-sin, sin], axis=-1))
    g = _to_mla_lanes(gain.astype(F32))
    return own * g, partner * jnp.roll(g, LANES // 2)


def _layer_c(x, g_mix, w_in, q_a_gain, w_q_b, kv_a_gain, w_kv_b, tables):
    d = w_in.shape[0]
    heads = lambda w, n: _to_mla_lanes(w.reshape(w.shape[0], HEADS, n)).reshape(w.shape[0], HEADS * LANES)
    w_in_p = jnp.concatenate([w_in[:, :C_Q_RANK + C_KV_RANK], _to_mla_lanes(w_in[:, C_Q_RANK + C_KV_RANK:])],
                             axis=1).astype(MXU_DTYPE)
    wq = heads(w_q_b, C_QK).astype(MXU_DTYPE)
    w_kv = w_kv_b.reshape(C_KV_RANK, HEADS, C_NOPE + HEAD_DIM)
    wk = heads(w_kv[:, :, :C_NOPE].reshape(C_KV_RANK, -1), C_NOPE).astype(MXU_DTYPE)
    wv = w_kv[:, :, C_NOPE:].reshape(C_KV_RANK, -1).astype(MXU_DTYPE)
    consts = [g_mix.reshape(1, -1), w_in_p, q_a_gain.reshape(1, -1), kv_a_gain.reshape(1, -1), wq, wk, wv]
    outs = [("rows", HEADS * LANES, MXU_DTYPE), ("rows", HEADS * LANES, MXU_DTYPE),
            ("stacked", HEADS * VT_ROWS, MXU_DTYPE)]
    return _proj_call(_proj_c_kernel, x, consts, list(tables), outs)


def kernel(x, positions, rel_bias, norm_mix, norm_mlp, w_up, w_down, a_w_in, a_q_gain, a_k_gain, a_w_out,
           b_w_in, b_q_gain, b_k_gain, b_sinks, b_w_out, c_w_in, c_q_a_gain, c_w_q_b, c_kv_a_gain, c_w_kv_b,
           c_q_gain, c_k_gain, c_w_out):
    b, l, d = x.shape
    t = b * l
    depth = norm_mix.shape[0]

    band = _band_bias(rel_bias.astype(F32))
    far = rel_bias[N_BUCKETS - 1].astype(F32)[:, None, None]
    bias_t = jnp.transpose(((band - far) * LOG2_E).reshape(PAIRS, 2, BLK, 2, BLK), (3, 1, 4, 0, 2)).reshape(
        2, 2 * BLK, PAIRS * BLK)
    tri = jnp.asarray(np.arange(BLK)[:, None] >= np.arange(BLK)[None, :], MXU_DTYPE)

    inv_freq = ROPE_THETA ** (-jnp.arange(0, C_ROPE, 2, dtype=F32) / C_ROPE)
    ang = positions.astype(F32).reshape(t, 1) * inv_freq
    cos, sin = jnp.cos(ang), jnp.sin(ang)

    xf = x.reshape(t, d).astype(F32)
    ia = ib = ic = 0
    for i in range(depth):
        kind = i % 3
        g_mix = norm_mix[i]
        if kind == 0:
            q, kc, vt, qi, kic, wi = _layer_a(xf, g_mix, a_w_in[ia], a_q_gain[ia], a_k_gain[ia])
            attn = _dsa_call(q, qi, wi, kc, vt, kic, bias_t, tri, b)
            w_out = a_w_out[ia]
            ia += 1
        elif kind == 1:
            q, kc, vt = _layer_b(xf, g_mix, b_w_in[ib], b_q_gain[ib], b_k_gain[ib])
            group_pairs = PAIRS // B_KV_HEADS
            swa_bias = jnp.transpose(band.reshape(B_KV_HEADS, group_pairs, 2, BLK, 2 * BLK),
                                     (0, 2, 4, 1, 3)).reshape(B_KV_HEADS, 2, 2 * BLK, group_pairs * BLK)
            sinks = jnp.transpose(b_sinks[ib].astype(F32).reshape(B_KV_HEADS, group_pairs, 2), (0, 2, 1))
            sinks = jnp.broadcast_to(sinks[..., None], (B_KV_HEADS, 2, group_pairs, BLK)).reshape(
                B_KV_HEADS, 2, 1, group_pairs * BLK)
            attn = _swa_call(q, kc, vt, swa_bias, sinks, b)
            w_out = b_w_out[ib]
            ib += 1
        else:
            tables = (_rope_tables(cos, sin, c_q_gain[ic] * (C_QK ** -0.5 * LOG2_E))
                      + _rope_tables(cos, sin, c_k_gain[ic]))
            q, k, vt = _layer_c(xf, g_mix, c_w_in[ic], c_q_a_gain[ic], c_w_q_b[ic], c_kv_a_gain[ic],
                               c_w_kv_b[ic], tables)
            r3 = lambda a: a.reshape(b, l, a.shape[-1])
            attn = _mla_call(r3(q), r3(k), vt)
            w_out = c_w_out[ic]
            ic += 1
        xf = _out_mlp_call(xf, attn.reshape(t, -1), w_out.astype(MXU_DTYPE), norm_mlp[i].reshape(1, -1),
                           w_up[i].astype(MXU_DTYPE), w_down[i].astype(MXU_DTYPE))
    return xf.reshape(b, l, d).astype(x.dtype)
```

```python
import functools
import math

import numpy as np
import jax
import jax.numpy as jnp
from jax import lax
from jax.experimental import pallas as pl
from jax.experimental.pallas import tpu as pltpu

F32 = jnp.float32
MXU_DTYPE = jnp.bfloat16

D_MODEL = 1024
D_FF = 4 * D_MODEL
RMS_EPS = 1e-6
N_BUCKETS = 32
MAX_DISTANCE = 128
HEADS = 16
HEAD_DIM = 64
PAIRS = HEADS // 2
LANES = 128
BLK = 128
IDX_HEADS = 8
IDX_DIM = 64
TOPK_MAX = 256
B_KV_HEADS = 4
WINDOW = 128
C_Q_RANK = 256
C_KV_RANK = 128
C_NOPE = 64
C_ROPE = 32
C_QK = C_NOPE + C_ROPE
ROPE_THETA = 10000.0
NEG_INF = float("-inf")
LOG2_E = math.log2(math.e)
HALF_RANGE = 1 << 15
TILE_GROUP = 4
SEARCH_STEP = 2 * TILE_GROUP
VT_ROWS = HEAD_DIM + 16
VMEM_LIMIT = 56 * 1024 * 1024


def _nt_dot(a, b):
    return lax.dot_general(a, b, (((1,), (1,)), ((), ())), preferred_element_type=F32)


def _dot(a, b):
    return jnp.dot(a, b, preferred_element_type=F32)


def _rms(x, g):
    ms = jnp.mean(x * x, axis=-1, keepdims=True)
    return x * lax.rsqrt(ms + RMS_EPS) * g


def _lane_iota(shape):
    return lax.broadcasted_iota(jnp.int32, shape, len(shape) - 1)


def _pair_rms(y, gain):
    left = _lane_iota(y.shape) < HEAD_DIM
    y2 = y * y
    s_left = jnp.sum(jnp.where(left, y2, 0.0), axis=-1, keepdims=True)
    s_right = jnp.sum(y2, axis=-1, keepdims=True) - s_left
    r = jnp.where(left, lax.rsqrt(s_left / HEAD_DIM + RMS_EPS), lax.rsqrt(s_right / HEAD_DIM + RMS_EPS))
    return y * r * gain


def _split_kv(t, parity, ones_lane=None):
    lane = _lane_iota(t.shape)
    keep = (lane < HEAD_DIM) if parity == 0 else (lane >= HEAD_DIM)
    out = jnp.where(keep, t, jnp.zeros_like(t))
    if ones_lane is not None:
        out = jnp.where(lane == ones_lane, jnp.ones_like(t), out)
    return out


def _put_blocks(out_ref, slot, tile):
    for bk in range(tile.shape[0] // BLK):
        out_ref[bk, slot * BLK:(slot + 1) * BLK, :] = tile[bk * BLK:(bk + 1) * BLK].astype(out_ref.dtype)


def _two_tiles(xn, w_ref, o):
    y = _dot(xn, w_ref[:, o:o + 2 * LANES])
    return y[:, :LANES], y[:, LANES:]


def _proj_a_kernel(x_ref, g_ref, w_ref, gq_ref, gk_ref, q_out, kc_out, vt_out, qi_out, kic_out, wi_out):
    xn = _rms(x_ref[...], g_ref[...]).astype(MXU_DTYPE)
    two = functools.partial(_two_tiles, xn, w_ref)
    for p in range(0, PAIRS, 2):
        qa, qb = two(p * LANES)
        _put_blocks(q_out, p, _pair_rms(qa, gq_ref[...]))
        _put_blocks(q_out, p + 1, _pair_rms(qb, gq_ref[...]))
    o = HEADS * HEAD_DIM
    k, v = two(o)
    k = _pair_rms(k, gk_ref[...])
    _put_blocks(kc_out, 0, _split_kv(k, 0))
    _put_blocks(kc_out, 1, _split_kv(k, 1))
    _put_vt(vt_out, 0, v)
    o += 2 * LANES
    for hp in range(0, IDX_HEADS // 2, 2):
        qia, qib = two(o)
        _put_blocks(qi_out, hp, qia * IDX_DIM ** -0.5)
        _put_blocks(qi_out, hp + 1, qib * IDX_DIM ** -0.5)
        o += 2 * LANES
    ki, wi = two(o)
    _put_blocks(kic_out, 0, _split_kv(ki, 0))
    _put_blocks(kic_out, 1, _split_kv(ki, 1))
    wi_out[...] = wi * IDX_HEADS ** -0.5


def _put_vt(vt_out, slot, v):
    vt = jnp.transpose(v)
    row = lax.broadcasted_iota(jnp.int32, vt.shape, 0)
    vt = jnp.where(row == HEAD_DIM, 1.0, vt)
    for bk in range(vt.shape[1] // BLK):
        vt_out[bk, slot * VT_ROWS:(slot + 1) * VT_ROWS, :] = (
            vt[:VT_ROWS, bk * BLK:(bk + 1) * BLK].astype(vt_out.dtype))


def _proj_b_kernel(x_ref, g_ref, w_ref, gq_ref, gk_ref, q_out, kc_out, vt_out):
    xn = _rms(x_ref[...], g_ref[...]).astype(MXU_DTYPE)
    two = functools.partial(_two_tiles, xn, w_ref)
    for p in range(0, PAIRS, 2):
        qa, qb = two(p * LANES)
        _put_blocks(q_out, p, _pair_rms(qa, gq_ref[...]))
        _put_blocks(q_out, p + 1, _pair_rms(qb, gq_ref[...]))
    o = HEADS * HEAD_DIM
    for c in range(0, B_KV_HEADS, 2):
        for c1, k in zip((c, c + 1), two(o + c * LANES)):
            k = _pair_rms(k, gk_ref[...])
            _put_blocks(kc_out, 2 * c1, _split_kv(k, 0))
            _put_blocks(kc_out, 2 * c1 + 1, _split_kv(k, 1))
    o += B_KV_HEADS * LANES
    for c in range(0, B_KV_HEADS, 2):
        for c1, v in zip((c, c + 1), two(o + c * LANES)):
            _put_vt(vt_out, c1, v)


def _norm_rope(y, own, partner):
    y = y * lax.rsqrt(jnp.sum(y * y, axis=-1, keepdims=True) / C_QK + RMS_EPS)
    return y * own + pltpu.roll(y, LANES // 2, 1) * partner


def _proj_c_kernel(x_ref, g_ref, w_in_ref, gqa_ref, gkva_ref, wq_ref, wk_ref, wv_ref,
                   q_own_ref, q_partner_ref, k_own_ref, k_partner_ref, q_out, k_out, vt_out):
    xn = _rms(x_ref[...], g_ref[...]).astype(MXU_DTYPE)
    lat = _dot(xn, w_in_ref[...])
    qa = _rms(lat[:, :C_Q_RANK], gqa_ref[...]).astype(MXU_DTYPE)
    kva = _rms(lat[:, C_Q_RANK:C_Q_RANK + C_KV_RANK], gkva_ref[...]).astype(MXU_DTYPE)
    k_rope = lat[:, C_Q_RANK + C_KV_RANK:]
    q_own, q_partner = q_own_ref[...], q_partner_ref[...]
    k_own, k_partner = k_own_ref[...], k_partner_ref[...]
    v = _dot(kva, wv_ref[...])
    tm = v.shape[0]
    ones_row = jnp.where(lax.broadcasted_iota(jnp.int32, (VT_ROWS - HEAD_DIM, tm), 0) == 0, 1.0, 0.0)
    for p in range(PAIRS):
        v_t = jnp.transpose(v[:, p * LANES:(p + 1) * LANES])
        for e in range(2):
            h = 2 * p + e
            head_t = jnp.concatenate([v_t[e * HEAD_DIM:(e + 1) * HEAD_DIM], ones_row], axis=0)
            for bk in range(tm // BLK):
                vt_out[bk, h * VT_ROWS:(h + 1) * VT_ROWS, :] = (
                    head_t[:, bk * BLK:(bk + 1) * BLK].astype(vt_out.dtype))
    for h0 in range(0, HEADS, 2):
        q2 = _dot(qa, wq_ref[:, h0 * LANES:(h0 + 2) * LANES])
        k2 = _dot(kva, wk_ref[:, h0 * LANES:(h0 + 2) * LANES])
        for e in range(2):
            sl = slice((h0 + e) * LANES, (h0 + e + 1) * LANES)
            q_out[:, sl] = _norm_rope(q2[:, e * LANES:(e + 1) * LANES], q_own, q_partner).astype(q_out.dtype)
            k_out[:, sl] = _norm_rope(k2[:, e * LANES:(e + 1) * LANES] + k_rope,
                                      k_own, k_partner).astype(k_out.dtype)


def _row_spec(tm, n):
    return pl.BlockSpec((tm, n), lambda i: (i, 0))


def _const_spec(shape):
    return pl.BlockSpec(shape, lambda i: (0,) * len(shape))


def _proj_call(kernel, x, consts, row_inputs, outs, tm=512):
    t = x.shape[0]
    in_specs = [_row_spec(tm, x.shape[1])] + [_const_spec(c.shape) for c in consts]
    in_specs += [_row_spec(tm, r.shape[1]) for r in row_inputs]
    out_shape, out_specs = [], []
    for kind, n, dtype in outs:
        if kind == "rows":
            out_shape.append(jax.ShapeDtypeStruct((t, n), dtype))
            out_specs.append(_row_spec(tm, n))
        else:
            out_shape.append(jax.ShapeDtypeStruct((t // BLK, n, LANES), dtype))
            out_specs.append(pl.BlockSpec((tm // BLK, n, LANES), lambda i: (i, 0, 0)))
    return pl.pallas_call(
        kernel,
        name=kernel.__name__.strip("_"),
        out_shape=out_shape,
        grid=(t // tm,),
        in_specs=in_specs,
        out_specs=out_specs,
        compiler_params=pltpu.CompilerParams(dimension_semantics=("parallel",), vmem_limit_bytes=VMEM_LIMIT),
    )(x, *consts, *row_inputs)


def _out_mlp_kernel(x_ref, a_ref, wo_ref, g_ref, wu_ref, wd_ref, o_ref, *, f_chunk):
    x1 = x_ref[...] + _dot(a_ref[...], wo_ref[...])
    h = _rms(x1, g_ref[...]).astype(MXU_DTYPE)
    acc = x1
    for c in range(D_FF // f_chunk):
        u = jnp.maximum(_dot(h, wu_ref[:, c * f_chunk:(c + 1) * f_chunk]), 0.0)
        acc = acc + _dot((u * u).astype(MXU_DTYPE), wd_ref[c * f_chunk:(c + 1) * f_chunk, :])
    o_ref[...] = acc


def _out_mlp_call(x, attn, w_out, g, w_up, w_down, tm=512, f_chunk=1024):
    t = x.shape[0]
    single = pl.Buffered(1)
    return pl.pallas_call(
        functools.partial(_out_mlp_kernel, f_chunk=f_chunk),
        name="out_mlp",
        out_shape=jax.ShapeDtypeStruct((t, D_MODEL), F32),
        grid=(t // tm,),
        in_specs=[
            _row_spec(tm, D_MODEL),
            _row_spec(tm, attn.shape[1]),
            pl.BlockSpec(w_out.shape, lambda i: (0, 0), pipeline_mode=single),
            _const_spec(g.shape),
            pl.BlockSpec(w_up.shape, lambda i: (0, 0), pipeline_mode=single),
            pl.BlockSpec(w_down.shape, lambda i: (0, 0), pipeline_mode=single),
        ],
        out_specs=_row_spec(tm, D_MODEL),
        compiler_params=pltpu.CompilerParams(dimension_semantics=("parallel",), vmem_limit_bytes=VMEM_LIMIT),
    )(x, attn, w_out, g, w_up, w_down)


def _swa_kernel(q_ref, kp_ref, kc_ref, vp_ref, vc_ref, bias_ref, sink_ref, o_ref):
    first_step = pl.program_id(1) == 0
    group_pairs = PAIRS // B_KV_HEADS
    w = group_pairs * BLK
    key = lax.broadcasted_iota(jnp.int32, (2 * BLK, w), 0)
    query = lax.broadcasted_iota(jnp.int32, (2 * BLK, w), 1) & (BLK - 1)
    rel = query + WINDOW - key
    in_band = (rel >= 0) & (rel < WINDOW)
    for blk in range(q_ref.shape[1]):
        k_prev, v_prev = (kp_ref, vp_ref) if blk == 0 else (kc_ref, vc_ref)
        prev = max(blk - 1, 0)
        ok = in_band & ((key >= WINDOW) | jnp.logical_not(first_step)) if blk == 0 else in_band
        for c in range(B_KV_HEADS):
            kk = jnp.concatenate([k_prev[0, prev, c * 2 * BLK:(c + 1) * 2 * BLK, :],
                                  kc_ref[0, blk, c * 2 * BLK:(c + 1) * 2 * BLK, :]], axis=0)
            s = _nt_dot(kk, q_ref[0, blk, c * w:(c + 1) * w, :])
            vt = jnp.concatenate([v_prev[0, prev, c * VT_ROWS:(c + 1) * VT_ROWS, :],
                                  vc_ref[0, blk, c * VT_ROWS:(c + 1) * VT_ROWS, :]], axis=1)
            halves = []
            for e in range(2):
                se = jnp.concatenate([s[e * BLK:(e + 1) * BLK], s[(2 + e) * BLK:(3 + e) * BLK]], axis=0)
                se = jnp.where(ok, se + bias_ref[c, e], NEG_INF)
                sink = sink_ref[c, e]
                m = jnp.maximum(jnp.max(se, axis=0, keepdims=True), sink)
                acc = _dot(vt, jnp.exp(se - m).astype(MXU_DTYPE))
                halves.append(acc[:HEAD_DIM] / (acc[HEAD_DIM:HEAD_DIM + 1] + jnp.exp(sink - m)))
            for pg in range(group_pairs):
                sl = slice(pg * BLK, (pg + 1) * BLK)
                pair = jnp.concatenate([halves[0][:, sl], halves[1][:, sl]], axis=0)
                p = c * group_pairs + pg
                o_ref[0, blk * BLK:(blk + 1) * BLK, p * LANES:(p + 1) * LANES] = (
                    jnp.transpose(pair).astype(o_ref.dtype))


def _swa_call(q, kc, vt, bias_t, sinks_t, batch, blocks_per_step=2):
    nb = q.shape[0] // batch
    g = blocks_per_step
    assert nb % g == 0
    per_batch = lambda a: a.reshape(batch, nb, *a.shape[1:])
    q, kc, vt = map(per_batch, (q, kc, vt))
    prev = lambda a: pl.BlockSpec((1, 1) + a.shape[2:], lambda bi, n: (bi, jnp.maximum(n * g - 1, 0), 0, 0))
    cur = lambda a: pl.BlockSpec((1, g) + a.shape[2:], lambda bi, n: (bi, n, 0, 0))
    return pl.pallas_call(
        _swa_kernel,
        name="swa_attn",
        out_shape=jax.ShapeDtypeStruct((batch, nb * BLK, HEADS * HEAD_DIM), MXU_DTYPE),
        grid=(batch, nb // g),
        in_specs=[cur(q), prev(kc), cur(kc), prev(vt), cur(vt),
                  pl.BlockSpec(bias_t.shape, lambda bi, n: (0, 0, 0, 0)),
                  pl.BlockSpec(sinks_t.shape, lambda bi, n: (0, 0, 0, 0))],
        out_specs=pl.BlockSpec((1, g * BLK, HEADS * HEAD_DIM), lambda bi, n: (bi, n, 0)),
        compiler_params=pltpu.CompilerParams(dimension_semantics=("parallel", "parallel"),
                                             vmem_limit_bytes=VMEM_LIMIT),
    )(q, kc, kc, vt, vt, bias_t, sinks_t)


def _pipeline_scratch(tk, n_chunks, lanes):
    return [pltpu.VMEM((2, 2, n_chunks, tk, lanes), F32), pltpu.VMEM((2, 2, n_chunks, tk, lanes), MXU_DTYPE),
            pltpu.VMEM((2, 2, n_chunks, 1, lanes), F32), pltpu.VMEM((2, n_chunks, 1, lanes), F32),
            pltpu.VMEM((2, n_chunks, VT_ROWS, lanes), F32)]


def _init_pipeline(s_scr, p_scr, alpha_scr, m_scr, acc_scr):
    m_scr[...] = jnp.full(m_scr.shape, NEG_INF, F32)
    acc_scr[...] = jnp.zeros(acc_scr.shape, F32)
    s_scr[1] = jnp.full(s_scr.shape[1:], NEG_INF, F32)
    p_scr[0] = jnp.zeros(p_scr.shape[1:], p_scr.dtype)
    alpha_scr[0] = jnp.ones(alpha_scr.shape[1:], F32)


def _softmax_stage(s_scr, p_scr, alpha_scr, m_scr, slot, chunks=(0,)):
    for e in range(2):
        for ch in chunks:
            s = s_scr[slot, e, ch]
            m_old = m_scr[e, ch]
            m_new = jnp.maximum(m_old, jnp.max(s, axis=0, keepdims=True))
            m_safe = jnp.where(m_new == NEG_INF, 0.0, m_new)
            p_scr[slot, e, ch] = jnp.exp2(s - m_safe).astype(p_scr.dtype)
            alpha_scr[slot, e, ch] = jnp.exp2(m_old - m_safe)
            m_scr[e, ch] = m_new


def _pv_stage(vts, p_scr, alpha_scr, acc_scr, slot, chunks=(0,)):
    for e in range(2):
        for ch in chunks:
            acc_scr[e, ch] = acc_scr[e, ch] * alpha_scr[slot, e, ch] + _dot(vts[e], p_scr[slot, e, ch])


def _normalized_pair(acc_scr, ch, lanes):
    a0, a1 = acc_scr[0, ch, :, lanes], acc_scr[1, ch, :, lanes]
    return jnp.concatenate([a0[:HEAD_DIM] / a0[HEAD_DIM:HEAD_DIM + 1],
                            a1[:HEAD_DIM] / a1[HEAD_DIM:HEAD_DIM + 1]], axis=0)


def _mla_kernel(q_ref, k_ref, vt_ref, o_ref, s_scr, p_scr, alpha_scr, m_scr, acc_scr, *, tq, tk):
    seq = q_ref.shape[1]
    n_diag = tq // tk
    sub = tk // BLK
    all_chunks = tuple(range(n_diag))
    below_diagonal = (lax.broadcasted_iota(jnp.int32, (tk, tk), 0)
                      <= lax.broadcasted_iota(jnp.int32, (tk, tk), 1))

    def q_tile(qt, _):
        q0 = pl.multiple_of(qt * tq, tq)
        qs = [q_ref[0, pl.ds(q0, tq), e * LANES:(e + 1) * LANES] for e in range(2)]
        n_full = qt * n_diag
        _init_pipeline(s_scr, p_scr, alpha_scr, m_scr, acc_scr)

        def vts(kt):
            kt = jnp.maximum(kt, 0)
            return [jnp.concatenate([vt_ref[0, kt * sub + c, e * VT_ROWS:(e + 1) * VT_ROWS, :]
                                     for c in range(sub)], axis=1) for e in range(2)]

        def qk_stage(kt, slot, diag):
            k0 = pl.multiple_of(kt * tk, tk)
            first = 0 if diag is None else diag
            for e in range(2):
                s = _nt_dot(k_ref[0, pl.ds(k0, tk), e * LANES:(e + 1) * LANES], qs[e][first * tk:])
                for ch in range(first, n_diag):
                    sc = s[:, (ch - first) * tk:(ch - first + 1) * tk]
                    if ch == diag:
                        sc = jnp.where(below_diagonal, sc, NEG_INF)
                    s_scr[slot, e, ch] = sc

        def chunks_of(c):
            return all_chunks[max(c, 0):]

        def full_steps(kk, carry):
            for u in range(n_diag):
                kt = n_diag * kk + u
                _pv_stage(vts(kt - 2), p_scr, alpha_scr, acc_scr, u % 2, all_chunks)
                qk_stage(kt, u % 2, None)
                _softmax_stage(s_scr, p_scr, alpha_scr, m_scr, 1 - u % 2, all_chunks)
            return carry

        lax.fori_loop(0, qt, full_steps, 0)
        for c in range(n_diag + 2):
            par = c % 2
            _pv_stage(vts(n_full + c - 2), p_scr, alpha_scr, acc_scr, par, chunks_of(c - 2))
            if c < n_diag:
                qk_stage(n_full + c, par, c)
            if c < n_diag + 1:
                _softmax_stage(s_scr, p_scr, alpha_scr, m_scr, 1 - par, chunks_of(c - 1))
        for ch in all_chunks:
            o_ref[0, pl.ds(pl.multiple_of(q0 + ch * tk, tk), tk), :] = jnp.transpose(
                _normalized_pair(acc_scr, ch, slice(None))).astype(o_ref.dtype)
        return 0

    lax.fori_loop(0, seq // tq, q_tile, 0)


def _mla_call(q, k, vt, tq=1024, tk=256):
    b, l, _ = q.shape
    tq, tk = min(tq, l), min(tk, l)
    assert (tq // tk) % 2 == 0 and l % tq == 0
    nb = l // BLK
    vt = vt.reshape(b, nb, HEADS * VT_ROWS, LANES)
    return pl.pallas_call(
        functools.partial(_mla_kernel, tq=tq, tk=tk),
        name="mla_attn",
        out_shape=jax.ShapeDtypeStruct((b, l, HEADS * HEAD_DIM), MXU_DTYPE),
        grid=(b, PAIRS),
        in_specs=[
            pl.BlockSpec((1, l, 2 * LANES), lambda bi, p: (bi, 0, p)),
            pl.BlockSpec((1, l, 2 * LANES), lambda bi, p: (bi, 0, p)),
            pl.BlockSpec((1, nb, 2 * VT_ROWS, LANES), lambda bi, p: (bi, 0, p, 0)),
        ],
        out_specs=pl.BlockSpec((1, l, LANES), lambda bi, p: (bi, 0, p)),
        scratch_shapes=_pipeline_scratch(tk, tq // tk, tk),
        compiler_params=pltpu.CompilerParams(dimension_semantics=("parallel", "parallel"),
                                             vmem_limit_bytes=VMEM_LIMIT),
    )(q, k, vt)


def _dsa_kernel(q_ref, qi_ref, wi_ref, kc_ref, vt_ref, kic_ref, bias_ref, tri_ref, o_ref,
                idx_scr, hi_scr, lo_scr, s_scr, p_scr, alpha_scr, m_scr, acc_scr, *, k_sel):
    i = pl.program_id(1)
    n_tiles = i + 1
    n_groups = i // TILE_GROUP + 1
    key_row = lax.broadcasted_iota(jnp.int32, (BLK, BLK), 0)
    query_col = lax.broadcasted_iota(jnp.int32, (BLK, BLK), 1)

    def admissible(j):
        return (j < i) | ((j == i) & (key_row <= query_col))

    wt = jnp.transpose(wi_ref[0])
    qi = qi_ref[0, 0]

    def score_group(g, carry):
        for u in range(TILE_GROUP):
            j = g * TILE_GROUP + u
            d = jnp.maximum(_nt_dot(kic_ref[0, j], qi), 0.0)
            sc = jnp.zeros((BLK, BLK), F32)
            for hp in range(IDX_HEADS // 2):
                for e in range(2):
                    h = 2 * hp + e
                    sc = sc + wt[h:h + 1, :] * d[e * BLK:(e + 1) * BLK, hp * BLK:(hp + 1) * BLK]
            sc = jnp.where(admissible(j), sc, NEG_INF)
            idx_scr[j] = sc
            bits = lax.bitcast_convert_type(sc, jnp.int32)
            key = jnp.where(bits >= 0, bits, bits ^ jnp.int32(0x7FFFFFFF))
            hi_scr[j] = lax.shift_right_arithmetic(key, 16).astype(jnp.int16)
            lo_scr[j] = ((key & 0xFFFF) - HALF_RANGE).astype(jnp.int16)
        return carry

    lax.fori_loop(0, n_groups, score_group, 0)

    n_blocks = idx_scr.shape[0]
    step = min(SEARCH_STEP, n_blocks)
    sizes = tuple(range(step, n_blocks + 1, step))
    lowest = jnp.full((BLK, BLK), -HALF_RANGE, jnp.int16)
    for u in range(step - TILE_GROUP):
        hi_scr[n_groups * TILE_GROUP + u] = lowest
        lo_scr[n_groups * TILE_GROUP + u] = lowest

    def count16(ref, cand, n_static):
        cand16 = cand.astype(jnp.int16)
        c = jnp.zeros((BLK, BLK), jnp.int16)
        for j in range(n_static):
            c = c + jnp.where(ref[j] >= cand16, jnp.int16(1), jnp.int16(0))
        return jnp.sum(c.astype(jnp.int32), axis=0, keepdims=True)

    def search16(ref, need, n_static):
        def bit_step(it, v):
            cand_v = v | lax.shift_left(jnp.int32(1), 15 - it)
            return jnp.where(count16(ref, cand_v - HALF_RANGE, n_static) >= need, cand_v, v)
        return lax.fori_loop(0, 16, bit_step, jnp.zeros((1, BLK), jnp.int32))

    def search(n_static):
        t_hi = search16(hi_scr, k_sel, n_static) - HALF_RANGE
        above = jnp.where(t_hi == HALF_RANGE - 1, 0,
                          count16(hi_scr, jnp.minimum(t_hi + 1, HALF_RANGE - 1), n_static))
        t_hi16 = t_hi.astype(jnp.int16)
        for j in range(n_static):
            lo_scr[j] = jnp.where(hi_scr[j] == t_hi16, lo_scr[j], jnp.int16(-HALF_RANGE))
        t_lo = search16(lo_scr, k_sel - above, n_static)
        key = lax.shift_left(t_hi, 16) | t_lo
        bits = jnp.where(key >= 0, key, key ^ jnp.int32(0x7FFFFFFF))
        t = lax.bitcast_convert_type(bits, F32)
        return jnp.where(t != t, NEG_INF, t)

    def run_search():
        which = (n_groups * TILE_GROUP + step - 1) // step - 1
        t = lax.switch(which, [functools.partial(search, n) for n in sizes])

        def body(g, c):
            gt, ge = c
            for u in range(TILE_GROUP):
                s = idx_scr[g * TILE_GROUP + u]
                gt = gt + jnp.where(s > t, 1.0, 0.0)
                ge = ge + jnp.where(s >= t, 1.0, 0.0)
            return gt, ge
        zero = jnp.zeros((BLK, BLK), F32)
        gt, ge = lax.fori_loop(0, n_groups, body, (zero, zero))
        return t, jnp.sum(gt, axis=0, keepdims=True), jnp.sum(ge, axis=0, keepdims=True)

    def no_search():
        zero = jnp.zeros((1, BLK), F32)
        return jnp.full((1, BLK), NEG_INF, F32), zero, zero

    thr, n_gt, n_ge = lax.cond(n_tiles * BLK > k_sel, run_search, no_search)

    tied = (n_ge > k_sel) & (thr > NEG_INF)

    @pl.when(jnp.max(jnp.where(tied, 1.0, 0.0)) > 0.0)
    def _():
        need = k_sel - n_gt

        def tie_tile(j, seen):
            s = idx_scr[j]
            eq = s == thr
            eq_f = jnp.where(eq, 1.0, 0.0).astype(MXU_DTYPE)
            rank = seen + _dot(tri_ref[...], eq_f)
            idx_scr[j] = jnp.where(eq & tied & (rank > need), NEG_INF, s)
            return seen + jnp.sum(eq_f.astype(F32), axis=0, keepdims=True)

        lax.fori_loop(0, n_tiles, tie_tile, jnp.zeros((1, BLK), F32))

    q = q_ref[0, 0]
    _init_pipeline(s_scr, p_scr, alpha_scr, m_scr, acc_scr)

    def qk_stage(j, slot, near, valid=None):
        sel = idx_scr[j] >= thr
        if near is not None:
            sel = sel & admissible(j)
        if valid is not None:
            sel = sel & valid
        mask = jnp.where(sel, 0.0, NEG_INF)
        mask = jnp.concatenate([mask] * PAIRS, axis=1)
        s = _nt_dot(kc_ref[0, j], q)
        if near is not None:
            s = s + bias_ref[near]
        for e in range(2):
            s_scr[slot, e, 0] = s[e * BLK:(e + 1) * BLK] + mask

    def vts(j):
        vt = vt_ref[0, jnp.maximum(j, 0)]
        return [vt, vt]

    n_far = jnp.maximum(i - 1, 0)
    n_trips = (n_far + TILE_GROUP - 1) // TILE_GROUP

    def far_steps(kk, carry):
        for u in range(TILE_GROUP):
            j = TILE_GROUP * kk + u
            _pv_stage(vts(j - 2), p_scr, alpha_scr, acc_scr, u % 2)
            qk_stage(jnp.minimum(j, i), u % 2, None, j < n_far)
            _softmax_stage(s_scr, p_scr, alpha_scr, m_scr, 1 - u % 2)
        return carry

    lax.fori_loop(0, n_trips, far_steps, 0)
    j0 = TILE_GROUP * n_trips
    tail = [(n_far, 0, i >= 1), (i, 1, None)]
    done = [jnp.minimum(j0 - 2, i), jnp.minimum(j0 - 1, i)] + [t[0] for t in tail]
    for c in range(len(tail) + 2):
        par = c % 2
        _pv_stage(vts(done[c]), p_scr, alpha_scr, acc_scr, par)
        if c < len(tail):
            qk_stage(tail[c][0], par, tail[c][1], tail[c][2])
        if c < len(tail) + 1:
            _softmax_stage(s_scr, p_scr, alpha_scr, m_scr, 1 - par)

    for p in range(PAIRS):
        sl = slice(p * BLK, (p + 1) * BLK)
        o_ref[0, :, sl] = jnp.transpose(_normalized_pair(acc_scr, 0, sl)).astype(o_ref.dtype)


def _dsa_call(q, qi, wi, kc, vt, kic, bias_t, tri, batch):
    nb = q.shape[0] // batch
    l = nb * BLK
    k_sel = min(TOPK_MAX, l // 4)
    assert nb % TILE_GROUP == 0
    per_batch =lambda a: a.reshape(batch, nb, *a.shape[1:])
    q, qi, kc, vt, kic = map(per_batch, (q, qi, kc, vt, kic))
    blk = lambda a: pl.BlockSpec((1, 1) + a.shape[2:], lambda bi, n: (bi, n, 0, 0))
    seq = lambda a: pl.BlockSpec((1,) + a.shape[1:], lambda bi, n: (bi, 0, 0, 0))
    return pl.pallas_call(
        functools.partial(_dsa_kernel, k_sel=k_sel),
        name="dsa_attn",
        out_shape=jax.ShapeDtypeStruct((batch, l, HEADS * HEAD_DIM), MXU_DTYPE),
        grid=(batch, nb),
        in_specs=[
            blk(q),
            blk(qi),
            pl.BlockSpec((1, BLK, LANES), lambda bi, n: (bi, n, 0)),
            seq(kc),
            seq(vt),
            seq(kic),
            pl.BlockSpec(bias_t.shape, lambda bi, n: (0, 0, 0)),
            pl.BlockSpec(tri.shape, lambda bi, n: (0, 0)),
        ],
        out_specs=pl.BlockSpec((1, BLK, HEADS * HEAD_DIM), lambda bi, n: (bi, n, 0)),
        scratch_shapes=[pltpu.VMEM((nb, BLK, BLK), F32), pltpu.VMEM((nb + SEARCH_STEP, BLK, BLK), jnp.int16),
                        pltpu.VMEM((nb + SEARCH_STEP, BLK, BLK), jnp.int16)]
        + _pipeline_scratch(BLK, 1, PAIRS * BLK),
        compiler_params=pltpu.CompilerParams(dimension_semantics=("parallel", "arbitrary"),
                                             vmem_limit_bytes=VMEM_LIMIT),
    )(q, qi, wi.reshape(batch, l, LANES), kc, vt, kic, bias_t, tri)


def _t5_bucket(rel):
    n = jnp.maximum(rel, 0)
    max_exact = N_BUCKETS // 2
    nf = jnp.maximum(n, 1).astype(F32)
    large = max_exact + (jnp.log(nf / max_exact) / math.log(MAX_DISTANCE / max_exact)
                         * (N_BUCKETS - max_exact)).astype(jnp.int32)
    large = jnp.minimum(large, N_BUCKETS - 1)
    return jnp.where(n < max_exact, n, large)


def _band_bias(rel_bias):
    qi = jnp.arange(BLK, dtype=jnp.int32)[:, None] + BLK
    kj = jnp.arange(2 * BLK, dtype=jnp.int32)[None, :]
    one_hot = (_t5_bucket(qi - kj)[None] == jnp.arange(N_BUCKETS, dtype=jnp.int32)[:, None, None]).astype(F32)
    return jnp.einsum("nh,nqk->hqk", rel_bias, one_hot, precision=lax.Precision.HIGHEST)


def _pair_gain(g, scale=1.0):
    return (jnp.concatenate([g, g]) * scale).reshape(1, LANES).astype(F32)


def _dup(w, n_heads):
    d = w.shape[0]
    w = w.reshape(d, n_heads, 1, HEAD_DIM)
    return jnp.broadcast_to(w, (d, n_heads, 2, HEAD_DIM)).reshape(d, n_heads * LANES)


def _layer_a(x, g_mix, w_in, q_gain, k_gain):
    nq = HEADS * HEAD_DIM
    sizes = [nq, HEAD_DIM, HEAD_DIM, IDX_HEADS * IDX_DIM, IDX_DIM, IDX_HEADS]
    wq, wk, wv, wqi, wki, wwi = jnp.split(w_in, np.cumsum(sizes)[:-1].tolist(), axis=1)
    pad_to_lanes = lambda a: jnp.pad(a, ((0, 0), (0, LANES - a.shape[1])))
    w = jnp.concatenate([wq, _dup(wk, 1), pad_to_lanes(wv), wqi, _dup(wki, 1), pad_to_lanes(wwi)],
                        axis=1).astype(MXU_DTYPE)
    consts = [g_mix.reshape(1, -1), w, _pair_gain(q_gain, HEAD_DIM ** -0.5 * LOG2_E), _pair_gain(k_gain)]
    outs = [("stacked", PAIRS * BLK, MXU_DTYPE), ("stacked", 2 * BLK, MXU_DTYPE), ("stacked", VT_ROWS, MXU_DTYPE),
            ("stacked", IDX_HEADS // 2 * BLK, MXU_DTYPE), ("stacked", 2 * BLK, MXU_DTYPE), ("rows", LANES, F32)]
    return _proj_call(_proj_a_kernel, x, consts, [], outs)


def _layer_b(x, g_mix, w_in, q_gain, k_gain):
    nq = HEADS * HEAD_DIM
    nkv = B_KV_HEADS * HEAD_DIM
    wq, wk, wv = jnp.split(w_in, [nq, nq + nkv], axis=1)
    wv = jnp.pad(wv.reshape(-1, B_KV_HEADS, HEAD_DIM), ((0, 0), (0, 0), (0, LANES - HEAD_DIM)))
    w = jnp.concatenate([wq, _dup(wk, B_KV_HEADS), wv.reshape(-1, B_KV_HEADS * LANES)], axis=1).astype(MXU_DTYPE)
    consts = [g_mix.reshape(1, -1), w, _pair_gain(q_gain, HEAD_DIM ** -0.5), _pair_gain(k_gain)]
    outs = [("stacked", PAIRS * BLK, MXU_DTYPE), ("stacked", B_KV_HEADS * 2 * BLK, MXU_DTYPE),
            ("stacked", B_KV_HEADS * VT_ROWS, MXU_DTYPE)]
    return _proj_call(_proj_b_kernel, x, consts, [], outs)


_half = C_ROPE // 2
MLA_LANE = np.concatenate([np.arange(LANES // 2 - _half), LANES // 2 + np.arange(C_NOPE - (LANES // 2 - _half)),
                           LANES // 2 - _half + np.arange(_half), LANES - _half + np.arange(_half)])


def _to_mla_lanes(a):
    n = a.shape[-1]
    lanes = MLA_LANE[:n] if n != C_ROPE else MLA_LANE[C_NOPE:]
    place = np.zeros((n, LANES), np.float32)
    place[np.arange(n), lanes] = 1.0
    return jnp.einsum("...n,nl->...l", a, jnp.asarray(place, a.dtype), precision=lax.Precision.HIGHEST)


def _rope_tables(cos, sin, gain):
    ones = jnp.ones(cos.shape[:-1] + (C_NOPE,), F32)
    own = _to_mla_lanes(jnp.concatenate([ones, cos, cos], axis=-1))
    partner = _to_mla_lanes(jnp.concatenate([-sin, sin], axis=-1))
    g = _to_mla_lanes(gain.astype(F32))
    return own * g, partner * jnp.roll(g, LANES // 2)


def _layer_c(x, g_mix, w_in, q_a_gain, w_q_b, kv_a_gain, w_kv_b, tables):
    d = w_in.shape[0]
    heads = lambda w, n: _to_mla_lanes(w.reshape(w.shape[0], HEADS, n)).reshape(w.shape[0], HEADS * LANES)
    w_in_p = jnp.concatenate([w_in[:, :C_Q_RANK + C_KV_RANK], _to_mla_lanes(w_in[:, C_Q_RANK + C_KV_RANK:])],
                             axis=1).astype(MXU_DTYPE)
    wq = heads(w_q_b, C_QK).astype(MXU_DTYPE)
    w_kv = w_kv_b.reshape(C_KV_RANK, HEADS, C_NOPE + HEAD_DIM)
    wk = heads(w_kv[:, :, :C_NOPE].reshape(C_KV_RANK, -1), C_NOPE).astype(MXU_DTYPE)
    wv = w_kv[:, :, C_NOPE:].reshape(C_KV_RANK, -1).astype(MXU_DTYPE)
    consts = [g_mix.reshape(1, -1), w_in_p, q_a_gain.reshape(1, -1), kv_a_gain.reshape(1, -1), wq, wk, wv]
    outs = [("rows", HEADS * LANES, MXU_DTYPE), ("rows", HEADS * LANES, MXU_DTYPE),
            ("stacked", HEADS * VT_ROWS, MXU_DTYPE)]
    return _proj_call(_proj_c_kernel, x, consts, list(tables), outs)


def kernel(x, positions, rel_bias, norm_mix, norm_mlp, w_up, w_down, a_w_in, a_q_gain, a_k_gain, a_w_out,
           b_w_in, b_q_gain, b_k_gain, b_sinks, b_w_out, c_w_in, c_q_a_gain, c_w_q_b, c_kv_a_gain, c_w_kv_b,
           c_q_gain, c_k_gain, c_w_out):
    b, l, d = x.shape
    t = b * l
    depth = norm_mix.shape[0]

    band = _band_bias(rel_bias.astype(F32))
    far = rel_bias[N_BUCKETS - 1].astype(F32)[:, None, None]
    bias_t = jnp.transpose(((band - far) * LOG2_E).reshape(PAIRS, 2, BLK, 2, BLK), (3, 1, 4, 0, 2)).reshape(
        2, 2 * BLK, PAIRS * BLK)
    tri = jnp.asarray(np.arange(BLK)[:, None] >= np.arange(BLK)[None, :], MXU_DTYPE)

    inv_freq = ROPE_THETA ** (-jnp.arange(0, C_ROPE, 2, dtype=F32) / C_ROPE)
    ang = positions.astype(F32).reshape(t, 1) * inv_freq
    cos, sin = jnp.cos(ang), jnp.sin(ang)

    xf = x.reshape(t, d).astype(F32)
    ia = ib = ic = 0
    for i in range(depth):
        kind = i % 3
        g_mix = norm_mix[i]
        if kind == 0:
            q, kc, vt, qi, kic, wi = _layer_a(xf, g_mix, a_w_in[ia], a_q_gain[ia], a_k_gain[ia])
            attn = _dsa_call(q, qi, wi, kc, vt, kic, bias_t, tri, b)
            w_out = a_w_out[ia]
            ia += 1
        elif kind == 1:
            q, kc, vt = _layer_b(xf, g_mix, b_w_in[ib], b_q_gain[ib], b_k_gain[ib])
            group_pairs = PAIRS // B_KV_HEADS
            swa_bias = jnp.transpose(band.reshape(B_KV_HEADS, group_pairs, 2, BLK, 2 * BLK),
                                     (0, 2, 4, 1, 3)).reshape(B_KV_HEADS, 2, 2 * BLK, group_pairs * BLK)
            sinks = jnp.transpose(b_sinks[ib].astype(F32).reshape(B_KV_HEADS, group_pairs, 2), (0, 2, 1))
            sinks = jnp.broadcast_to(sinks[..., None], (B_KV_HEADS, 2, group_pairs, BLK)).reshape(
                B_KV_HEADS, 2, 1, group_pairs * BLK)
            attn = _swa_call(q, kc, vt, swa_bias, sinks, b)
            w_out = b_w_out[ib]
            ib += 1
        else:
            tables = (_rope_tables(cos, sin, c_q_gain[ic] * (C_QK ** -0.5 * LOG2_E))
                      + _rope_tables(cos, sin, c_k_gain[ic]))
            q, k, vt = _layer_c(xf, g_mix, c_w_in[ic], c_q_a_gain[ic], c_w_q_b[ic], c_kv_a_gain[ic],
                               c_w_kv_b[ic], tables)
            r3 = lambda a: a.reshape(b, l, a.shape[-1])
            attn = _mla_call(r3(q), r3(k), vt)
            w_out = c_w_out[ic]
            ic += 1
        xf = _out_mlp_call(xf, attn.reshape(t, -1), w_out.astype(MXU_DTYPE), norm_mlp[i].reshape(1, -1),
                           w_up[i].astype(MXU_DTYPE), w_down[i].astype(MXU_DTYPE))
    return xf.reshape(b, l, d).astype(x.dtype)
```
